```python
import math, functools
import jax, jax.numpy as jnp
from jax import lax
import numpy as np

D_MODEL = 1024
BATCH = 2
SEQ = 8192
DEPTH = 4
DEC_BATCH = 32
DEC_SEQ = 4
PAST_LEN = 8192
PAGE_SIZE = 128

D_MIX = D_MODEL
RWKV_WIDTH = D_MIX // 2
DIFF_WIDTH = D_MIX - RWKV_WIDTH
RWKV_HEAD = 64
RWKV_HEADS = RWKV_WIDTH // RWKV_HEAD
DECAY_LORA = 64
ICLR_LORA = 64
DIFF_QK = 64
DIFF_HEADS = DIFF_WIDTH // (2 * DIFF_QK)
DIFF_V = 2 * DIFF_QK
NUM_BUCKETS = 32
MAX_DISTANCE = 128
Q_BLOCK = 128
RMS_EPS = 1e-6
SUBLN_EPS = 1e-5
GN_EPS = 64e-5
NEG_INF = -1e30
SHIFT_WIDTH = 3 * RWKV_WIDTH + DECAY_LORA + ICLR_LORA
IN_SIZES = (SHIFT_WIDTH, RWKV_WIDTH, DIFF_WIDTH, DIFF_WIDTH, DIFF_WIDTH, DIFF_WIDTH)
IN_WIDTH = sum(IN_SIZES)
IN_SPLITS = tuple(int(s) for s in np.cumsum(IN_SIZES)[:-1])
RWKV_SPLITS = tuple(int(s) for s in np.cumsum((RWKV_WIDTH, RWKV_WIDTH, RWKV_WIDTH, DECAY_LORA)))
N_PAGES = PAST_LEN // PAGE_SIZE
N_POOL_PAGES = (5 * DEC_BATCH * N_PAGES + 3) // 4
F32 = jnp.float32

kernel_name = 'hymba_rwkv7_diffattn_step'


def rms_norm(x, g, eps):
    xf = x.astype(F32)
    y = xf * lax.rsqrt(jnp.mean(xf * xf, axis=-1, keepdims=True) + eps)
    return (y * g.astype(F32)).astype(x.dtype)


def rel_bucket(n):
    n = jnp.maximum(n, 0)
    max_exact = NUM_BUCKETS // 2
    nf = jnp.maximum(n, 1).astype(F32)
    large = max_exact + (jnp.log(nf / max_exact) / math.log(MAX_DISTANCE / max_exact)
                         * (NUM_BUCKETS - max_exact)).astype(jnp.int32)
    large = jnp.minimum(large, NUM_BUCKETS - 1)
    return jnp.where(n < max_exact, n, large)


def rwkv7_group(zsh, shift_prev, wkv_prev, mu, w0, w_up, a0, a_up, k_k, k_a, r_k, lnx_w, lnx_b):
    B, T, _ = zsh.shape
    dt = zsh.dtype
    zf = zsh.astype(F32)
    prev = jnp.concatenate([shift_prev[:, None].astype(F32), zf[:, :-1]], axis=1)
    zm = zf + mu.astype(F32) * (prev - zf)
    r, k, v, zw, za = jnp.split(zm, RWKV_SPLITS, axis=-1)
    w = -jax.nn.softplus(-(w0.astype(F32) + jnp.tanh(zw) @ w_up.astype(F32))) - 0.5
    decay = jnp.exp(-jnp.exp(w))
    a = jax.nn.sigmoid(a0.astype(F32) + za @ a_up.astype(F32))
    heads = lambda t: t.reshape(B, T, RWKV_HEADS, RWKV_HEAD)
    kk = heads(k * k_k.astype(F32))
    kk = kk / jnp.maximum(jnp.sqrt(jnp.sum(kk * kk, axis=-1, keepdims=True)), 1e-12)
    k = k * (1.0 + (a - 1.0) * k_a.astype(F32))
    r, k, v, decay, a = heads(r), heads(k), heads(v), heads(decay), heads(a)
    a_vec = -kk
    b_vec = kk * a

    def step(S, inp):
        r_t, w_t, k_t, v_t, a_t, b_t = inp
        sa = jnp.einsum('bhvk,bhk->bhv', S, a_t)
        S = (S * w_t[:, :, None, :] + sa[..., None] * b_t[:, :, None, :]
             + v_t[..., None] * k_t[:, :, None, :])
        return S, jnp.einsum('bhvk,bhk->bhv', S, r_t)

    tm = lambda t: jnp.moveaxis(t, 1, 0)
    S_fin, ys = lax.scan(step, wkv_prev.astype(F32),
                         (tm(r), tm(decay), tm(k), tm(v), tm(a_vec), tm(b_vec)))
    y = jnp.moveaxis(ys, 0, 1)
    mean = jnp.mean(y, axis=-1, keepdims=True)
    var = jnp.mean(jnp.square(y - mean), axis=-1, keepdims=True)
    y = ((y - mean) * lax.rsqrt(var + GN_EPS)).reshape(B, T, RWKV_WIDTH)
    y = y * lnx_w.astype(F32) + lnx_b.astype(F32)
    bonus = jnp.sum(r * k * r_k.astype(F32), axis=-1, keepdims=True) * v
    y = y + bonus.reshape(B, T, RWKV_WIDTH)
    return y.astype(dt), zsh[:, -1], S_fin.astype(dt)


def diff_attn_core(q, k, v, q_pos, k_pos, lam, rel_bias):
    logits = jnp.einsum('bqhcd,bkhcd->bhcqk', q.astype(F32), k.astype(F32)) * (DIFF_QK ** -0.5)
    dist = q_pos[:, None] - k_pos[None, :]
    bias = jnp.transpose(rel_bias.astype(F32)[rel_bucket(dist)], (2, 0, 1))
    logits = jnp.where(dist >= 0, logits + bias[None, :, None], NEG_INF)
    p = jax.nn.softmax(logits, axis=-1)
    p = p[:, :, 0] - lam * p[:, :, 1]
    return jnp.einsum('bhqk,bkhe->bqhe', p, v.astype(F32))


def attend_prompt(q, k, v, lam, rel_bias):
    B, T = q.shape[0], q.shape[1]
    n_blk = T // Q_BLOCK
    pos = jnp.arange(T, dtype=jnp.int32)
    qb = jnp.swapaxes(q.reshape(B, n_blk, Q_BLOCK, DIFF_HEADS, 2, DIFF_QK), 0, 1)
    pb = pos.reshape(n_blk, Q_BLOCK)
    out = lax.map(lambda qp: diff_attn_core(qp[0], k, v, qp[1], pos, lam, rel_bias), (qb, pb))
    return jnp.swapaxes(out, 0, 1).reshape(B, T, DIFF_HEADS, DIFF_V)


def attend_sample(q, k, v, lam, cache_k, cache_v, page_table, l, rel_bias):
    B, T = q.shape[0], q.shape[1]
    past = page_table.shape[1] * PAGE_SIZE
    past_k = cache_k[l, page_table].reshape(B, past, DIFF_HEADS, 2, DIFF_QK)
    past_v = cache_v[l, page_table].reshape(B, past, DIFF_HEADS, DIFF_V)
    k_all = jnp.concatenate([past_k, k.astype(past_k.dtype)], axis=1)
    v_all = jnp.concatenate([past_v, v.astype(past_v.dtype)], axis=1)
    k_pos = jnp.arange(past + T, dtype=jnp.int32)
    q_pos = past + jnp.arange(T, dtype=jnp.int32)
    return diff_attn_core(q, k_all, v_all, q_pos, k_pos, lam, rel_bias)


def mixer_layer(x, l, shift_prev, wkv_prev, attend, P):
    B, T, _ = x.shape
    dt = x.dtype
    h = rms_norm(x, P['pre_norm'][l], RMS_EPS)
    z = h @ P['w_in'][l]
    zsh, g_r, q, k, v, g_d = jnp.split(z, IN_SPLITS, axis=-1)
    y_r, shift_new, wkv_new = rwkv7_group(
        zsh, shift_prev, wkv_prev, P['mu_shift'][l], P['w0'][l], P['w_up'][l], P['a0'][l],
        P['a_up'][l], P['k_k'][l], P['k_a'][l], P['r_k'][l], P['lnx_w'][l], P['lnx_b'][l])
    q = q.reshape(B, T, DIFF_HEADS, 2, DIFF_QK)
    k = k.reshape(B, T, DIFF_HEADS, 2, DIFF_QK)
    v = v.reshape(B, T, DIFF_HEADS, DIFF_V)
    lam_init = 0.8 - 0.6 * math.exp(-0.3 * l)
    lam = (jnp.exp(jnp.sum(P['lam_q1'][l].astype(F32) * P['lam_k1'][l].astype(F32)))
           - jnp.exp(jnp.sum(P['lam_q2'][l].astype(F32) * P['lam_k2'][l].astype(F32))) + lam_init)
    o = attend(q, k, v, lam)
    o = rms_norm(o, P['subln'][l], SUBLN_EPS) * (1.0 - lam_init)
    y_d = o.reshape(B, T, DIFF_WIDTH).astype(dt)
    mixed = jnp.concatenate([y_r * jax.nn.silu(g_r), y_d * jax.nn.silu(g_d)], axis=-1)
    out = mixed @ P['w_out'][l]
    x = x + rms_norm(out, P['post_norm'][l], RMS_EPS)
    return x, shift_new, wkv_new, k.reshape(B, T, DIFF_HEADS, 2 * DIFF_QK), v


def setup_inputs(seed: int = 0) -> dict:
    key = jax.random.key(seed)
    ks = jax.random.split(key, 32)
    nrm = lambda i, shape, s: s * jax.random.normal(ks[i], shape, F32)
    perm = jax.random.permutation(ks[31], N_POOL_PAGES)[:DEC_BATCH * N_PAGES]
    return {
        'x_prompt': nrm(0, (BATCH, SEQ, D_MODEL), 1.0),
        'x_sample': nrm(1, (DEC_BATCH, DEC_SEQ, D_MODEL), 1.0),
        'cache_k': nrm(2, (DEPTH, N_POOL_PAGES, PAGE_SIZE, DIFF_HEADS, 2 * DIFF_QK), 1.0),
        'cache_v': nrm(3, (DEPTH, N_POOL_PAGES, PAGE_SIZE, DIFF_HEADS, DIFF_V), 1.0),
        'state_wkv': nrm(4, (DEPTH, DEC_BATCH, RWKV_HEADS, RWKV_HEAD, RWKV_HEAD), 1.0),
        'state_shift': nrm(5, (DEPTH, DEC_BATCH, SHIFT_WIDTH), 1.0),
        'page_table': perm.reshape(DEC_BATCH, N_PAGES).astype(jnp.int32),
        'rel_bias': nrm(6, (NUM_BUCKETS, DIFF_HEADS), 0.5),
        'pre_norm': 1.0 + nrm(7, (DEPTH, D_MODEL), 0.02),
        'post_norm': 1.0 + nrm(8, (DEPTH, D_MODEL), 0.02),
        'w_in': nrm(9, (DEPTH, D_MODEL, IN_WIDTH), D_MODEL ** -0.5),
        'mu_shift': jax.random.uniform(ks[10], (DEPTH, SHIFT_WIDTH), F32),
        'w0': jax.random.uniform(ks[11], (DEPTH, RWKV_WIDTH), F32, -4.0, 0.0),
        'w_up': nrm(12, (DEPTH, DECAY_LORA, RWKV_WIDTH), 0.5 * DECAY_LORA ** -0.5),
        'a0': nrm(13, (DEPTH, RWKV_WIDTH), 0.1),
        'a_up': nrm(14, (DEPTH, ICLR_LORA, RWKV_WIDTH), 0.5 * ICLR_LORA ** -0.5),
        'k_k': 1.0 + nrm(15, (DEPTH, RWKV_WIDTH), 0.1),
        'k_a': 1.0 + nrm(16, (DEPTH, RWKV_WIDTH), 0.1),
        'r_k': nrm(17, (DEPTH, RWKV_HEADS, RWKV_HEAD), 0.1),
        'lnx_w': 1.0 + nrm(18, (DEPTH, RWKV_WIDTH), 0.02),
        'lnx_b': nrm(19, (DEPTH, RWKV_WIDTH), 0.02),
        'lam_q1': nrm(20, (DEPTH, DIFF_QK), 0.1),
        'lam_k1': nrm(21, (DEPTH, DIFF_QK), 0.1),
        'lam_q2': nrm(22, (DEPTH, DIFF_QK), 0.1),
        'lam_k2': nrm(23, (DEPTH, DIFF_QK), 0.1),
        'subln': 1.0 + nrm(24, (DEPTH, DIFF_V), 0.02),
        'w_out': nrm(25, (DEPTH, D_MIX, D_MODEL), D_MIX ** -0.5),
    }


def reference(x_prompt, x_sample, cache_k, cache_v, state_wkv, state_shift, page_table,
              rel_bias, pre_norm, post_norm, w_in, mu_shift, w0, w_up, a0, a_up, k_k, k_a,
              r_k, lnx_w, lnx_b, lam_q1, lam_k1, lam_q2, lam_k2, subln, w_out):
    P = dict(pre_norm=pre_norm, post_norm=post_norm, w_in=w_in, mu_shift=mu_shift, w0=w0,
             w_up=w_up, a0=a0, a_up=a_up, k_k=k_k, k_a=k_a, r_k=r_k, lnx_w=lnx_w,
             lnx_b=lnx_b, lam_q1=lam_q1, lam_k1=lam_k1, lam_q2=lam_q2, lam_k2=lam_k2,
             subln=subln, w_out=w_out)
    Bp = x_prompt.shape[0]
    zero_shift = jnp.zeros((Bp, SHIFT_WIDTH), x_prompt.dtype)
    zero_wkv = jnp.zeros((Bp, RWKV_HEADS, RWKV_HEAD, RWKV_HEAD), x_prompt.dtype)
    prompt_attend = functools.partial(attend_prompt, rel_bias=rel_bias)
    yp, ys = x_prompt, x_sample
    kp_l, vp_l, wp_l, sp_l, ks_l, vs_l, ws_l, ss_l = [], [], [], [], [], [], [], []
    for l in range(DEPTH):
        yp, sp, wp, kp, vp = mixer_layer(yp, l, zero_shift, zero_wkv, prompt_attend, P)
        sample_attend = functools.partial(attend_sample, cache_k=cache_k, cache_v=cache_v,
                                          page_table=page_table, l=l, rel_bias=rel_bias)
        ys, ss, ws, k_s, v_s = mixer_layer(ys, l, state_shift[l], state_wkv[l], sample_attend, P)
        kp_l.append(kp); vp_l.append(vp); wp_l.append(wp); sp_l.append(sp)
        ks_l.append(k_s); vs_l.append(v_s); ws_l.append(ws); ss_l.append(ss)
    return (yp, ys, jnp.stack(kp_l), jnp.stack(vp_l), jnp.stack(wp_l), jnp.stack(sp_l),
            jnp.stack(ks_l), jnp.stack(vs_l), jnp.stack(ws_l), jnp.stack(ss_l))
```

```python
import functools
import math

import jax
import jax.numpy as jnp
from jax import lax
from jax.experimental import pallas as pl
from jax.experimental.pallas import tpu as pltpu

F32 = jnp.float32
BF16 = jnp.bfloat16

RWKV_WIDTH = 512
DIFF_WIDTH = 512
HEAD = 64
RWKV_HEADS = RWKV_WIDTH // HEAD
LORA = 64
DIFF_HEADS = 4
DIFF_V = 128
NUM_BUCKETS = 32
MAX_EXACT = NUM_BUCKETS // 2
MAX_DISTANCE = 128
PAGE = 128
RMS_EPS = 1e-6
SUBLN_EPS = 1e-5
GN_EPS = 64e-5
NEG_INF = -1e30
SHIFT_WIDTH = 3 * RWKV_WIDTH + 2 * LORA
SEG_SHIFT = (0, SHIFT_WIDTH)
SEG_GR = (SHIFT_WIDTH, SHIFT_WIDTH + 512)
SEG_Q = (SEG_GR[1], SEG_GR[1] + 512)
SEG_K = (SEG_Q[1], SEG_Q[1] + 512)
SEG_V = (SEG_K[1], SEG_K[1] + 512)
SEG_GD = (SEG_V[1], SEG_V[1] + 512)

VMEM_LIMIT_V7X = 48 * 1024 * 1024
LANES = 128
GROUP_HEADS = 4
GROUP_W = GROUP_HEADS * HEAD
N_GROUPS = RWKV_HEADS // GROUP_HEADS


def _dot(a, b):
    return jnp.dot(a.astype(BF16), b.astype(BF16), preferred_element_type=F32)


def _dot_nt(a, b):
    return lax.dot_general(a.astype(BF16), b.astype(BF16), (((1,), (1,)), ((), ())),
                           preferred_element_type=F32)


def _dot_tn(a, b):
    return lax.dot_general(a.astype(BF16), b.astype(BF16), (((0,), (0,)), ((), ())),
                           preferred_element_type=F32)


def _split(x):
    hi = x.astype(BF16)
    lo = (x - hi.astype(F32)).astype(BF16)
    return hi, lo


def _dot_exact_rhs(a, b_exact):
    hi, lo = _split(a)
    return (jnp.dot(hi, b_exact, preferred_element_type=F32)
            + jnp.dot(lo, b_exact, preferred_element_type=F32))


def _dot_exact_lhs(a_exact, b):
    hi, lo = _split(b)
    return (jnp.dot(a_exact, hi, preferred_element_type=F32)
            + jnp.dot(a_exact, lo, preferred_element_type=F32))


def _sigmoid(x):
    return 1.0 / (1.0 + jnp.exp(-x))


def _softplus(x):
    return jnp.maximum(x, 0.0) + jnp.log(1.0 + jnp.exp(-jnp.abs(x)))


def _rel_bias_minus_far(dist, rb_ref, head):
    n = jnp.maximum(dist, 0)
    nf = jnp.maximum(n, 1).astype(F32)
    large = MAX_EXACT + (jnp.log(nf / MAX_EXACT) / math.log(MAX_DISTANCE / MAX_EXACT)
                         * (NUM_BUCKETS - MAX_EXACT)).astype(jnp.int32)
    large = jnp.minimum(large, NUM_BUCKETS - 1)
    bucket = jnp.where(n < MAX_EXACT, n, large)
    far = rb_ref[NUM_BUCKETS - 1, head]
    out = jnp.zeros(dist.shape, F32)
    for b in range(NUM_BUCKETS - 1):
        out = jnp.where(bucket == b, rb_ref[b, head] - far, out)
    return out


def _inproj_kernel(x_ref, g_ref, w_ref, zsh_ref, gr_ref, q_ref, k_ref, v_ref, kb_ref, vb_ref,
                   gd_ref):
    x = x_ref[...]
    ms = jnp.mean(x * x, axis=-1, keepdims=True)
    h = (x * lax.rsqrt(ms + RMS_EPS) * g_ref[...]).astype(BF16)

    def seg(lo_hi):
        lo, hi = lo_hi
        return jnp.dot(h, w_ref[:, lo:hi], preferred_element_type=F32)

    zsh_ref[...] = seg(SEG_SHIFT)
    gr_ref[...] = seg(SEG_GR)
    q_ref[...] = (seg(SEG_Q) * (HEAD ** -0.5)).astype(q_ref.dtype)
    k = seg(SEG_K)
    k_ref[...] = k
    kb_ref[...] = k.astype(BF16)
    v = seg(SEG_V)
    v_ref[...] = v
    vb_ref[...] = v.astype(BF16)
    gd_ref[...] = seg(SEG_GD)


def _inproj(x, g, w_bf16, tm, q_dtype):
    m, d = x.shape
    n = w_bf16.shape[1]
    row = lambda width: pl.BlockSpec((tm, width), lambda i: (i, 0))
    out_shape = (
        jax.ShapeDtypeStruct((m, SHIFT_WIDTH), F32),
        jax.ShapeDtypeStruct((m, 512), F32),
        jax.ShapeDtypeStruct((m, 512), q_dtype),
        jax.ShapeDtypeStruct((m, 512), F32),
        jax.ShapeDtypeStruct((m, 512), F32),
        jax.ShapeDtypeStruct((m, 512), BF16),
        jax.ShapeDtypeStruct((m, 512), BF16),
        jax.ShapeDtypeStruct((m, 512), F32),
    )
    return pl.pallas_call(
        _inproj_kernel,
        grid=(m // tm,),
        in_specs=[row(d), pl.BlockSpec((1, d), lambda i: (0, 0)),
                  pl.BlockSpec((d, n), lambda i: (0, 0))],
        out_specs=(row(SHIFT_WIDTH), row(512), row(512), row(512), row(512), row(512), row(512),
                   row(512)),
        out_shape=out_shape,
        compiler_params=pltpu.CompilerParams(dimension_semantics=("arbitrary",),
                                             vmem_limit_bytes=VMEM_LIMIT_V7X),
        name="inproj",
    )(x, g, w_bf16)


def _outproj_kernel(yr_ref, yd_ref, w_ref, g_ref, x_ref, o_ref):
    out = (jnp.dot(yr_ref[...], w_ref[0:RWKV_WIDTH, :], preferred_element_type=F32)
           + jnp.dot(yd_ref[...], w_ref[RWKV_WIDTH:, :], preferred_element_type=F32))
    ms = jnp.mean(out * out, axis=-1, keepdims=True)
    o_ref[...] = x_ref[...] + out * lax.rsqrt(ms + RMS_EPS) * g_ref[...]


def _outproj(yr, yd, w_bf16, g, x, tm):
    m, d = x.shape
    row = lambda width: pl.BlockSpec((tm, width), lambda i: (i, 0))
    return pl.pallas_call(
        _outproj_kernel,
        grid=(m // tm,),
        in_specs=[row(512), row(512), pl.BlockSpec(w_bf16.shape, lambda i: (0, 0)),
                  pl.BlockSpec((1, d), lambda i: (0, 0)), row(d)],
        out_specs=row(d),
        out_shape=jax.ShapeDtypeStruct((m, d), F32),
        compiler_params=pltpu.CompilerParams(dimension_semantics=("arbitrary",),
                                             vmem_limit_bytes=VMEM_LIMIT_V7X),
        name="outproj",
    )(yr, yd, w_bf16, g, x)


def _rwkv_kernel(zsh_ref, gr_ref, sprev_ref, s0_ref, mu_ref, w0_ref, w1_ref, a0_ref, w2_ref,
                 kk_ref, ka_ref, rk_ref, lnw_ref, lnb_ref, e_ref,
                 y_ref, sout_ref, shift_ref, s_scr, carry_scr, *, chunk, t_valid):
    C = chunk
    R = GROUP_HEADS * C
    c_idx = pl.program_id(1)
    last = pl.num_programs(1) - 1

    @pl.when(c_idx == 0)
    def _():
        s_scr[...] = s0_ref[...]
        carry_scr[...] = sprev_ref[...]

    z = zsh_ref[...]
    row = lax.broadcasted_iota(jnp.int32, (C, 1), 0)
    prev = jnp.where(row == 0, carry_scr[...], pltpu.roll(z, 1, axis=0))
    zm = z + mu_ref[...] * (prev - z)
    last_row = z[t_valid - 1:t_valid, :]
    carry_scr[...] = last_row

    r = zm[:, 0:512]
    k = zm[:, 512:1024]
    v = zm[:, 1024:1536]
    zwa = zm[:, 1536:1664]
    lora_w = _dot(jnp.tanh(zwa), w1_ref[...])
    lora_a = _dot(zwa, w2_ref[...])
    w_raw = -_softplus(-(w0_ref[...] + lora_w)) - 0.5
    logw = -jnp.exp(w_raw)
    alr = _sigmoid(a0_ref[...] + lora_a)

    e_ones = e_ref[...]
    kk = k * kk_ref[...]
    kk = kk / jnp.maximum(jnp.sqrt(_dot_exact_rhs(kk * kk, e_ones)), 1e-12)
    k2 = k * (1.0 + (alr - 1.0) * ka_ref[...])
    a_vec = -kk
    b_vec = kk * alr
    if t_valid < C:
        valid = row < t_valid
        logw = jnp.where(valid, logw, 0.0)
        b_vec = jnp.where(valid, b_vec, 0.0)
        k2 = jnp.where(valid, k2, 0.0)

    ti = lax.broadcasted_iota(jnp.int32, (C, C), 0)
    si = lax.broadcasted_iota(jnp.int32, (C, C), 1)
    ltri = jnp.where(ti >= si, 1.0, 0.0).astype(BF16)
    cum = _dot_exact_lhs(ltri, logw)
    ew_incl = jnp.exp(cum)
    ew_excl = jnp.exp(cum - logw)
    e_inv = jnp.exp(-cum)
    a_t = a_vec * ew_excl
    r_t = r * ew_incl
    b_t = b_vec * e_inv
    k_t = k2 * e_inv
    g_c = ew_incl[C - 1:C, :]
    b_h = b_t * g_c
    k_h = k_t * g_c

    ri = lax.broadcasted_iota(jnp.int32, (R, GROUP_W), 0)
    li = lax.broadcasted_iota(jnp.int32, (R, GROUP_W), 1)
    head_mask = (ri // C) == (li // HEAD)
    qi = lax.broadcasted_iota(jnp.int32, (R, R), 0)
    qj = lax.broadcasted_iota(jnp.int32, (R, R), 1)
    same = (qi // C) == (qj // C)
    strict = same & ((qi % C) > (qj % C))
    incl = same & ((qi % C) >= (qj % C))
    eye = qi == qj
    di = lax.broadcasted_iota(jnp.int32, (GROUP_W, GROUP_W), 0)
    dj = lax.broadcasted_iota(jnp.int32, (GROUP_W, GROUP_W), 1)
    ones_g = jnp.ones((GROUP_W, GROUP_W), BF16)

    def stack(x, masked):
        s = jnp.concatenate([x] * GROUP_HEADS, axis=0)
        return jnp.where(head_mask, s, 0.0) if masked else s

    n_levels = int(math.log2(C))
    ys = []
    for g in range(N_GROUPS):
        sl = slice(g * GROUP_W, (g + 1) * GROUP_W)
        la = stack(a_t[:, sl], True).astype(BF16)
        lr = stack(r_t[:, sl], True).astype(BF16)
        bst = stack(b_t[:, sl], False).astype(BF16)
        kst = stack(k_t[:, sl], False).astype(BF16)
        bhm = stack(b_h[:, sl], True).astype(BF16)
        khm = stack(k_h[:, sl], True).astype(BF16)
        vst = stack(v[:, sl], True).astype(BF16)
        s_prev = s_scr[g]
        s_bf = s_prev.astype(BF16)

        n_ab = jnp.where(strict, _dot_nt(la, bst), 0.0)
        n_ak = jnp.where(strict, _dot_nt(la, kst), 0.0)
        m_rb = jnp.where(incl, _dot_nt(lr, bst), 0.0)
        m_rk = jnp.where(incl, _dot_nt(lr, kst), 0.0)
        rhs = _dot(la, s_bf) + _dot(n_ak, vst)
        p = n_ab
        t_inv = jnp.where(eye, 1.0, 0.0) + n_ab
        for _ in range(1, n_levels):
            p = _dot(p, p)
            t_inv = t_inv + _dot(t_inv, p)
        u = _dot(t_inv, rhs).astype(BF16)
        y_st = _dot(lr, s_bf) + _dot(m_rb, u) + _dot(m_rk, vst)
        y_g = y_st[0:C]
        for h in range(1, GROUP_HEADS):
            y_g = y_g + y_st[h * C:(h + 1) * C]
        ys.append(y_g)
        diag = jnp.where(di == dj, jnp.broadcast_to(g_c[:, sl], (GROUP_W, GROUP_W)), 0.0)
        g_col = _dot_exact_rhs(diag, ones_g)
        s_scr[g] = s_prev * g_col + _dot_tn(bhm, u) + _dot_tn(khm, vst)

    y = jnp.concatenate(ys, axis=-1)
    inv_n = 1.0 / HEAD
    mean = _dot_exact_rhs(y, e_ones) * inv_n
    d = y - mean
    var = _dot_exact_rhs(d * d, e_ones) * inv_n
    yn = d * lax.rsqrt(var + GN_EPS) * lnw_ref[...] + lnb_ref[...]
    bonus = _dot_exact_rhs(r * k2 * rk_ref[...], e_ones) * v
    gate = gr_ref[...]
    y_ref[...] = ((yn + bonus) * (gate * _sigmoid(gate))).astype(y_ref.dtype)

    @pl.when(c_idx == last)
    def _():
        sout_ref[...] = s_scr[...]
        shift_ref[...] = last_row


def _rwkv(zsh, gr, shift_prev, s0_bd, p, chunk, t_valid):
    b, t, _ = zsh.shape
    nc = t // chunk
    tok = lambda width: pl.BlockSpec((None, chunk, width), lambda i, c: (i, c, 0))
    per_b = lambda shape: pl.BlockSpec((None,) + shape, lambda i, c: (i,) + (0,) * len(shape))
    const = lambda a: pl.BlockSpec(a.shape, lambda i, c: (0,) * a.ndim)
    params = (p["mu"], p["w0"], p["w1"], p["a0"], p["w2"], p["k_k"], p["k_a"], p["r_k"],
              p["lnx_w"], p["lnx_b"], p["e_ones"])
    return pl.pallas_call(
        functools.partial(_rwkv_kernel, chunk=chunk, t_valid=t_valid),
        grid=(b, nc),
        in_specs=[tok(SHIFT_WIDTH), tok(512), per_b((1, SHIFT_WIDTH)),
                  per_b((N_GROUPS, GROUP_W, GROUP_W))] + [const(a) for a in params],
        out_specs=(tok(512), per_b((N_GROUPS, GROUP_W, GROUP_W)), per_b((1, SHIFT_WIDTH))),
        out_shape=(jax.ShapeDtypeStruct((b, t, 512), BF16),
                   jax.ShapeDtypeStruct((b, N_GROUPS, GROUP_W, GROUP_W), F32),
                   jax.ShapeDtypeStruct((b, 1, SHIFT_WIDTH), F32)),
        scratch_shapes=[pltpu.VMEM((N_GROUPS, GROUP_W, GROUP_W), F32),
                        pltpu.VMEM((1, SHIFT_WIDTH), F32)],
        compiler_params=pltpu.CompilerParams(dimension_semantics=("arbitrary", "arbitrary"),
                                             vmem_limit_bytes=VMEM_LIMIT_V7X),
        name="rwkv7",
    )(zsh, gr, shift_prev, s0_bd, *params)


def _state_to_blockdiag(s):
    b = s.shape[0]
    st = jnp.swapaxes(s, -1, -2).reshape(b, N_GROUPS, GROUP_HEADS, HEAD, HEAD)
    eye = jnp.eye(GROUP_HEADS, dtype=s.dtype)
    bd = st[:, :, :, :, None, :] * eye[None, None, :, None, :, None]
    return bd.reshape(b, N_GROUPS, GROUP_W, GROUP_W)


def _blockdiag_to_state(bd):
    b = bd.shape[0]
    x = bd.reshape(b, N_GROUPS, GROUP_HEADS, HEAD, GROUP_HEADS, HEAD)
    idx = jnp.arange(GROUP_HEADS)
    blocks = x[:, :, idx, :, idx, :]
    blocks = jnp.moveaxis(blocks, 0, 2).reshape(b, RWKV_HEADS, HEAD, HEAD)
    return jnp.swapaxes(blocks, -1, -2)


def _lambda(lq1_ref, lk1_ref, lq2_ref, lk2_ref, lam_init):
    s1 = jnp.sum(lq1_ref[...] * lk1_ref[...], axis=-1, keepdims=True)
    s2 = jnp.sum(lq2_ref[...] * lk2_ref[...], axis=-1, keepdims=True)
    return jnp.exp(s1) - jnp.exp(s2) + lam_init


def _diff_epilogue(acc0, l0, acc1, l1, lam, lam_init, subln, gate):
    o = acc0 / l0 - lam * (acc1 / l1)
    ms = jnp.mean(o * o, axis=-1, keepdims=True)
    on = o * lax.rsqrt(ms + SUBLN_EPS) * subln * (1.0 - lam_init)
    return on * (gate * _sigmoid(gate))


def _attn_kernel(rb_ref, li_ref, q_ref, k_ref, v_ref, gd_ref, lq1_ref, lk1_ref, lq2_ref, lk2_ref,
                 sub_ref, o_ref, bias0_scr, bias1_scr, m_scr, l_scr, acc_scr, *, tq):
    h = pl.program_id(1)
    qi = pl.program_id(2)
    tk = tq

    @pl.when(qi == 0)
    def _():
        rr = lax.broadcasted_iota(jnp.int32, (tq, tk), 0)
        cc = lax.broadcasted_iota(jnp.int32, (tq, tk), 1)
        d0 = rr - cc
        bias0_scr[...] = jnp.where(d0 >= 0, _rel_bias_minus_far(d0, rb_ref, h), NEG_INF)
        bias1_scr[...] = _rel_bias_minus_far(d0 + tk, rb_ref, h)

    q = q_ref[...]
    lane = lax.broadcasted_iota(jnp.int32, (tq, 2 * HEAD), 1)
    q_maps = (jnp.where(lane < HEAD, q, jnp.zeros_like(q)),
              jnp.where(lane >= HEAD, q, jnp.zeros_like(q)))

    m_scr[...] = jnp.full(m_scr.shape, NEG_INF, F32)
    l_scr[...] = jnp.zeros(l_scr.shape, F32)
    acc_scr[...] = jnp.zeros(acc_scr.shape, F32)

    def tile(kj, bias):
        off = pl.multiple_of(kj * tk, tk)
        ks = k_ref[pl.ds(off, tk), :]
        vs = v_ref[pl.ds(off, tk), :]
        for c in range(2):
            s = _dot_nt(q_maps[c], ks)
            if bias is not None:
                s = s + bias
            m_old = m_scr[c]
            m_new = jnp.maximum(m_old, jnp.max(s, axis=-1, keepdims=True))
            alpha = jnp.exp(m_old - m_new)
            pm = jnp.exp(s - m_new)
            l_scr[c] = alpha * l_scr[c] + jnp.sum(pm, axis=-1, keepdims=True)
            acc_scr[c] = alpha * acc_scr[c] + jnp.dot(pm.astype(BF16), vs,
                                                      preferred_element_type=F32)
            m_scr[c] = m_new

    def far_body(kj, carry):
        tile(kj, None)
        return carry

    lax.fori_loop(0, qi - 1, far_body, 0)

    @pl.when(qi >= 1)
    def _():
        tile(qi - 1, bias1_scr[...])

    tile(qi, bias0_scr[...])

    lam_init = li_ref[0]
    lam = _lambda(lq1_ref, lk1_ref, lq2_ref, lk2_ref, lam_init)
    out = _diff_epilogue(acc_scr[0], l_scr[0], acc_scr[1], l_scr[1], lam, lam_init,
                         sub_ref[...], gd_ref[...])
    o_ref[...] = out.astype(o_ref.dtype)


def _attn_prompt(q, kb, vb, gd, rel_bias, lam_init, lam_vecs, subln, batch, tq):
    m = q.shape[0]
    t = m // batch
    nq = t // tq
    tile = pl.BlockSpec((tq, DIFF_V), lambda b, h, i: (b * nq + i, h))
    seq = pl.BlockSpec((t, DIFF_V), lambda b, h, i: (b, h))
    vec = lambda a: pl.BlockSpec(a.shape, lambda b, h, i: (0,) * a.ndim)
    smem = pl.BlockSpec(memory_space=pltpu.SMEM)
    return pl.pallas_call(
        functools.partial(_attn_kernel, tq=tq),
        grid=(batch, DIFF_HEADS, nq),
        in_specs=[smem, smem, tile, seq, seq, tile] + [vec(a) for a in lam_vecs] + [vec(subln)],
        out_specs=tile,
        out_shape=jax.ShapeDtypeStruct((m, DIFF_WIDTH), BF16),
        scratch_shapes=[pltpu.VMEM((tq, tq), F32), pltpu.VMEM((tq, tq), F32),
                        pltpu.VMEM((2, tq, 1), F32), pltpu.VMEM((2, tq, 1), F32),
                        pltpu.VMEM((2, tq, DIFF_V), F32)],
        compiler_params=pltpu.CompilerParams(
            dimension_semantics=("arbitrary", "arbitrary", "arbitrary"),
            vmem_limit_bytes=VMEM_LIMIT_V7X),
        name="diff_attn_prompt",
    )(rel_bias, lam_init, q, kb, vb, gd, *lam_vecs, subln)


def _decode_kernel(pt_ref, rb_ref, li_ref, q_ref, kn_ref, vn_ref, gd_ref, lq1_ref, lk1_ref,
                   lq2_ref, lk2_ref, sub_ref, *rest, pages_per_step, past_len):
    P = pages_per_step
    k_refs = rest[0:P]
    v_refs = rest[P:2 * P]
    o_ref = rest[2 * P]
    m_scr, l_scr, acc_scr = rest[2 * P + 1:]
    del pt_ref
    j = pl.program_id(1)
    last = pl.num_programs(1) - 1
    TP = 8
    NR = DIFF_HEADS * 2 * TP

    @pl.when(j == 0)
    def _():
        m_scr[...] = jnp.full(m_scr.shape, NEG_INF, F32)
        l_scr[...] = jnp.zeros(l_scr.shape, F32)
        acc_scr[...] = jnp.zeros(acc_scr.shape, F32)

    q = q_ref[...]
    rowi = lax.broadcasted_iota(jnp.int32, (NR, DIFF_WIDTH), 0)
    lanei = lax.broadcasted_iota(jnp.int32, (NR, DIFF_WIDTH), 1)
    q_bd = jnp.where((rowi // TP) == (lanei // HEAD),
                     jnp.concatenate([q] * (2 * DIFF_HEADS), axis=0), 0.0).astype(BF16)

    def update(s, vals, narrow=False):
        m_old = m_scr[...]
        m_new = jnp.maximum(m_old, jnp.max(s, axis=-1, keepdims=True))
        alpha = jnp.exp(m_old - m_new)
        pm = jnp.exp(s - m_new)
        l_scr[...] = alpha * l_scr[...] + jnp.sum(pm, axis=-1, keepdims=True)
        pb = pm.astype(BF16).astype(F32) if narrow else pm.astype(BF16)
        acc_scr[...] = alpha * acc_scr[...] + jnp.dot(pb, vals, preferred_element_type=F32)
        m_scr[...] = m_new

    def page_scores(i):
        return _dot_nt(q_bd, k_refs[i][...].astype(BF16))

    def page_values(i):
        return v_refs[i][...].astype(BF16)

    def row_bias(dist_of):
        parts = []
        for h in range(DIFF_HEADS):
            t_row = lax.broadcasted_iota(jnp.int32, (2 * TP, 1), 0) % TP
            parts.append(_rel_bias_minus_far(dist_of(t_row), rb_ref, h))
        return jnp.concatenate(parts, axis=0)

    @pl.when(j != last)
    def _():
        for i in range(P):
            update(page_scores(i), page_values(i))

    @pl.when(j == last)
    def _():
        for i in range(P - 1):
            update(page_scores(i), page_values(i))
        col = lax.broadcasted_iota(jnp.int32, (1, PAGE), 1)
        bias_p = row_bias(lambda t_row: (PAGE + t_row) - col)
        update(page_scores(P - 1) + bias_p, page_values(P - 1))
        coln = lax.broadcasted_iota(jnp.int32, (1, TP), 1)
        round_bf = lambda a: a.astype(BF16).astype(F32)
        s_new = lax.dot_general(q_bd.astype(F32), round_bf(kn_ref[...]),
                                (((1,), (1,)), ((), ())), preferred_element_type=F32)
        t_all = lax.broadcasted_iota(jnp.int32, (NR, 1), 0) % TP
        bias_n = row_bias(lambda t_row: t_row - coln)
        s_new = jnp.where(t_all >= coln, s_new + bias_n, NEG_INF)
        update(s_new, round_bf(vn_ref[...]), narrow=True)

        lam_init = li_ref[0]
        lam = _lambda(lq1_ref, lk1_ref, lq2_ref, lk2_ref, lam_init)
        gate = gd_ref[...]
        outs = []
        for h in range(DIFF_HEADS):
            r0 = h * 2 * TP
            cs = slice(h * DIFF_V, (h + 1) * DIFF_V)
            outs.append(_diff_epilogue(
                acc_scr[r0:r0 + TP, cs], l_scr[r0:r0 + TP, :],
                acc_scr[r0 + TP:r0 + 2 * TP, cs], l_scr[r0 + TP:r0 + 2 * TP, :],
                lam, lam_init, sub_ref[...], gate[:, cs]))
        o_ref[...] = jnp.concatenate(outs, axis=-1).astype(o_ref.dtype)
    del past_len


def _attn_sample(q, kn, vn, gd, cache_k, cache_v, page_ids, rel_bias, lam_init, lam_vecs, subln,
                 pages_per_step):
    b = q.shape[0]
    n_pages = page_ids.shape[0] // b
    P = pages_per_step
    per_b = pl.BlockSpec((None, 8, DIFF_WIDTH), lambda i, j, pt: (i, 0, 0))
    vec = lambda a: pl.BlockSpec(a.shape, lambda i, j, pt: (0,) * a.ndim)
    smem = pl.BlockSpec(memory_space=pltpu.SMEM)

    def page_spec(slot):
        return pl.BlockSpec((None, PAGE, DIFF_WIDTH),
                            lambda i, j, pt: (pt[i * n_pages + j * P + slot], 0, 0))

    grid_spec = pltpu.PrefetchScalarGridSpec(
        num_scalar_prefetch=1,
        grid=(b, n_pages // P),
        in_specs=([smem, smem, per_b, per_b, per_b, per_b] + [vec(a) for a in lam_vecs]
                  + [vec(subln)] + [page_spec(s) for s in range(P)]
                  + [page_spec(s) for s in range(P)]),
        out_specs=per_b,
        scratch_shapes=[pltpu.VMEM((64, 1), F32), pltpu.VMEM((64, 1), F32),
                        pltpu.VMEM((64, DIFF_WIDTH), F32)],
    )
    return pl.pallas_call(
        functools.partial(_decode_kernel, pages_per_step=P, past_len=n_pages * PAGE),
        grid_spec=grid_spec,
        out_shape=jax.ShapeDtypeStruct((b, 8, DIFF_WIDTH), BF16),
        compiler_params=pltpu.CompilerParams(dimension_semantics=("arbitrary", "arbitrary"),
                                             vmem_limit_bytes=VMEM_LIMIT_V7X),
        name="diff_attn_sample",
    )(page_ids, rel_bias, lam_init, q, kn, vn, gd, *lam_vecs, subln,
      *([cache_k] * P), *([cache_v] * P))


def _layer_params(l, pre_norm, post_norm, w_in, mu_shift, w0, w_up, a0, a_up, k_k, k_a, r_k,
                  lnx_w, lnx_b, lam_q1, lam_k1, lam_q2, lam_k2, subln, w_out):
    row = lambda a: a.reshape(1, -1)
    zeros = jnp.zeros((LORA, RWKV_WIDTH), F32)
    head_of = jnp.arange(RWKV_WIDTH) // HEAD
    return dict(
        pre_norm=row(pre_norm[l]), post_norm=row(post_norm[l]),
        w_in=w_in[l].astype(BF16), w_out=w_out[l].astype(BF16),
        mu=row(mu_shift[l]), w0=row(w0[l]), a0=row(a0[l]),
        w1=jnp.concatenate([w_up[l], zeros], axis=0).astype(BF16),
        w2=jnp.concatenate([zeros, a_up[l]], axis=0).astype(BF16),
        k_k=row(k_k[l]), k_a=row(k_a[l]), r_k=row(r_k[l]),
        lnx_w=row(lnx_w[l]), lnx_b=row(lnx_b[l]),
        e_ones=(head_of[:, None] == head_of[None, :]).astype(BF16),
        lam_vecs=(row(lam_q1[l]), row(lam_k1[l]), row(lam_q2[l]), row(lam_k2[l])),
        subln=row(subln[l]),
        lam_init=jnp.full((1,), 0.8 - 0.6 * math.exp(-0.3 * l), F32),
    )


def _mixer(x2d, batch, p, rel_bias, shift_prev, s0_bd, attend, tm, chunk, t_valid, q_dtype):
    m = x2d.shape[0]
    t = m // batch
    zsh, gr, q, k, v, kb, vb, gd = _inproj(x2d, p["pre_norm"], p["w_in"], tm, q_dtype)
    y_r, s_bd, shift = _rwkv(zsh.reshape(batch, t, SHIFT_WIDTH), gr.reshape(batch, t, 512),
                             shift_prev, s0_bd, p, chunk, t_valid)
    y_d = attend(q, k, v, kb, vb, gd, p)
    x_new = _outproj(y_r.reshape(m, RWKV_WIDTH), y_d.reshape(m, DIFF_WIDTH), p["w_out"],
                     p["post_norm"], x2d, tm)
    return x_new, shift, s_bd, k, v


def kernel(x_prompt, x_sample, cache_k, cache_v, state_wkv, state_shift, page_table, rel_bias,
           pre_norm, post_norm, w_in, mu_shift, w0, w_up, a0, a_up, k_k, k_a, r_k, lnx_w, lnx_b,
           lam_q1, lam_k1, lam_q2, lam_k2, subln, w_out):
    return _forward(x_prompt, x_sample, cache_k, cache_v, state_wkv, state_shift, page_table,
                    rel_bias, pre_norm, post_norm, w_in, mu_shift, w0, w_up, a0, a_up, k_k, k_a,
                    r_k, lnx_w, lnx_b, lam_q1, lam_k1, lam_q2, lam_k2, subln, w_out,
                    PROMPT_TM=256, PROMPT_CHUNK=64, PROMPT_TQ=512, PAGES_PER_STEP=8)


def _forward(x_prompt, x_sample, cache_k, cache_v, state_wkv, state_shift, page_table, rel_bias,
             pre_norm, post_norm, w_in, mu_shift, w0, w_up, a0, a_up, k_k, k_a, r_k, lnx_w, lnx_b,
             lam_q1, lam_k1, lam_q2, lam_k2, subln, w_out, *, PROMPT_TM, PROMPT_CHUNK, PROMPT_TQ,
             PAGES_PER_STEP):
    bp, tp, d = x_prompt.shape
    bs, ts, _ = x_sample.shape
    depth = w_in.shape[0]
    n_pool = cache_k.shape[1]
    n_pages = page_table.shape[1]
    TS_PAD = 8

    xp = x_prompt.reshape(bp * tp, d)
    xs = jnp.pad(x_sample, ((0, 0), (0, TS_PAD - ts), (0, 0))).reshape(bs * TS_PAD, d)
    ck = cache_k.reshape(depth * n_pool, PAGE, DIFF_WIDTH)
    cv = cache_v.reshape(depth * n_pool, PAGE, DIFF_WIDTH)
    zero_shift = jnp.zeros((bp, 1, SHIFT_WIDTH), F32)
    zero_state = jnp.zeros((bp, N_GROUPS, GROUP_W, GROUP_W), F32)

    outs = {name: [] for name in ("kp", "vp", "wp", "sp", "ks", "vs", "ws", "ss")}
    for l in range(depth):
        p = _layer_params(l, pre_norm, post_norm, w_in, mu_shift, w0, w_up, a0, a_up, k_k, k_a,
                          r_k, lnx_w, lnx_b, lam_q1, lam_k1, lam_q2, lam_k2, subln, w_out)

        def attend_prompt(q, k, v, kb, vb, gd, p):
            return _attn_prompt(q, kb, vb, gd, rel_bias, p["lam_init"], p["lam_vecs"], p["subln"],
                                bp, PROMPT_TQ)

        page_ids = (page_table + l * n_pool).reshape(-1).astype(jnp.int32)

        def attend_sample(q, k, v, kb, vb, gd, p):
            r3 = lambda a: a.reshape(bs, TS_PAD, DIFF_WIDTH)
            return _attn_sample(r3(q), r3(k), r3(v), r3(gd), ck, cv, page_ids, rel_bias,
                                p["lam_init"], p["lam_vecs"], p["subln"], PAGES_PER_STEP)

        xp, sp, wp, kp, vp = _mixer(xp, bp, p, rel_bias, zero_shift, zero_state, attend_prompt,
                                    PROMPT_TM, PROMPT_CHUNK, PROMPT_CHUNK, BF16)
        xs, ss, ws, k_s, v_s = _mixer(xs, bs, p, rel_bias, state_shift[l][:, None, :],
                                      _state_to_blockdiag(state_wkv[l]), attend_sample,
                                      bs * TS_PAD, TS_PAD, ts, F32)
        outs["kp"].append(kp.reshape(bp, tp, DIFF_HEADS, DIFF_V))
        outs["vp"].append(vp.reshape(bp, tp, DIFF_HEADS, DIFF_V))
        outs["wp"].append(_blockdiag_to_state(wp))
        outs["sp"].append(sp.reshape(bp, SHIFT_WIDTH))
        outs["ks"].append(k_s.reshape(bs, TS_PAD, DIFF_HEADS, DIFF_V)[:, :ts])
        outs["vs"].append(v_s.reshape(bs, TS_PAD, DIFF_HEADS, DIFF_V)[:, :ts])
        outs["ws"].append(_blockdiag_to_state(ws))
        outs["ss"].append(ss.reshape(bs, SHIFT_WIDTH))

    st = lambda name: jnp.stack(outs[name])
    return (xp.reshape(bp, tp, d), xs.reshape(bs, TS_PAD, d)[:, :ts], st("kp"), st("vp"),
            st("wp"), st("sp"), st("ks"), st("vs"), st("ws"), st("ss"))
```

```python
import functools
import math

import jax
import jax.numpy as jnp
from jax import lax
from jax.experimental import pallas as pl
from jax.experimental.pallas import tpu as pltpu

F32 = jnp.float32
BF16 = jnp.bfloat16

RWKV_WIDTH = 512
DIFF_WIDTH = 512
HEAD = 64
RWKV_HEADS = RWKV_WIDTH // HEAD
LORA = 64
DIFF_HEADS = 4
DIFF_V = 128
NUM_BUCKETS = 32
MAX_EXACT = NUM_BUCKETS // 2
MAX_DISTANCE = 128
PAGE = 128
RMS_EPS = 1e-6
SUBLN_EPS = 1e-5
GN_EPS = 64e-5
NEG_INF = -1e30
LOG2E = 1.4426950408889634
SHIFT_WIDTH = 3 * RWKV_WIDTH + 2 * LORA
SEG_SHIFT = (0, SHIFT_WIDTH)
SEG_GR = (SHIFT_WIDTH, SHIFT_WIDTH + 512)
SEG_Q = (SEG_GR[1], SEG_GR[1] + 512)
SEG_K = (SEG_Q[1], SEG_Q[1] + 512)
SEG_V = (SEG_K[1], SEG_K[1] + 512)
SEG_GD = (SEG_V[1], SEG_V[1] + 512)

VMEM_LIMIT_V7X = 48 * 1024 * 1024
LANES = 128
GROUP_HEADS = 4
GROUP_W = GROUP_HEADS * HEAD
N_GROUPS = RWKV_HEADS // GROUP_HEADS


def _dot(a, b):
    return jnp.dot(a.astype(BF16), b.astype(BF16), preferred_element_type=F32)


def _dot_nt(a, b):
    return lax.dot_general(a.astype(BF16), b.astype(BF16), (((1,), (1,)), ((), ())),
                           preferred_element_type=F32)


def _dot_tn(a, b):
    return lax.dot_general(a.astype(BF16), b.astype(BF16), (((0,), (0,)), ((), ())),
                           preferred_element_type=F32)


def _split(x):
    hi = x.astype(BF16)
    lo = (x - hi.astype(F32)).astype(BF16)
    return hi, lo


def _dot_exact_rhs(a, b_exact):
    hi, lo = _split(a)
    return (jnp.dot(hi, b_exact, preferred_element_type=F32)
            + jnp.dot(lo, b_exact, preferred_element_type=F32))


def _dot_exact_lhs(a_exact, b):
    hi, lo = _split(b)
    return (jnp.dot(a_exact, hi, preferred_element_type=F32)
            + jnp.dot(a_exact, lo, preferred_element_type=F32))


def _sigmoid(x):
    return 1.0 / (1.0 + jnp.exp(-x))


def _softplus(x):
    return jnp.maximum(x, 0.0) + jnp.log(1.0 + jnp.exp(-jnp.abs(x)))


def _rel_bias_minus_far(dist, rb_ref, head):
    n = jnp.maximum(dist, 0)
    nf = jnp.maximum(n, 1).astype(F32)
    large = MAX_EXACT + (jnp.log(nf / MAX_EXACT) / math.log(MAX_DISTANCE / MAX_EXACT)
                         * (NUM_BUCKETS - MAX_EXACT)).astype(jnp.int32)
    large = jnp.minimum(large, NUM_BUCKETS - 1)
    bucket = jnp.where(n < MAX_EXACT, n, large)
    far = rb_ref[NUM_BUCKETS - 1, head]
    out = jnp.zeros(dist.shape, F32)
    for b in range(NUM_BUCKETS - 1):
        out = jnp.where(bucket == b, rb_ref[b, head] - far, out)
    return out


def _inproj_kernel(x_ref, g_ref, w_ref, zsh_ref, gr_ref, q_ref, k_ref, v_ref, gd_ref, *extra):
    x = x_ref[...]
    ms = jnp.mean(x * x, axis=-1, keepdims=True)
    h = (x * lax.rsqrt(ms + RMS_EPS) * g_ref[...]).astype(BF16)

    def seg(lo_hi):
        lo, hi = lo_hi
        return jnp.dot(h, w_ref[:, lo:hi], preferred_element_type=F32)

    zsh_ref[...] = seg(SEG_SHIFT)
    gr_ref[...] = seg(SEG_GR)
    q = seg(SEG_Q) * (LOG2E * HEAD ** -0.5)
    k = seg(SEG_K)
    k_ref[...] = k
    v = seg(SEG_V)
    v_ref[...] = v
    gd_ref[...] = seg(SEG_GD)
    if extra:
        kb_ref, vt_ref = extra
        q_ref[...] = jnp.transpose(q).astype(BF16)
        kb_ref[...] = k.astype(BF16)
        vt_ref[...] = jnp.transpose(v).astype(BF16)
    else:
        q_ref[...] = q


def _inproj(x, g, w_bf16, tm, transposed_attn_operands):
    m, d = x.shape
    n = w_bf16.shape[1]
    row = lambda width: pl.BlockSpec((tm, width), lambda i: (i, 0))
    col = pl.BlockSpec((512, tm), lambda i: (0, i))
    f32 = jax.ShapeDtypeStruct((m, 512), F32)
    bf_t = jax.ShapeDtypeStruct((512, m), BF16)
    out_shape = [jax.ShapeDtypeStruct((m, SHIFT_WIDTH), F32), f32, f32, f32, f32, f32]
    out_specs = [row(SHIFT_WIDTH), row(512), row(512), row(512), row(512), row(512)]
    if transposed_attn_operands:
        out_shape[2] = bf_t
        out_specs[2] = col
        out_shape += [jax.ShapeDtypeStruct((m, 512), BF16), bf_t]
        out_specs += [row(512), col]
    return pl.pallas_call(
        _inproj_kernel,
        grid=(m // tm,),
        in_specs=[row(d), pl.BlockSpec((1, d), lambda i: (0, 0)),
                  pl.BlockSpec((d, n), lambda i: (0, 0))],
        out_specs=tuple(out_specs),
        out_shape=tuple(out_shape),
        compiler_params=pltpu.CompilerParams(dimension_semantics=("arbitrary",),
                                             vmem_limit_bytes=VMEM_LIMIT_V7X),
        name="inproj",
    )(x, g, w_bf16)


def _outproj_kernel(yr_ref, yd_ref, w_ref, g_ref, x_ref, o_ref):
    out = (jnp.dot(yr_ref[...], w_ref[0:RWKV_WIDTH, :], preferred_element_type=F32)
           + jnp.dot(yd_ref[...], w_ref[RWKV_WIDTH:, :], preferred_element_type=F32))
    ms = jnp.mean(out * out, axis=-1, keepdims=True)
    o_ref[...] = x_ref[...] + out * lax.rsqrt(ms + RMS_EPS) * g_ref[...]


def _outproj(yr, yd, w_bf16, g, x, tm):
    m, d = x.shape
    row = lambda width: pl.BlockSpec((tm, width), lambda i: (i, 0))
    return pl.pallas_call(
        _outproj_kernel,
        grid=(m // tm,),
        in_specs=[row(512), row(512), pl.BlockSpec(w_bf16.shape, lambda i: (0, 0)),
                  pl.BlockSpec((1, d), lambda i: (0, 0)), row(d)],
        out_specs=row(d),
        out_shape=jax.ShapeDtypeStruct((m, d), F32),
        compiler_params=pltpu.CompilerParams(dimension_semantics=("arbitrary",),
                                             vmem_limit_bytes=VMEM_LIMIT_V7X),
        name="outproj",
    )(yr, yd, w_bf16, g, x)


def _rwkv_kernel(zsh_ref, gr_ref, sprev_ref, s0_ref, mu_ref, w0_ref, w1_ref, a0_ref, w2_ref,
                 kk_ref, ka_ref, rk_ref, lnw_ref, lnb_ref, e_ref,
                 y_ref, sout_ref, shift_ref, s_scr, carry_scr, *, chunk, t_valid):
    C = chunk
    NB = zsh_ref.shape[0]
    R = GROUP_HEADS * C
    c_idx = pl.program_id(1)
    last = pl.num_programs(1) - 1

    @pl.when(c_idx == 0)
    def _():
        s_scr[...] = s0_ref[...]
        carry_scr[...] = sprev_ref[...]

    z = zsh_ref[...].reshape(NB * C, SHIFT_WIDTH)
    row = lax.broadcasted_iota(jnp.int32, (NB * C, 1), 0)
    tok = row % C
    prev = pltpu.roll(z, 1, axis=0)
    last_rows = []
    for b in range(NB):
        prev = jnp.where(row == b * C, carry_scr[b], prev)
        last_rows.append(z[b * C + t_valid - 1:b * C + t_valid, :])
        carry_scr[b] = last_rows[b]
    zm = z + mu_ref[...] * (prev - z)

    r = zm[:, 0:512]
    k = zm[:, 512:1024]
    v = zm[:, 1024:1536]
    zwa = zm[:, 1536:1664]
    lora_w = _dot(jnp.tanh(zwa), w1_ref[...])
    lora_a = _dot(zwa, w2_ref[...])
    w_raw = -_softplus(-(w0_ref[...] + lora_w)) - 0.5
    logw = -jnp.exp(w_raw)
    alr = _sigmoid(a0_ref[...] + lora_a)

    e_ones = e_ref[...]
    kk = k * kk_ref[...]
    kk = kk / jnp.maximum(jnp.sqrt(_dot_exact_rhs(kk * kk, e_ones)), 1e-12)
    k2 = k * (1.0 + (alr - 1.0) * ka_ref[...])
    a_vec = -kk
    b_vec = kk * alr
    if t_valid < C:
        valid = tok < t_valid
        logw = jnp.where(valid, logw, 0.0)
        b_vec = jnp.where(valid, b_vec, 0.0)
        k2 = jnp.where(valid, k2, 0.0)

    ti = lax.broadcasted_iota(jnp.int32, (NB * C, NB * C), 0)
    si = lax.broadcasted_iota(jnp.int32, (NB * C, NB * C), 1)
    ltri = jnp.where(((ti // C) == (si // C)) & (ti >= si), 1.0, 0.0).astype(BF16)
    cum = _dot_exact_lhs(ltri, logw)
    ew_incl = jnp.exp(cum)
    ew_excl = jnp.exp(cum - logw)
    e_inv = jnp.exp(-cum)
    a_t = a_vec * ew_excl
    r_t = r * ew_incl
    b_t = b_vec * e_inv
    k_t = k2 * e_inv

    ri = lax.broadcasted_iota(jnp.int32, (R, GROUP_W), 0)
    li = lax.broadcasted_iota(jnp.int32, (R, GROUP_W), 1)
    head_mask = (ri // C) == (li // HEAD)
    qi = lax.broadcasted_iota(jnp.int32, (R, R), 0)
    qj = lax.broadcasted_iota(jnp.int32, (R, R), 1)
    same = (qi // C) == (qj // C)
    strict = same & ((qi % C) > (qj % C))
    incl = same & ((qi % C) >= (qj % C))
    eye = jnp.where(qi == qj, 1.0, 0.0)

    def stack(x, masked):
        s = jnp.concatenate([x] * GROUP_HEADS, axis=0)
        return (jnp.where(head_mask, s, 0.0) if masked else s).astype(BF16)

    def dot_cat(lhs, rhs):
        if all(a.shape[1] % LANES == 0 for a in lhs):
            return _dot(jnp.concatenate(lhs, axis=1), jnp.concatenate(rhs, axis=0))
        return sum(_dot(a, b) for a, b in zip(lhs, rhs))

    units = [(b, g) for b in range(NB) for g in range(N_GROUPS)]
    n_levels = int(math.log2(C))

    ops = []
    for b, g in units:
        rs = slice(b * C, (b + 1) * C)
        ls = slice(g * GROUP_W, (g + 1) * GROUP_W)
        g_c = ew_incl[b * C + C - 1:b * C + C, ls]
        ops.append(dict(
            la=stack(a_t[rs, ls], True), lr=stack(r_t[rs, ls], True),
            bst=stack(b_t[rs, ls], False), kst=stack(k_t[rs, ls], False),
            bhm=stack(b_t[rs, ls] * g_c, True), khm=stack(k_t[rs, ls] * g_c, True),
            vst=stack(v[rs, ls], True), g_c=g_c, s_prev=s_scr[b, g]))
    for o in ops:
        o["s_bf"] = o["s_prev"].astype(BF16)
        o["n_ab"] = jnp.where(strict, _dot_nt(o["la"], o["bst"]), 0.0)
        o["n_ak"] = jnp.where(strict, _dot_nt(o["la"], o["kst"]), 0.0)
    for o in ops:
        o["m_rb"] = jnp.where(incl, _dot_nt(o["lr"], o["bst"]), 0.0)
        o["m_rk"] = jnp.where(incl, _dot_nt(o["lr"], o["kst"]), 0.0)
        o["rhs"] = dot_cat([o["la"], o["n_ak"]], [o["s_bf"], o["vst"]])
        o["p"] = o["n_ab"]
        o["t_inv"] = eye + o["n_ab"]
    for _ in range(1, n_levels):
        for o in ops:
            o["p"] = _dot(o["p"], o["p"])
        for o in ops:
            o["t_inv"] = o["t_inv"] + _dot(o["t_inv"], o["p"])
    for o in ops:
        o["u"] = _dot(o["t_inv"], o["rhs"]).astype(BF16)
    y_units = []
    for o in ops:
        y_st = dot_cat([o["lr"], o["m_rb"], o["m_rk"]], [o["s_bf"], o["u"], o["vst"]])
        y_g = y_st[0:C]
        for h in range(1, GROUP_HEADS):
            y_g = y_g + y_st[h * C:(h + 1) * C]
        y_units.append(y_g)
    for (b, g), o in zip(units, ops):
        g_col = jnp.transpose(jnp.broadcast_to(o["g_c"], (GROUP_W, GROUP_W)))
        s_scr[b, g] = o["s_prev"] * g_col + _dot_tn(
            jnp.concatenate([o["bhm"], o["khm"]], axis=0),
            jnp.concatenate([o["u"], o["vst"]], axis=0))

    y = jnp.concatenate(
        [jnp.concatenate(y_units[b * N_GROUPS:(b + 1) * N_GROUPS], axis=-1) for b in range(NB)],
        axis=0)
    inv_n = 1.0 / HEAD
    mean = _dot_exact_rhs(y, e_ones) * inv_n
    d = y - mean
    var = _dot_exact_rhs(d * d, e_ones) * inv_n
    yn = d * lax.rsqrt(var + GN_EPS) * lnw_ref[...] + lnb_ref[...]
    bonus = _dot_exact_rhs(r * k2 * rk_ref[...], e_ones) * v
    gate = gr_ref[...].reshape(NB * C, RWKV_WIDTH)
    out = (yn + bonus) * (gate * _sigmoid(gate))
    y_ref[...] = out.reshape(NB, C, RWKV_WIDTH).astype(y_ref.dtype)

    @pl.when(c_idx == last)
    def _():
        sout_ref[...] = s_scr[...]
        for b in range(NB):
            shift_ref[b] = last_rows[b]


def _rwkv(zsh, gr, shift_prev, s0_bd, p, chunk, t_valid, seqs_per_step):
    b, t, _ = zsh.shape
    nc = t // chunk
    nb = seqs_per_step
    tok = lambda width: pl.BlockSpec((nb, chunk, width), lambda i, c: (i, c, 0))
    per_b = lambda shape: pl.BlockSpec((nb,) + shape, lambda i, c: (i,) + (0,) * len(shape))
    const = lambda a: pl.BlockSpec(a.shape, lambda i, c: (0,) * a.ndim)
    params = (p["mu"], p["w0"], p["w1"], p["a0"], p["w2"], p["k_k"], p["k_a"], p["r_k"],
              p["lnx_w"], p["lnx_b"], p["e_ones"])
    return pl.pallas_call(
        functools.partial(_rwkv_kernel, chunk=chunk, t_valid=t_valid),
        grid=(b // nb, nc),
        in_specs=[tok(SHIFT_WIDTH), tok(512), per_b((1, SHIFT_WIDTH)),
                  per_b((N_GROUPS, GROUP_W, GROUP_W))] + [const(a) for a in params],
        out_specs=(tok(512), per_b((N_GROUPS, GROUP_W, GROUP_W)), per_b((1, SHIFT_WIDTH))),
        out_shape=(jax.ShapeDtypeStruct((b, t, 512), BF16),
                   jax.ShapeDtypeStruct((b, N_GROUPS, GROUP_W, GROUP_W), F32),
                   jax.ShapeDtypeStruct((b, 1, SHIFT_WIDTH), F32)),
        scratch_shapes=[pltpu.VMEM((nb, N_GROUPS, GROUP_W, GROUP_W), F32),
                        pltpu.VMEM((nb, 1, SHIFT_WIDTH), F32)],
        compiler_params=pltpu.CompilerParams(dimension_semantics=("arbitrary", "arbitrary"),
                                             vmem_limit_bytes=VMEM_LIMIT_V7X),
        name="rwkv7",
    )(zsh, gr, shift_prev, s0_bd, *params)


def _state_to_blockdiag(s):
    b = s.shape[0]
    st = jnp.swapaxes(s, -1, -2).reshape(b, N_GROUPS, GROUP_HEADS, HEAD, HEAD)
    eye = jnp.eye(GROUP_HEADS, dtype=s.dtype)
    bd = st[:, :, :, :, None, :] * eye[None, None, :, None, :, None]
    return bd.reshape(b, N_GROUPS, GROUP_W, GROUP_W)


def _blockdiag_to_state(bd):
    b = bd.shape[0]
    x = bd.reshape(b, N_GROUPS, GROUP_HEADS, HEAD, GROUP_HEADS, HEAD)
    idx = jnp.arange(GROUP_HEADS)
    blocks = x[:, :, idx, :, idx, :]
    blocks = jnp.moveaxis(blocks, 0, 2).reshape(b, RWKV_HEADS, HEAD, HEAD)
    return jnp.swapaxes(blocks, -1, -2)


def _lambda(lq1_ref, lk1_ref, lq2_ref, lk2_ref, lam_init):
    s1 = jnp.sum(lq1_ref[...] * lk1_ref[...], axis=-1, keepdims=True)
    s2 = jnp.sum(lq2_ref[...] * lk2_ref[...], axis=-1, keepdims=True)
    return jnp.exp(s1) - jnp.exp(s2) + lam_init


def _diff_epilogue(acc0, l0, acc1, l1, lam, lam_init, subln, gate):
    o = acc0 / l0 - lam * (acc1 / l1)
    ms = jnp.mean(o * o, axis=-1, keepdims=True)
    on = o * lax.rsqrt(ms + SUBLN_EPS) * subln * (1.0 - lam_init)
    return on * (gate * _sigmoid(gate))


def _attn_kernel(rb_ref, li_ref, qt_ref, k_ref, vt_ref, gd_ref, lq1_ref, lk1_ref, lq2_ref,
                 lk2_ref, sub_ref, o_ref, bias_scr, acc_scr, *, tq, tk):
    h = pl.program_id(1)
    qi = pl.program_id(2)
    nsub = tq // tk
    first_near = nsub * qi - 1

    @pl.when(qi == 0)
    def _():
        key = lax.broadcasted_iota(jnp.int32, (tk, tq), 0)
        qry = lax.broadcasted_iota(jnp.int32, (tk, tq), 1)
        for i in range(nsub + 1):
            dist = qry - key - (i - 1) * tk
            bias = _rel_bias_minus_far(dist, rb_ref, h) * LOG2E
            bias = jnp.where(dist >= 0, bias, NEG_INF)
            bias_scr[i] = jnp.concatenate([bias, bias], axis=1)

    qt = qt_ref[...]
    feat = lax.broadcasted_iota(jnp.int32, (2 * HEAD, tq), 0)
    qt2 = jnp.concatenate([jnp.where(feat < HEAD, qt, jnp.zeros_like(qt)),
                           jnp.where(feat >= HEAD, qt, jnp.zeros_like(qt))], axis=1)

    acc_scr[...] = jnp.zeros(acc_scr.shape, F32)

    def tile(kj, bias, stats):
        m_old, l_old = stats
        off = pl.multiple_of(kj * tk, tk)
        ks = k_ref[pl.ds(off, tk), :]
        vt = vt_ref[:, pl.ds(off, tk)]
        s = jnp.dot(ks, qt2, preferred_element_type=F32)
        if bias is not None:
            s = s + bias
        m_new = jnp.maximum(m_old, jnp.max(s, axis=0, keepdims=True))
        alpha = jnp.exp2(m_old - m_new)
        pm = jnp.exp2(s - m_new)
        l_new = alpha * l_old + jnp.sum(pm, axis=0, keepdims=True)
        acc_scr[...] = alpha * acc_scr[...] + jnp.dot(vt, pm.astype(BF16),
                                                      preferred_element_type=F32)
        return m_new, l_new

    stats = (jnp.full((1, 2 * tq), NEG_INF, F32), jnp.zeros((1, 2 * tq), F32))
    stats = lax.fori_loop(
        0, first_near // 2,
        lambda i, st: tile(2 * i + 1, None, tile(2 * i, None, st)), stats)
    n_odd = jnp.where(first_near > 0, first_near % 2, 0)
    stats = lax.fori_loop(0, n_odd, lambda i, st: tile(first_near - 1, None, st), stats)
    stats = lax.fori_loop(0, jnp.minimum(qi, 1),
                          lambda i, st: tile(first_near, bias_scr[0], st), stats)
    for i in range(nsub):
        stats = tile(first_near + 1 + i, bias_scr[1 + i], stats)
    l_fin = stats[1]

    lam_init = li_ref[0]
    lam = _lambda(lq1_ref, lk1_ref, lq2_ref, lk2_ref, lam_init)
    o_t = (acc_scr[:, 0:tq] / l_fin[:, 0:tq]
           - lam * (acc_scr[:, tq:2 * tq] / l_fin[:, tq:2 * tq]))
    ms = jnp.mean(o_t * o_t, axis=0, keepdims=True)
    o = jnp.transpose(o_t * lax.rsqrt(ms + SUBLN_EPS))
    gate = gd_ref[...]
    out = o * sub_ref[...] * (1.0 - lam_init) * (gate * _sigmoid(gate))
    o_ref[...] = out.astype(o_ref.dtype)


def _attn_prompt(qt, kb, vt, gd, rel_bias, lam_init, lam_vecs, subln, batch, tq, tk):
    m = kb.shape[0]
    t = m // batch
    nq = t // tq
    assert tk >= MAX_DISTANCE and tq % tk == 0
    tile = pl.BlockSpec((tq, DIFF_V), lambda b, h, i: (b * nq + i, h))
    tile_t = pl.BlockSpec((DIFF_V, tq), lambda b, h, i: (h, b * nq + i))
    seq = pl.BlockSpec((t, DIFF_V), lambda b, h, i: (b, h))
    seq_t = pl.BlockSpec((DIFF_V, t), lambda b, h, i: (h, b))
    vec = lambda a: pl.BlockSpec(a.shape, lambda b, h, i: (0,) * a.ndim)
    smem = pl.BlockSpec(memory_space=pltpu.SMEM)
    return pl.pallas_call(
        functools.partial(_attn_kernel, tq=tq, tk=tk),
        grid=(batch, DIFF_HEADS, nq),
        in_specs=([smem, smem, tile_t, seq, seq_t, tile] + [vec(a) for a in lam_vecs]
                  + [vec(subln)]),
        out_specs=tile,
        out_shape=jax.ShapeDtypeStruct((m, DIFF_WIDTH), BF16),
        scratch_shapes=[pltpu.VMEM((tq // tk + 1, tk, 2 * tq), F32),
                        pltpu.VMEM((DIFF_V, 2 * tq), F32)],
        compiler_params=pltpu.CompilerParams(
            dimension_semantics=("arbitrary", "arbitrary", "arbitrary"),
            vmem_limit_bytes=VMEM_LIMIT_V7X),
        name="diff_attn_prompt",
    )(rel_bias, lam_init, qt, kb, vt, gd, *lam_vecs, subln)


TOKENS_PADDED = 8
SCORE_ROWS = DIFF_HEADS * 2 * TOKENS_PADDED
PAGE_ROWS = PAGE * DIFF_HEADS


def _decode_kernel(pt_ref, rb_ref, li_ref, q_ref, kn_ref, vn_ref, gd_ref, lq1_ref, lk1_ref,
                   lq2_ref, lk2_ref, sub_ref, *rest, pages_per_step):
    P = pages_per_step
    k_refs = rest[0:P]
    v_refs = rest[P:2 * P]
    o_ref = rest[2 * P]
    m_scr, acc_scr = rest[2 * P + 1:]
    del pt_ref
    j = pl.program_id(1)
    last = pl.num_programs(1) - 1
    TP = TOKENS_PADDED
    NR = SCORE_ROWS

    @pl.when(j == 0)
    def _():
        m_scr[...] = jnp.full(m_scr.shape, NEG_INF, F32)
        acc_scr[...] = jnp.zeros(acc_scr.shape, F32)

    q = q_ref[...]
    lane = lax.broadcasted_iota(jnp.int32, (TP, DIFF_V), 1)
    q_rows = jnp.concatenate(
        [jnp.where((lane // HEAD) == c, q[:, h * DIFF_V:(h + 1) * DIFF_V], 0.0)
         for h in range(DIFF_HEADS) for c in range(2)], axis=0).astype(BF16)
    row = lax.broadcasted_iota(jnp.int32, (NR, 1), 0)
    row_head = row // (2 * TP)
    row_tok = row % TP

    def scores(k_rows, bias=None, causal=False):
        n = k_rows.shape[0]
        s = _dot_nt(q_rows, k_rows)
        col = lax.broadcasted_iota(jnp.int32, (1, n), 1)
        keep = (col % DIFF_HEADS) == row_head
        if causal:
            keep = keep & (row_tok >= col // DIFF_HEADS)
        if bias is not None:
            s = s + bias
        return jnp.where(keep, s, NEG_INF)

    def row_bias(n, dist_of):
        col_tok = lax.broadcasted_iota(jnp.int32, (1, n), 1) // DIFF_HEADS
        t_row = lax.broadcasted_iota(jnp.int32, (2 * TP, 1), 0) % TP
        dist = dist_of(t_row, col_tok)
        return jnp.concatenate([_rel_bias_minus_far(dist, rb_ref, h) for h in range(DIFF_HEADS)],
                               axis=0) * LOG2E

    def update(s_list, v_list):
        m_old = m_scr[...]
        m_new = m_old
        for s in s_list:
            m_new = jnp.maximum(m_new, jnp.max(s, axis=-1, keepdims=True))
        acc = jnp.exp2(m_old - m_new) * acc_scr[...]
        for s, v in zip(s_list, v_list):
            v_ext = jnp.concatenate([v, jnp.ones(v.shape, BF16)], axis=-1)
            acc = acc + jnp.dot(jnp.exp2(s - m_new).astype(BF16), v_ext,
                                preferred_element_type=F32)
        acc_scr[...] = acc
        m_scr[...] = m_new

    page_k = lambda i: k_refs[i][...].astype(BF16)
    page_v = lambda i: v_refs[i][...].astype(BF16)

    @pl.when(j != last)
    def _():
        update([scores(page_k(i)) for i in range(P)], [page_v(i) for i in range(P)])

    @pl.when(j == last)
    def _():
        bias_p = row_bias(PAGE_ROWS, lambda t_row, col_tok: (PAGE + t_row) - col_tok)
        n_new = TP * DIFF_HEADS
        bias_n = row_bias(n_new, lambda t_row, col_tok: t_row - col_tok)
        s_list = [scores(page_k(i)) for i in range(P - 1)]
        s_list.append(scores(page_k(P - 1), bias=bias_p))
        s_list.append(scores(kn_ref[...].astype(BF16), bias=bias_n, causal=True))
        update(s_list, [page_v(i) for i in range(P)] + [vn_ref[...].astype(BF16)])

        lam_init = li_ref[0]
        lam = _lambda(lq1_ref, lk1_ref, lq2_ref, lk2_ref, lam_init)
        gate = gd_ref[...]
        outs = []
        for h in range(DIFF_HEADS):
            r0 = h * 2 * TP
            r1 = r0 + TP
            outs.append(_diff_epilogue(
                acc_scr[r0:r0 + TP, 0:DIFF_V], acc_scr[r0:r0 + TP, DIFF_V:DIFF_V + 1],
                acc_scr[r1:r1 + TP, 0:DIFF_V], acc_scr[r1:r1 + TP, DIFF_V:DIFF_V + 1],
                lam, lam_init, sub_ref[...], gate[:, h * DIFF_V:(h + 1) * DIFF_V]))
        o_ref[...] = jnp.concatenate(outs, axis=-1).astype(o_ref.dtype)


def _attn_sample(q, kn, vn, gd, cache_k, cache_v, page_ids, rel_bias, lam_init, lam_vecs, subln,
                 pages_per_step):
    b = q.shape[0]
    n_pages = page_ids.shape[0] // b
    P = pages_per_step
    per_b = lambda shape: pl.BlockSpec((None,) + shape, lambda i, j, pt: (i, 0, 0))
    vec = lambda a: pl.BlockSpec(a.shape, lambda i, j, pt: (0,) * a.ndim)
    smem = pl.BlockSpec(memory_space=pltpu.SMEM)
    tok = per_b((TOKENS_PADDED, DIFF_WIDTH))
    new = per_b((TOKENS_PADDED * DIFF_HEADS, DIFF_V))

    def page_spec(slot):
        return pl.BlockSpec((None, PAGE_ROWS, DIFF_V),
                            lambda i, j, pt: (pt[i * n_pages + j * P + slot], 0, 0))

    grid_spec = pltpu.PrefetchScalarGridSpec(
        num_scalar_prefetch=1,
        grid=(b, n_pages // P),
        in_specs=([smem, smem, tok, new, new, tok] + [vec(a) for a in lam_vecs]
                  + [vec(subln)] + [page_spec(s) for s in range(P)]
                  + [page_spec(s) for s in range(P)]),
        out_specs=tok,
        scratch_shapes=[pltpu.VMEM((SCORE_ROWS, 1), F32),
                        pltpu.VMEM((SCORE_ROWS, 2 * DIFF_V), F32)],
    )
    return pl.pallas_call(
        functools.partial(_decode_kernel, pages_per_step=P),
        grid_spec=grid_spec,
        out_shape=jax.ShapeDtypeStruct((b, TOKENS_PADDED, DIFF_WIDTH), BF16),
        compiler_params=pltpu.CompilerParams(dimension_semantics=("arbitrary", "arbitrary"),
                                             vmem_limit_bytes=VMEM_LIMIT_V7X),
        name="diff_attn_sample",
    )(page_ids, rel_bias, lam_init, q, kn, vn, gd, *lam_vecs, subln,
      *([cache_k] * P), *([cache_v] * P))


def _layer_params(l, pre_norm, post_norm, w_in, mu_shift, w0, w_up, a0, a_up, k_k, k_a, r_k,
                  lnx_w, lnx_b, lam_q1, lam_k1, lam_q2, lam_k2, subln, w_out):
    row = lambda a: a.reshape(1, -1)
    zeros = jnp.zeros((LORA, RWKV_WIDTH), F32)
    head_of = jnp.arange(RWKV_WIDTH) // HEAD
    return dict(
        pre_norm=row(pre_norm[l]), post_norm=row(post_norm[l]),
        w_in=w_in[l].astype(BF16), w_out=w_out[l].astype(BF16),
        mu=row(mu_shift[l]), w0=row(w0[l]), a0=row(a0[l]),
        w1=jnp.concatenate([w_up[l], zeros], axis=0).astype(BF16),
        w2=jnp.concatenate([zeros, a_up[l]], axis=0).astype(BF16),
        k_k=row(k_k[l]), k_a=row(k_a[l]), r_k=row(r_k[l]),
        lnx_w=row(lnx_w[l]), lnx_b=row(lnx_b[l]),
        e_ones=(head_of[:, None] == head_of[None, :]).astype(BF16),
        lam_vecs=(row(lam_q1[l]), row(lam_k1[l]), row(lam_q2[l]), row(lam_k2[l])),
        subln=row(subln[l]),
        lam_init=jnp.full((1,), 0.8 - 0.6 * math.exp(-0.3 * l), F32),
    )


def _mixer(x2d, batch, p, shift_prev, s0_bd, attend, tm, chunk, t_valid, seqs_per_step,
           transposed_attn_operands):
    m = x2d.shape[0]
    t = m // batch
    zsh, gr, q, k, v, gd, *extra = _inproj(x2d, p["pre_norm"], p["w_in"], tm,
                                           transposed_attn_operands)
    y_r, s_bd, shift = _rwkv(zsh.reshape(batch, t, SHIFT_WIDTH), gr.reshape(batch, t, 512),
                             shift_prev, s0_bd, p, chunk, t_valid, seqs_per_step)
    y_d = attend(q, k, v, gd, p, *extra)
    x_new = _outproj(y_r.reshape(m, RWKV_WIDTH), y_d.reshape(m, DIFF_WIDTH), p["w_out"],
                     p["post_norm"], x2d, tm)
    return x_new, shift, s_bd, k, v


def kernel(x_prompt, x_sample, cache_k, cache_v, state_wkv, state_shift, page_table, rel_bias,
           pre_norm, post_norm, w_in, mu_shift, w0, w_up, a0, a_up, k_k, k_a, r_k, lnx_w, lnx_b,
           lam_q1, lam_k1, lam_q2, lam_k2, subln, w_out):
    return _forward(x_prompt, x_sample, cache_k, cache_v, state_wkv, state_shift, page_table,
                    rel_bias, pre_norm, post_norm, w_in, mu_shift, w0, w_up, a0, a_up, k_k, k_a,
                    r_k, lnx_w, lnx_b, lam_q1, lam_k1, lam_q2, lam_k2, subln, w_out,
                    PROMPT_TM=256, PROMPT_CHUNK=64, PROMPT_TQ=512, PROMPT_TK=256,
                    PAGES_PER_STEP=8, SAMPLE_SEQS_PER_STEP=4)


def _forward(x_prompt, x_sample, cache_k, cache_v, state_wkv, state_shift, page_table, rel_bias,
             pre_norm, post_norm, w_in, mu_shift, w0, w_up, a0, a_up, k_k, k_a, r_k, lnx_w, lnx_b,
             lam_q1, lam_k1, lam_q2, lam_k2, subln, w_out, *, PROMPT_TM, PROMPT_CHUNK, PROMPT_TQ,
             PROMPT_TK, PAGES_PER_STEP, SAMPLE_SEQS_PER_STEP):
    bp, tp, d = x_prompt.shape
    bs, ts, _ = x_sample.shape
    depth = w_in.shape[0]
    n_pool = cache_k.shape[1]
    n_pages = page_table.shape[1]
    TS_PAD = TOKENS_PADDED

    xp = x_prompt.reshape(bp * tp, d)
    xs = jnp.pad(x_sample, ((0, 0), (0, TS_PAD - ts), (0, 0))).reshape(bs * TS_PAD, d)
    ck = cache_k.reshape(depth * n_pool, PAGE_ROWS, DIFF_V)
    cv = cache_v.reshape(depth * n_pool, PAGE_ROWS, DIFF_V)
    zero_shift = jnp.zeros((bp, 1, SHIFT_WIDTH), F32)
    zero_state = jnp.zeros((bp, N_GROUPS, GROUP_W, GROUP_W), F32)

    outs = {name: [] for name in ("kp", "vp", "wp", "sp", "ks", "vs", "ws", "ss")}
    for l in range(depth):
        p = _layer_params(l, pre_norm, post_norm, w_in, mu_shift, w0, w_up, a0, a_up, k_k, k_a,
                          r_k, lnx_w, lnx_b, lam_q1, lam_k1, lam_q2, lam_k2, subln, w_out)

        def attend_prompt(qt, k, v, gd, p, kb, vt):
            return _attn_prompt(qt, kb, vt, gd, rel_bias, p["lam_init"], p["lam_vecs"],
                                p["subln"], bp, PROMPT_TQ, PROMPT_TK)

        page_ids = (page_table + l * n_pool).reshape(-1).astype(jnp.int32)

        def attend_sample(q, k, v, gd, p):
            r3 = lambda a: a.reshape(bs, TS_PAD, DIFF_WIDTH)
            rows = lambda a: a.reshape(bs, TS_PAD * DIFF_HEADS, DIFF_V)
            return _attn_sample(r3(q), rows(k), rows(v), r3(gd), ck, cv, page_ids, rel_bias,
                                p["lam_init"], p["lam_vecs"], p["subln"], PAGES_PER_STEP)

        xp, sp, wp, kp, vp = _mixer(xp, bp, p, zero_shift, zero_state, attend_prompt,
                                    PROMPT_TM, PROMPT_CHUNK, PROMPT_CHUNK, bp, True)
        xs, ss, ws, k_s, v_s = _mixer(xs, bs, p, state_shift[l][:, None, :],
                                      _state_to_blockdiag(state_wkv[l]), attend_sample,
                                      bs * TS_PAD, TS_PAD, ts, SAMPLE_SEQS_PER_STEP, False)
        outs["kp"].append(kp.reshape(bp, tp, DIFF_HEADS, DIFF_V))
        outs["vp"].append(vp.reshape(bp, tp, DIFF_HEADS, DIFF_V))
        outs["wp"].append(_blockdiag_to_state(wp))
        outs["sp"].append(sp.reshape(bp, SHIFT_WIDTH))
        outs["ks"].append(k_s.reshape(bs, TS_PAD, DIFF_HEADS, DIFF_V)[:, :ts])
        outs["vs"].append(v_s.reshape(bs, TS_PAD, DIFF_HEADS, DIFF_V)[:, :ts])
        outs["ws"].append(_blockdiag_to_state(ws))
        outs["ss"].append(ss.reshape(bs, SHIFT_WIDTH))

    st = lambda name: jnp.stack(outs[name])
    return (xp.reshape(bp, tp, d), xs.reshape(bs, TS_PAD, d)[:, :ts], st("kp"), st("vp"),
            st("wp"), st("sp"), st("ks"), st("vs"), st("ws"), st("ss"))
```

```python
import functools
import math

import jax
import jax.numpy as jnp
from jax import lax
from jax.experimental import pallas as pl
from jax.experimental.pallas import tpu as pltpu

F32 = jnp.float32
BF16 = jnp.bfloat16

RWKV_WIDTH = 512
DIFF_WIDTH = 512
HEAD = 64
RWKV_HEADS = RWKV_WIDTH // HEAD
LORA = 64
DIFF_HEADS = 4
DIFF_V = 128
NUM_BUCKETS = 32
MAX_EXACT = NUM_BUCKETS // 2
MAX_DISTANCE = 128
PAGE = 128
RMS_EPS = 1e-6
SUBLN_EPS = 1e-5
GN_EPS = 64e-5
NEG_INF = -1e30
LOG2E = 1.4426950408889634
SHIFT_WIDTH = 3 * RWKV_WIDTH + 2 * LORA
SEG_SHIFT = (0, SHIFT_WIDTH)
SEG_GR = (SHIFT_WIDTH, SHIFT_WIDTH + 512)
SEG_Q = (SEG_GR[1], SEG_GR[1] + 512)
SEG_K = (SEG_Q[1], SEG_Q[1] + 512)
SEG_V = (SEG_K[1], SEG_K[1] + 512)
SEG_GD = (SEG_V[1], SEG_V[1] + 512)

VMEM_LIMIT_V7X = 48 * 1024 * 1024
LANES = 128
ONES_ROWS = 16
GROUP_HEADS = 4
GROUP_W = GROUP_HEADS * HEAD
N_GROUPS = RWKV_HEADS // GROUP_HEADS


def _dot(a, b):
    return jnp.dot(a.astype(BF16), b.astype(BF16), preferred_element_type=F32)


def _dot_nt(a, b):
    return lax.dot_general(a.astype(BF16), b.astype(BF16), (((1,), (1,)), ((), ())),
                           preferred_element_type=F32)


def _dot_tn(a, b):
    return lax.dot_general(a.astype(BF16), b.astype(BF16), (((0,), (0,)), ((), ())),
                           preferred_element_type=F32)


def _split(x):
    hi = x.astype(BF16)
    lo = (x - hi.astype(F32)).astype(BF16)
    return hi, lo


def _dot_exact_rhs(a, b_exact):
    hi, lo = _split(a)
    return (jnp.dot(hi, b_exact, preferred_element_type=F32)
            + jnp.dot(lo, b_exact, preferred_element_type=F32))


def _dot_exact_lhs(a_exact, b):
    hi, lo = _split(b)
    return (jnp.dot(a_exact, hi, preferred_element_type=F32)
            + jnp.dot(a_exact, lo, preferred_element_type=F32))


def _sigmoid(x):
    return 1.0 / (1.0 + jnp.exp(-x))


def _softplus(x):
    return jnp.maximum(x, 0.0) + jnp.log(1.0 + jnp.exp(-jnp.abs(x)))


def _rel_bias_minus_far(dist, rb_ref, head):
    n = jnp.maximum(dist, 0)
    nf = jnp.maximum(n, 1).astype(F32)
    large = MAX_EXACT + (jnp.log(nf / MAX_EXACT) / math.log(MAX_DISTANCE / MAX_EXACT)
                         * (NUM_BUCKETS - MAX_EXACT)).astype(jnp.int32)
    large = jnp.minimum(large, NUM_BUCKETS - 1)
    bucket = jnp.where(n < MAX_EXACT, n, large)
    far = rb_ref[NUM_BUCKETS - 1, head]
    out = jnp.zeros(dist.shape, F32)
    for b in range(NUM_BUCKETS - 1):
        out = jnp.where(bucket == b, rb_ref[b, head] - far, out)
    return out


def _inproj_kernel(x_ref, g_ref, w_ref, zsh_ref, gr_ref, q_ref, k_ref, v_ref, gd_ref, *extra):
    x = x_ref[...]
    ms = jnp.mean(x * x, axis=-1, keepdims=True)
    h = (x * lax.rsqrt(ms + RMS_EPS) * g_ref[...]).astype(BF16)

    def seg(lo_hi):
        lo, hi = lo_hi
        return jnp.dot(h, w_ref[:, lo:hi], preferred_element_type=F32)

    zsh_ref[...] = seg(SEG_SHIFT)
    gr_ref[...] = seg(SEG_GR)
    q = seg(SEG_Q) * (LOG2E * HEAD ** -0.5)
    k = seg(SEG_K)
    k_ref[...] = k
    v = seg(SEG_V)
    v_ref[...] = v
    gd_ref[...] = seg(SEG_GD)
    if extra:
        kb_ref, vt_ref = extra
        q_ref[...] = jnp.transpose(q).astype(BF16)
        kb_ref[...] = k.astype(BF16)
        vt_ref[...] = jnp.transpose(v).astype(BF16)
    else:
        q_ref[...] = q


def _inproj(x, g, w_bf16, tm, transposed_attn_operands):
    m, d = x.shape
    n = w_bf16.shape[1]
    row = lambda width: pl.BlockSpec((tm, width), lambda i: (i, 0))
    col = pl.BlockSpec((512, tm), lambda i: (0, i))
    f32 = jax.ShapeDtypeStruct((m, 512), F32)
    bf_t = jax.ShapeDtypeStruct((512, m), BF16)
    out_shape = [jax.ShapeDtypeStruct((m, SHIFT_WIDTH), F32), f32, f32, f32, f32, f32]
    out_specs = [row(SHIFT_WIDTH), row(512), row(512), row(512), row(512), row(512)]
    if transposed_attn_operands:
        out_shape[2] = bf_t
        out_specs[2] = col
        out_shape += [jax.ShapeDtypeStruct((m, 512), BF16), bf_t]
        out_specs += [row(512), col]
    return pl.pallas_call(
        _inproj_kernel,
        grid=(m // tm,),
        in_specs=[row(d), pl.BlockSpec((1, d), lambda i: (0, 0)),
                  pl.BlockSpec((d, n), lambda i: (0, 0))],
        out_specs=tuple(out_specs),
        out_shape=tuple(out_shape),
        compiler_params=pltpu.CompilerParams(dimension_semantics=("arbitrary",),
                                             vmem_limit_bytes=VMEM_LIMIT_V7X),
        name="inproj",
    )(x, g, w_bf16)


def _outproj_kernel(yr_ref, yd_ref, w_ref, g_ref, x_ref, o_ref):
    out = (jnp.dot(yr_ref[...], w_ref[0:RWKV_WIDTH, :], preferred_element_type=F32)
           + jnp.dot(yd_ref[...], w_ref[RWKV_WIDTH:, :], preferred_element_type=F32))
    ms = jnp.mean(out * out, axis=-1, keepdims=True)
    o_ref[...] = x_ref[...] + out * lax.rsqrt(ms + RMS_EPS) * g_ref[...]


def _outproj(yr, yd, w_bf16, g, x, tm):
    m, d = x.shape
    row = lambda width: pl.BlockSpec((tm, width), lambda i: (i, 0))
    return pl.pallas_call(
        _outproj_kernel,
        grid=(m // tm,),
        in_specs=[row(512), row(512), pl.BlockSpec(w_bf16.shape, lambda i: (0, 0)),
                  pl.BlockSpec((1, d), lambda i: (0, 0)), row(d)],
        out_specs=row(d),
        out_shape=jax.ShapeDtypeStruct((m, d), F32),
        compiler_params=pltpu.CompilerParams(dimension_semantics=("arbitrary",),
                                             vmem_limit_bytes=VMEM_LIMIT_V7X),
        name="outproj",
    )(yr, yd, w_bf16, g, x)


def _rwkv_kernel(zsh_ref, gr_ref, sprev_ref, s0_ref, mu_ref, w0_ref, w1_ref, a0_ref, w2_ref,
                 kk_ref, ka_ref, rk_ref, lnw_ref, lnb_ref, e_ref,
                 y_ref, sout_ref, shift_ref, s_scr, carry_scr, *, chunk, t_valid):
    C = chunk
    NB = zsh_ref.shape[0]
    R = GROUP_HEADS * C
    c_idx = pl.program_id(1)
    last = pl.num_programs(1) - 1

    @pl.when(c_idx == 0)
    def _():
        s_scr[...] = s0_ref[...]
        carry_scr[...] = sprev_ref[...]

    z = zsh_ref[...].reshape(NB * C, SHIFT_WIDTH)
    row = lax.broadcasted_iota(jnp.int32, (NB * C, 1), 0)
    tok = row % C
    prev = pltpu.roll(z, 1, axis=0)
    last_rows = []
    for b in range(NB):
        prev = jnp.where(row == b * C, carry_scr[b], prev)
        last_rows.append(z[b * C + t_valid - 1:b * C + t_valid, :])
        carry_scr[b] = last_rows[b]
    zm = z + mu_ref[...] * (prev - z)

    r = zm[:, 0:512]
    k = zm[:, 512:1024]
    v = zm[:, 1024:1536]
    zwa = zm[:, 1536:1664]
    lora_w = _dot(jnp.tanh(zwa), w1_ref[...])
    lora_a = _dot(zwa, w2_ref[...])
    w_raw = -_softplus(-(w0_ref[...] + lora_w)) - 0.5
    logw = -jnp.exp(w_raw)
    alr = _sigmoid(a0_ref[...] + lora_a)

    e_ones = e_ref[...]

    def head_sum(x):
        return jnp.concatenate(
            [_dot_exact_rhs(x[:, g * GROUP_W:(g + 1) * GROUP_W], e_ones)
             for g in range(N_GROUPS)], axis=1)

    kk = k * kk_ref[...]
    kk = kk / jnp.maximum(jnp.sqrt(head_sum(kk * kk)), 1e-12)
    k2 = k * (1.0 + (alr - 1.0) * ka_ref[...])
    a_vec = -kk
    b_vec = kk * alr
    if t_valid < C:
        valid = tok < t_valid
        logw = jnp.where(valid, logw, 0.0)
        b_vec = jnp.where(valid, b_vec, 0.0)
        k2 = jnp.where(valid, k2, 0.0)

    ti = lax.broadcasted_iota(jnp.int32, (NB * C, NB * C), 0)
    si = lax.broadcasted_iota(jnp.int32, (NB * C, NB * C), 1)
    ltri = jnp.where(((ti // C) == (si // C)) & (ti >= si), 1.0, 0.0).astype(BF16)
    cum = _dot_exact_lhs(ltri, logw)
    ew_incl = jnp.exp(cum)
    ew_excl = jnp.exp(cum - logw)
    e_inv = jnp.exp(-cum)
    a_t = a_vec * ew_excl
    r_t = r * ew_incl
    b_t = b_vec * e_inv
    k_t = k2 * e_inv

    ri = lax.broadcasted_iota(jnp.int32, (R, GROUP_W), 0)
    li = lax.broadcasted_iota(jnp.int32, (R, GROUP_W), 1)
    head_mask = (ri // C) == (li // HEAD)
    qi = lax.broadcasted_iota(jnp.int32, (R, R), 0)
    qj = lax.broadcasted_iota(jnp.int32, (R, R), 1)
    same = (qi // C) == (qj // C)
    strict = same & ((qi % C) > (qj % C))
    incl = same & ((qi % C) >= (qj % C))
    eye = jnp.where(qi == qj, 1.0, 0.0)

    def stack(x, masked):
        s = jnp.concatenate([x] * GROUP_HEADS, axis=0)
        return (jnp.where(head_mask, s, 0.0) if masked else s).astype(BF16)

    def dot_cat(lhs, rhs):
        if all(a.shape[1] % LANES == 0 for a in lhs):
            return _dot(jnp.concatenate(lhs, axis=1), jnp.concatenate(rhs, axis=0))
        return sum(_dot(a, b) for a, b in zip(lhs, rhs))

    units = [(b, g) for b in range(NB) for g in range(N_GROUPS)]
    n_levels = int(math.log2(C))

    ops = []
    for b, g in units:
        rs = slice(b * C, (b + 1) * C)
        ls = slice(g * GROUP_W, (g + 1) * GROUP_W)
        g_c = ew_incl[b * C + C - 1:b * C + C, ls]
        ops.append(dict(
            la=stack(a_t[rs, ls], True), lr=stack(r_t[rs, ls], True),
            bst=stack(b_t[rs, ls], False), kst=stack(k_t[rs, ls], False),
            bhm=stack(b_t[rs, ls] * g_c, True), khm=stack(k_t[rs, ls] * g_c, True),
            vst=stack(v[rs, ls], True), g_c=g_c, s_prev=s_scr[b, g]))
    for o in ops:
        o["s_bf"] = o["s_prev"].astype(BF16)
        o["n_ab"] = jnp.where(strict, _dot_nt(o["la"], o["bst"]), 0.0)
        o["n_ak"] = jnp.where(strict, _dot_nt(o["la"], o["kst"]), 0.0)
    for o in ops:
        o["m_rb"] = jnp.where(incl, _dot_nt(o["lr"], o["bst"]), 0.0)
        o["m_rk"] = jnp.where(incl, _dot_nt(o["lr"], o["kst"]), 0.0)
        o["rhs"] = dot_cat([o["la"], o["n_ak"]], [o["s_bf"], o["vst"]])
        o["p"] = o["n_ab"]
        o["t_inv"] = eye + o["n_ab"]
    for _ in range(1, n_levels):
        for o in ops:
            o["p"] = _dot(o["p"], o["p"])
        for o in ops:
            o["t_inv"] = o["t_inv"] + _dot(o["t_inv"], o["p"])
    for o in ops:
        o["u"] = _dot(o["t_inv"], o["rhs"]).astype(BF16)
    y_units = []
    for o in ops:
        y_st = dot_cat([o["lr"], o["m_rb"], o["m_rk"]], [o["s_bf"], o["u"], o["vst"]])
        y_g = y_st[0:C]
        for h in range(1, GROUP_HEADS):
            y_g = y_g + y_st[h * C:(h + 1) * C]
        y_units.append(y_g)
    for (b, g), o in zip(units, ops):
        g_col = jnp.transpose(jnp.broadcast_to(o["g_c"], (GROUP_W, GROUP_W)))
        s_scr[b, g] = o["s_prev"] * g_col + _dot_tn(
            jnp.concatenate([o["bhm"], o["khm"]], axis=0),
            jnp.concatenate([o["u"], o["vst"]], axis=0))

    y = jnp.concatenate(
        [jnp.concatenate(y_units[b * N_GROUPS:(b + 1) * N_GROUPS], axis=-1) for b in range(NB)],
        axis=0)
    inv_n = 1.0 / HEAD
    mean = head_sum(y) * inv_n
    d = y - mean
    var = head_sum(d * d) * inv_n
    yn = d * lax.rsqrt(var + GN_EPS) * lnw_ref[...] + lnb_ref[...]
    bonus = head_sum(r * k2 * rk_ref[...]) * v
    gate = gr_ref[...].reshape(NB * C, RWKV_WIDTH)
    out = (yn + bonus) * (gate * _sigmoid(gate))
    y_ref[...] = out.reshape(NB, C, RWKV_WIDTH).astype(y_ref.dtype)

    @pl.when(c_idx == last)
    def _():
        sout_ref[...] = s_scr[...]
        for b in range(NB):
            shift_ref[b] = last_rows[b]


def _rwkv(zsh, gr, shift_prev, s0_bd, p, chunk, t_valid, seqs_per_step):
    b, t, _ = zsh.shape
    nc = t // chunk
    nb = seqs_per_step
    tok = lambda width: pl.BlockSpec((nb, chunk, width), lambda i, c: (i, c, 0))
    per_b = lambda shape: pl.BlockSpec((nb,) + shape, lambda i, c: (i,) + (0,) * len(shape))
    const = lambda a: pl.BlockSpec(a.shape, lambda i, c: (0,) * a.ndim)
    params = (p["mu"], p["w0"], p["w1"], p["a0"], p["w2"], p["k_k"], p["k_a"], p["r_k"],
              p["lnx_w"], p["lnx_b"], p["e_ones"])
    return pl.pallas_call(
        functools.partial(_rwkv_kernel, chunk=chunk, t_valid=t_valid),
        grid=(b // nb, nc),
        in_specs=[tok(SHIFT_WIDTH), tok(512), per_b((1, SHIFT_WIDTH)),
                  per_b((N_GROUPS, GROUP_W, GROUP_W))] + [const(a) for a in params],
        out_specs=(tok(512), per_b((N_GROUPS, GROUP_W, GROUP_W)), per_b((1, SHIFT_WIDTH))),
        out_shape=(jax.ShapeDtypeStruct((b, t, 512), BF16),
                   jax.ShapeDtypeStruct((b, N_GROUPS, GROUP_W, GROUP_W), F32),
                   jax.ShapeDtypeStruct((b, 1, SHIFT_WIDTH), F32)),
        scratch_shapes=[pltpu.VMEM((nb, N_GROUPS, GROUP_W, GROUP_W), F32),
                        pltpu.VMEM((nb, 1, SHIFT_WIDTH), F32)],
        compiler_params=pltpu.CompilerParams(dimension_semantics=("arbitrary", "arbitrary"),
                                             vmem_limit_bytes=VMEM_LIMIT_V7X),
        name="rwkv7",
    )(zsh, gr, shift_prev, s0_bd, *params)


def _state_to_blockdiag(s):
    b = s.shape[0]
    st = jnp.swapaxes(s, -1, -2).reshape(b, N_GROUPS, GROUP_HEADS, HEAD, HEAD)
    eye = jnp.eye(GROUP_HEADS, dtype=s.dtype)
    bd = st[:, :, :, :, None, :] * eye[None, None, :, None, :, None]
    return bd.reshape(b, N_GROUPS, GROUP_W, GROUP_W)


def _blockdiag_to_state(bd):
    b = bd.shape[0]
    x = bd.reshape(b, N_GROUPS, GROUP_HEADS, HEAD, GROUP_HEADS, HEAD)
    idx = jnp.arange(GROUP_HEADS)
    blocks = x[:, :, idx, :, idx, :]
    blocks = jnp.moveaxis(blocks, 0, 2).reshape(b, RWKV_HEADS, HEAD, HEAD)
    return jnp.swapaxes(blocks, -1, -2)


def _lambda(lq1_ref, lk1_ref, lq2_ref, lk2_ref, lam_init):
    s1 = jnp.sum(lq1_ref[...] * lk1_ref[...], axis=-1, keepdims=True)
    s2 = jnp.sum(lq2_ref[...] * lk2_ref[...], axis=-1, keepdims=True)
    return jnp.exp(s1) - jnp.exp(s2) + lam_init


def _diff_epilogue(acc0, l0, acc1, l1, lam, lam_init, subln, gate):
    o = acc0 / l0 - lam * (acc1 / l1)
    ms = jnp.mean(o * o, axis=-1, keepdims=True)
    on = o * lax.rsqrt(ms + SUBLN_EPS) * subln * (1.0 - lam_init)
    return on * (gate * _sigmoid(gate))


def _attn_kernel(rb_ref, li_ref, qt_ref, k_ref, vt_ref, gd_ref, lq1_ref, lk1_ref, lq2_ref,
                 lk2_ref, sub_ref, o_ref, bias_scr, s_scr, acc_scr, *, tq, tk):
    h = pl.program_id(1)
    qi = pl.program_id(2)
    nsub = tq // tk
    first_near = nsub * qi - 1

    @pl.when(qi == 0)
    def _():
        key = lax.broadcasted_iota(jnp.int32, (tk, tq), 0)
        qry = lax.broadcasted_iota(jnp.int32, (tk, tq), 1)
        for i in range(nsub + 1):
            dist = qry - key - (i - 1) * tk
            bias = _rel_bias_minus_far(dist, rb_ref, h) * LOG2E
            bias = jnp.where(dist >= 0, bias, NEG_INF)
            bias_scr[i] = jnp.concatenate([bias, bias], axis=1)

    qt = qt_ref[...]
    feat = lax.broadcasted_iota(jnp.int32, (2 * HEAD, tq), 0)
    qt2 = jnp.concatenate([jnp.where(feat < HEAD, qt, jnp.zeros_like(qt)),
                           jnp.where(feat >= HEAD, qt, jnp.zeros_like(qt))], axis=1)

    acc_scr[...] = jnp.zeros(acc_scr.shape, F32)
    ones_rows = jnp.ones((ONES_ROWS, tk), BF16)

    def logits(kj):
        off = pl.multiple_of(kj * tk, tk)
        return jnp.dot(k_ref[pl.ds(off, tk), :], qt2, preferred_element_type=F32)

    def absorb(s, kj, m_old):
        off = pl.multiple_of(kj * tk, tk)
        vt_ext = jnp.concatenate([vt_ref[:, pl.ds(off, tk)], ones_rows], axis=0)
        m_new = jnp.maximum(m_old, jnp.max(s, axis=0, keepdims=True))
        pm = jnp.exp2(s - m_new).astype(BF16)
        acc_scr[...] = (jnp.exp2(m_old - m_new) * acc_scr[...]
                        + jnp.dot(vt_ext, pm, preferred_element_type=F32))
        return m_new

    assert nsub == 2
    m_run = jnp.full((1, 2 * tq), NEG_INF, F32)

    @pl.when(qi >= 1)
    def _():
        s_scr[0] = logits(0)

    def far_pair(i, m):
        s_scr[1] = logits(2 * i + 1)
        m = absorb(s_scr[0], 2 * i, m)
        s_scr[0] = logits(2 * i + 2)
        return absorb(s_scr[1], 2 * i + 1, m)

    m_run = lax.fori_loop(0, qi - 1, far_pair, m_run)

    def last_pair(i, m):
        s_scr[1] = logits(first_near)
        m = absorb(s_scr[0], first_near - 1, m)
        return absorb(s_scr[1] + bias_scr[0], first_near, m)

    m_run = lax.fori_loop(0, jnp.minimum(qi, 1), last_pair, m_run)
    s_diag = [logits(first_near + 1 + i) for i in range(nsub)]
    for i in range(nsub):
        m_run = absorb(s_diag[i] + bias_scr[1 + i], first_near + 1 + i, m_run)

    lam_init = li_ref[0]
    lam = _lambda(lq1_ref, lk1_ref, lq2_ref, lk2_ref, lam_init)
    l_fin = acc_scr[DIFF_V:DIFF_V + 1, :]
    o_t = (acc_scr[0:DIFF_V, 0:tq] / l_fin[:, 0:tq]
           - lam * (acc_scr[0:DIFF_V, tq:2 * tq] / l_fin[:, tq:2 * tq]))
    ms = jnp.mean(o_t * o_t, axis=0, keepdims=True)
    o = jnp.transpose(o_t * lax.rsqrt(ms + SUBLN_EPS))
    gate = gd_ref[...]
    out = o * sub_ref[...] * (1.0 - lam_init) * (gate * _sigmoid(gate))
    o_ref[...] = out.astype(o_ref.dtype)


def _attn_prompt(qt, kb, vt, gd, rel_bias, lam_init, lam_vecs, subln, batch, tq, tk):
    m = kb.shape[0]
    t = m // batch
    nq = t // tq
    assert tk >= MAX_DISTANCE and tq % tk == 0
    tile = pl.BlockSpec((tq, DIFF_V), lambda b, h, i: (b * nq + i, h))
    tile_t = pl.BlockSpec((DIFF_V, tq), lambda b, h, i: (h, b * nq + i))
    seq = pl.BlockSpec((t, DIFF_V), lambda b, h, i: (b, h))
    seq_t = pl.BlockSpec((DIFF_V, t), lambda b, h, i: (h, b))
    vec = lambda a: pl.BlockSpec(a.shape, lambda b, h, i: (0,) * a.ndim)
    smem = pl.BlockSpec(memory_space=pltpu.SMEM)
    return pl.pallas_call(
        functools.partial(_attn_kernel, tq=tq, tk=tk),
        grid=(batch, DIFF_HEADS, nq),
        in_specs=([smem, smem, tile_t, seq, seq_t, tile] + [vec(a) for a in lam_vecs]
                  + [vec(subln)]),
        out_specs=tile,
        out_shape=jax.ShapeDtypeStruct((m, DIFF_WIDTH), BF16),
        scratch_shapes=[pltpu.VMEM((tq // tk + 1, tk, 2 * tq), F32),
                        pltpu.VMEM((2, tk, 2 * tq), F32),
                        pltpu.VMEM((DIFF_V + ONES_ROWS, 2 * tq), F32)],
        compiler_params=pltpu.CompilerParams(
            dimension_semantics=("arbitrary", "arbitrary", "arbitrary"),
            vmem_limit_bytes=VMEM_LIMIT_V7X),
        name="diff_attn_prompt",
    )(rel_bias, lam_init, qt, kb, vt, gd, *lam_vecs, subln)


TOKENS_PADDED = 8
SCORE_ROWS = DIFF_HEADS * 2 * TOKENS_PADDED
PAGE_ROWS = PAGE * DIFF_HEADS
PAGES_PER_SOFTMAX_STEP = 2


def _decode_kernel(pt_ref, rb_ref, li_ref, q_ref, kn_ref, vn_ref, gd_ref, lq1_ref, lk1_ref,
                   lq2_ref, lk2_ref, sub_ref, *rest, pages_per_step):
    P = pages_per_step
    k_refs = rest[0:P]
    v_refs = rest[P:2 * P]
    o_ref = rest[2 * P]
    m_scr, acc_scr = rest[2 * P + 1:]
    del pt_ref
    j = pl.program_id(1)
    last = pl.num_programs(1) - 1
    TP = TOKENS_PADDED
    NR = SCORE_ROWS

    @pl.when(j == 0)
    def _():
        m_scr[...] = jnp.full(m_scr.shape, NEG_INF, F32)
        acc_scr[...] = jnp.zeros(acc_scr.shape, F32)

    q = q_ref[...]
    lane = lax.broadcasted_iota(jnp.int32, (TP, DIFF_V), 1)
    q_rows = jnp.concatenate(
        [jnp.where((lane // HEAD) == c, q[:, h * DIFF_V:(h + 1) * DIFF_V], 0.0)
         for h in range(DIFF_HEADS) for c in range(2)], axis=0)
    q_rows = q_rows.astype(BF16).astype(F32)
    row = lax.broadcasted_iota(jnp.int32, (NR, 1), 0)
    row_head = row // (2 * TP)
    row_tok = row % TP

    def scores(k_rows, bias=None, causal=False):
        n = k_rows.shape[0]
        s = lax.dot_general(q_rows, k_rows, (((1,), (1,)), ((), ())),
                            preferred_element_type=F32)
        col = lax.broadcasted_iota(jnp.int32, (1, n), 1)
        keep = (col % DIFF_HEADS) == row_head
        if causal:
            keep = keep & (row_tok >= col // DIFF_HEADS)
        if bias is not None:
            s = s + bias
        return jnp.where(keep, s, NEG_INF)

    def row_bias(n, dist_of):
        col_tok = lax.broadcasted_iota(jnp.int32, (1, n), 1) // DIFF_HEADS
        t_row = lax.broadcasted_iota(jnp.int32, (2 * TP, 1), 0) % TP
        dist = dist_of(t_row, col_tok)
        return jnp.concatenate([_rel_bias_minus_far(dist, rb_ref, h) for h in range(DIFF_HEADS)],
                               axis=0) * LOG2E

    def update(s_list, v_list):
        m_old = m_scr[...]
        m_new = m_old
        for s in s_list:
            m_new = jnp.maximum(m_new, jnp.max(s, axis=-1, keepdims=True))
        acc = jnp.exp2(m_old - m_new) * acc_scr[...]
        for s, v in zip(s_list, v_list):
            v_ext = jnp.concatenate([v, jnp.ones(v.shape, F32)], axis=-1)
            acc = acc + jnp.dot(jnp.exp2(s - m_new), v_ext, preferred_element_type=F32)
        acc_scr[...] = acc
        m_scr[...] = m_new

    page_k = lambda i: k_refs[i][...]
    page_v = lambda i: v_refs[i][...]
    G = PAGES_PER_SOFTMAX_STEP

    @pl.when(j != last)
    def _():
        for i0 in range(0, P, G):
            update([scores(page_k(i)) for i in range(i0, i0 + G)],
                   [page_v(i) for i in range(i0, i0 + G)])

    @pl.when(j == last)
    def _():
        bias_p = row_bias(PAGE_ROWS, lambda t_row, col_tok: (PAGE + t_row) - col_tok)
        n_new = TP * DIFF_HEADS
        bias_n = row_bias(n_new, lambda t_row, col_tok: t_row - col_tok)
        for i0 in range(0, P - G, G):
            update([scores(page_k(i)) for i in range(i0, i0 + G)],
                   [page_v(i) for i in range(i0, i0 + G)])
        s_list = [scores(page_k(i)) for i in range(P - G, P - 1)]
        s_list.append(scores(page_k(P - 1), bias=bias_p))
        s_list.append(scores(kn_ref[...], bias=bias_n, causal=True))
        update(s_list, [page_v(i) for i in range(P - G, P)] + [vn_ref[...]])

        lam_init = li_ref[0]
        lam = _lambda(lq1_ref, lk1_ref, lq2_ref, lk2_ref, lam_init)
        gate = gd_ref[...]
        outs = []
        for h in range(DIFF_HEADS):
            r0 = h * 2 * TP
            r1 = r0 + TP
            outs.append(_diff_epilogue(
                acc_scr[r0:r0 + TP, 0:DIFF_V], acc_scr[r0:r0 + TP, DIFF_V:DIFF_V + 1],
                acc_scr[r1:r1 + TP, 0:DIFF_V], acc_scr[r1:r1 + TP, DIFF_V:DIFF_V + 1],
                lam, lam_init, sub_ref[...], gate[:, h * DIFF_V:(h + 1) * DIFF_V]))
        o_ref[...] = jnp.concatenate(outs, axis=-1).astype(o_ref.dtype)


def _attn_sample(q, kn, vn, gd, cache_k, cache_v, page_ids, rel_bias, lam_init, lam_vecs, subln,
                 pages_per_step):
    b = q.shape[0]
    n_pages = page_ids.shape[0] // b
    P = pages_per_step
    per_b = lambda shape: pl.BlockSpec((None,) + shape, lambda i, j, pt: (i, 0, 0))
    vec = lambda a: pl.BlockSpec(a.shape, lambda i, j, pt: (0,) * a.ndim)
    smem = pl.BlockSpec(memory_space=pltpu.SMEM)
    tok = per_b((TOKENS_PADDED, DIFF_WIDTH))
    new = per_b((TOKENS_PADDED * DIFF_HEADS, DIFF_V))

    def page_spec(slot):
        return pl.BlockSpec((None, PAGE_ROWS, DIFF_V),
                            lambda i, j, pt: (pt[i * n_pages + j * P + slot], 0, 0))

    grid_spec = pltpu.PrefetchScalarGridSpec(
        num_scalar_prefetch=1,
        grid=(b, n_pages // P),
        in_specs=([smem, smem, tok, new, new, tok] + [vec(a) for a in lam_vecs]
                  + [vec(subln)] + [page_spec(s) for s in range(P)]
                  + [page_spec(s) for s in range(P)]),
        out_specs=tok,
        scratch_shapes=[pltpu.VMEM((SCORE_ROWS, 1), F32),
                        pltpu.VMEM((SCORE_ROWS, 2 * DIFF_V), F32)],
    )
    return pl.pallas_call(
        functools.partial(_decode_kernel, pages_per_step=P),
        grid_spec=grid_spec,
        out_shape=jax.ShapeDtypeStruct((b, TOKENS_PADDED, DIFF_WIDTH), BF16),
        compiler_params=pltpu.CompilerParams(dimension_semantics=("arbitrary", "arbitrary"),
                                             vmem_limit_bytes=VMEM_LIMIT_V7X),
        name="diff_attn_sample",
    )(page_ids, rel_bias, lam_init, q, kn, vn, gd, *lam_vecs, subln,
      *([cache_k] * P), *([cache_v] * P))


def _layer_params(l, pre_norm, post_norm, w_in, mu_shift, w0, w_up, a0, a_up, k_k, k_a, r_k,
                  lnx_w, lnx_b, lam_q1, lam_k1, lam_q2, lam_k2, subln, w_out):
    row = lambda a: a.reshape(1, -1)
    zeros = jnp.zeros((LORA, RWKV_WIDTH), F32)
    head_of = jnp.arange(GROUP_W) // HEAD
    return dict(
        pre_norm=row(pre_norm[l]), post_norm=row(post_norm[l]),
        w_in=w_in[l].astype(BF16), w_out=w_out[l].astype(BF16),
        mu=row(mu_shift[l]), w0=row(w0[l]), a0=row(a0[l]),
        w1=jnp.concatenate([w_up[l], zeros], axis=0).astype(BF16),
        w2=jnp.concatenate([zeros, a_up[l]], axis=0).astype(BF16),
        k_k=row(k_k[l]), k_a=row(k_a[l]), r_k=row(r_k[l]),
        lnx_w=row(lnx_w[l]), lnx_b=row(lnx_b[l]),
        e_ones=(head_of[:, None] == head_of[None, :]).astype(BF16),
        lam_vecs=(row(lam_q1[l]), row(lam_k1[l]), row(lam_q2[l]), row(lam_k2[l])),
        subln=row(subln[l]),
        lam_init=jnp.full((1,), 0.8 - 0.6 * math.exp(-0.3 * l), F32),
    )


def _mixer(x2d, batch, p, shift_prev, s0_bd, attend, tm, chunk, t_valid, seqs_per_step,
           transposed_attn_operands):
    m = x2d.shape[0]
    t = m // batch
    zsh, gr, q, k, v, gd, *extra = _inproj(x2d, p["pre_norm"], p["w_in"], tm,
                                           transposed_attn_operands)
    y_r, s_bd, shift = _rwkv(zsh.reshape(batch, t, SHIFT_WIDTH), gr.reshape(batch, t, 512),
                             shift_prev, s0_bd, p, chunk, t_valid, seqs_per_step)
    y_d = attend(q, k, v, gd, p, *extra)
    x_new = _outproj(y_r.reshape(m, RWKV_WIDTH), y_d.reshape(m, DIFF_WIDTH), p["w_out"],
                     p["post_norm"], x2d, tm)
    return x_new, shift, s_bd, k, v


def kernel(x_prompt, x_sample, cache_k, cache_v, state_wkv, state_shift, page_table, rel_bias,
           pre_norm, post_norm, w_in, mu_shift, w0, w_up, a0, a_up, k_k, k_a, r_k, lnx_w, lnx_b,
           lam_q1, lam_k1, lam_q2, lam_k2, subln, w_out):
    return _forward(x_prompt, x_sample, cache_k, cache_v, state_wkv, state_shift, page_table,
                    rel_bias, pre_norm, post_norm, w_in, mu_shift, w0, w_up, a0, a_up, k_k, k_a,
                    r_k, lnx_w, lnx_b, lam_q1, lam_k1, lam_q2, lam_k2, subln, w_out,
                    PROMPT_TM=256, PROMPT_CHUNK=64, PROMPT_TQ=512, PROMPT_TK=256,
                    PAGES_PER_STEP=8, SAMPLE_SEQS_PER_STEP=4)


def _forward(x_prompt, x_sample, cache_k, cache_v, state_wkv, state_shift, page_table, rel_bias,
             pre_norm, post_norm, w_in, mu_shift, w0, w_up, a0, a_up, k_k, k_a, r_k, lnx_w, lnx_b,
             lam_q1, lam_k1, lam_q2, lam_k2, subln, w_out, *, PROMPT_TM, PROMPT_CHUNK, PROMPT_TQ,
             PROMPT_TK, PAGES_PER_STEP, SAMPLE_SEQS_PER_STEP):
    bp, tp, d = x_prompt.shape
    bs, ts, _ = x_sample.shape
    depth = w_in.shape[0]
    n_pool = cache_k.shape[1]
    n_pages = page_table.shape[1]
    TS_PAD = TOKENS_PADDED

    xp = x_prompt.reshape(bp * tp, d)
    xs = jnp.pad(x_sample, ((0, 0), (0, TS_PAD - ts), (0, 0))).reshape(bs * TS_PAD, d)
    ck = cache_k.reshape(depth * n_pool, PAGE_ROWS, DIFF_V)
    cv = cache_v.reshape(depth * n_pool, PAGE_ROWS, DIFF_V)
    zero_shift = jnp.zeros((bp, 1, SHIFT_WIDTH), F32)
    zero_state = jnp.zeros((bp, N_GROUPS, GROUP_W, GROUP_W), F32)

    outs = {name: [] for name in ("kp", "vp", "wp", "sp", "ks", "vs", "ws", "ss")}
    for l in range(depth):
        p = _layer_params(l, pre_norm, post_norm, w_in, mu_shift, w0, w_up, a0, a_up, k_k, k_a,
                          r_k, lnx_w, lnx_b, lam_q1, lam_k1, lam_q2, lam_k2, subln, w_out)

        def attend_prompt(qt, k, v, gd, p, kb, vt):
            return _attn_prompt(qt, kb, vt, gd, rel_bias, p["lam_init"], p["lam_vecs"],
                                p["subln"], bp, PROMPT_TQ, PROMPT_TK)

        page_ids = (page_table + l * n_pool).reshape(-1).astype(jnp.int32)

        def attend_sample(q, k, v, gd, p):
            r3 = lambda a: a.reshape(bs, TS_PAD, DIFF_WIDTH)
            rows = lambda a: a.reshape(bs, TS_PAD * DIFF_HEADS, DIFF_V)
            return _attn_sample(r3(q), rows(k), rows(v), r3(gd), ck, cv, page_ids, rel_bias,
                                p["lam_init"], p["lam_vecs"], p["subln"], PAGES_PER_STEP)

        xp, sp, wp, kp, vp = _mixer(xp, bp, p, zero_shift, zero_state, attend_prompt,
                                    PROMPT_TM, PROMPT_CHUNK, PROMPT_CHUNK, bp, True)
        xs, ss, ws, k_s, v_s = _mixer(xs, bs, p, state_shift[l][:, None, :],
                                      _state_to_blockdiag(state_wkv[l]), attend_sample,
                                      bs * TS_PAD, TS_PAD, ts, SAMPLE_SEQS_PER_STEP, False)
        outs["kp"].append(kp.reshape(bp, tp, DIFF_HEADS, DIFF_V))
        outs["vp"].append(vp.reshape(bp, tp, DIFF_HEADS, DIFF_V))
        outs["wp"].append(_blockdiag_to_state(wp))
        outs["sp"].append(sp.reshape(bp, SHIFT_WIDTH))
        outs["ks"].append(k_s.reshape(bs, TS_PAD, DIFF_HEADS, DIFF_V)[:, :ts])
        outs["vs"].append(v_s.reshape(bs, TS_PAD, DIFF_HEADS, DIFF_V)[:, :ts])
        outs["ws"].append(_blockdiag_to_state(ws))
        outs["ss"].append(ss.reshape(bs, SHIFT_WIDTH))

    st = lambda name: jnp.stack(outs[name])
    return (xp.reshape(bp, tp, d), xs.reshape(bs, TS_PAD, d)[:, :ts], st("kp"), st("vp"),
            st("wp"), st("sp"), st("ks"), st("vs"), st("ws"), st("ss"))
```

```python
import functools
import math

import jax
import jax.numpy as jnp
from jax import lax
from jax.experimental import pallas as pl
from jax.experimental.pallas import tpu as pltpu

F32 = jnp.float32
BF16 = jnp.bfloat16

RWKV_WIDTH = 512
DIFF_WIDTH = 512
HEAD = 64
RWKV_HEADS = RWKV_WIDTH // HEAD
LORA = 64
DIFF_HEADS = 4
DIFF_V = 128
NUM_BUCKETS = 32
MAX_EXACT = NUM_BUCKETS // 2
MAX_DISTANCE = 128
PAGE = 128
RMS_EPS = 1e-6
SUBLN_EPS = 1e-5
GN_EPS = 64e-5
NEG_INF = -1e30
LOG2E = 1.4426950408889634
SHIFT_WIDTH = 3 * RWKV_WIDTH + 2 * LORA
SEG_SHIFT = (0, SHIFT_WIDTH)
SEG_GR = (SHIFT_WIDTH, SHIFT_WIDTH + 512)
SEG_Q = (SEG_GR[1], SEG_GR[1] + 512)
SEG_K = (SEG_Q[1], SEG_Q[1] + 512)
SEG_V = (SEG_K[1], SEG_K[1] + 512)
SEG_GD = (SEG_V[1], SEG_V[1] + 512)

VMEM_LIMIT_V7X = 48 * 1024 * 1024
LANES = 128
ONES_ROWS = 16
GROUP_HEADS = 4
GROUP_W = GROUP_HEADS * HEAD
N_GROUPS = RWKV_HEADS // GROUP_HEADS


def _dot(a, b):
    return jnp.dot(a.astype(BF16), b.astype(BF16), preferred_element_type=F32)


def _dot_nt(a, b):
    return lax.dot_general(a.astype(BF16), b.astype(BF16), (((1,), (1,)), ((), ())),
                           preferred_element_type=F32)


def _dot_tn(a, b):
    return lax.dot_general(a.astype(BF16), b.astype(BF16), (((0,), (0,)), ((), ())),
                           preferred_element_type=F32)


def _split(x):
    hi = x.astype(BF16)
    lo = (x - hi.astype(F32)).astype(BF16)
    return hi, lo


def _dot_exact_rhs(a, b_exact):
    hi, lo = _split(a)
    return (jnp.dot(hi, b_exact, preferred_element_type=F32)
            + jnp.dot(lo, b_exact, preferred_element_type=F32))


def _dot_exact_lhs(a_exact, b):
    hi, lo = _split(b)
    return (jnp.dot(a_exact, hi, preferred_element_type=F32)
            + jnp.dot(a_exact, lo, preferred_element_type=F32))


def _sigmoid(x):
    return 1.0 / (1.0 + jnp.exp(-x))


def _softplus(x):
    return jnp.maximum(x, 0.0) + jnp.log(1.0 + jnp.exp(-jnp.abs(x)))


def _rel_bias_minus_far(dist, rb_ref, head):
    n = jnp.maximum(dist, 0)
    nf = jnp.maximum(n, 1).astype(F32)
    large = MAX_EXACT + (jnp.log(nf / MAX_EXACT) / math.log(MAX_DISTANCE / MAX_EXACT)
                         * (NUM_BUCKETS - MAX_EXACT)).astype(jnp.int32)
    large = jnp.minimum(large, NUM_BUCKETS - 1)
    bucket = jnp.where(n < MAX_EXACT, n, large)
    far = rb_ref[NUM_BUCKETS - 1, head]
    out = jnp.zeros(dist.shape, F32)
    for b in range(NUM_BUCKETS - 1):
        out = jnp.where(bucket == b, rb_ref[b, head] - far, out)
    return out


def _projected_segments(x_ref, g_ref, w_ref):
    x = x_ref[...]
    ms = jnp.mean(x * x, axis=-1, keepdims=True)
    h = (x * lax.rsqrt(ms + RMS_EPS) * g_ref[...]).astype(BF16)

    def seg(lo_hi):
        lo, hi = lo_hi
        return jnp.dot(h, w_ref[:, lo:hi], preferred_element_type=F32)

    return seg


Q_SCALE = LOG2E * HEAD ** -0.5


def _inproj_sample_kernel(x_ref, g_ref, w_ref, zsh_ref, gr_ref, q_ref, k_ref, v_ref, gd_ref):
    seg = _projected_segments(x_ref, g_ref, w_ref)
    zsh_ref[...] = seg(SEG_SHIFT)
    gr_ref[...] = seg(SEG_GR)
    q_ref[...] = seg(SEG_Q) * Q_SCALE
    k_ref[...] = seg(SEG_K)
    v_ref[...] = seg(SEG_V)
    gd_ref[...] = seg(SEG_GD)


def _inproj_prompt_kernel(x_ref, g_ref, w_ref, *refs):
    zsh_ref, gr_ref, qt_ref, krows_ref, vrows_ref, gd_ref, kb_ref, vt_ref = refs[-8:]
    seg = _projected_segments(x_ref, g_ref, w_ref)
    tm = x_ref.shape[0]
    zsh_ref[...] = seg(SEG_SHIFT)
    gr_ref[...] = seg(SEG_GR)
    qt_ref[...] = jnp.transpose(seg(SEG_Q) * Q_SCALE).astype(BF16)
    gd_ref[...] = seg(SEG_GD)
    k = seg(SEG_K)
    v = seg(SEG_V)
    kb_ref[...] = k.astype(BF16)
    vt_ref[...] = jnp.transpose(v).astype(BF16)
    for hd in range(DIFF_HEADS):
        cols = slice(hd * DIFF_V, (hd + 1) * DIFF_V)
        krows_ref[pl.ds(hd, tm, stride=DIFF_HEADS), :] = k[:, cols]
        vrows_ref[pl.ds(hd, tm, stride=DIFF_HEADS), :] = v[:, cols]


def _inproj_sample(x, g, w_bf16, tm):
    m, d = x.shape
    row = lambda width: pl.BlockSpec((tm, width), lambda i: (i, 0))
    f32 = jax.ShapeDtypeStruct((m, 512), F32)
    return pl.pallas_call(
        _inproj_sample_kernel,
        grid=(m // tm,),
        in_specs=[row(d), pl.BlockSpec((1, d), lambda i: (0, 0)),
                  pl.BlockSpec(w_bf16.shape, lambda i: (0, 0))],
        out_specs=(row(SHIFT_WIDTH), row(512), row(512), row(512), row(512), row(512)),
        out_shape=(jax.ShapeDtypeStruct((m, SHIFT_WIDTH), F32), f32, f32, f32, f32, f32),
        compiler_params=pltpu.CompilerParams(dimension_semantics=("arbitrary",),
                                             vmem_limit_bytes=VMEM_LIMIT_V7X),
        name="inproj_sample",
    )(x, g, w_bf16)


def _inproj_prompt(x, g, w_bf16, tm, layer, depth, kv_rows):
    m, d = x.shape
    steps = m // tm
    row = lambda width: pl.BlockSpec((tm, width), lambda i: (i, 0))
    col = pl.BlockSpec((512, tm), lambda i: (0, i))
    rows = pl.BlockSpec((tm * DIFF_HEADS, DIFF_V), lambda i: (layer * steps + i, 0))
    f32 = jax.ShapeDtypeStruct((m, 512), F32)
    bf_t = jax.ShapeDtypeStruct((512, m), BF16)
    rows_shape = jax.ShapeDtypeStruct((depth * m * DIFF_HEADS, DIFF_V), F32)
    in_specs = [row(d), pl.BlockSpec((1, d), lambda i: (0, 0)),
                pl.BlockSpec(w_bf16.shape, lambda i: (0, 0))]
    operands = [x, g, w_bf16]
    aliases = {}
    if kv_rows is not None:
        in_specs += [pl.BlockSpec(memory_space=pl.ANY)] * 2
        operands += list(kv_rows)
        aliases = {3: 3, 4: 4}
    return pl.pallas_call(
        _inproj_prompt_kernel,
        grid=(steps,),
        in_specs=in_specs,
        out_specs=(row(SHIFT_WIDTH), row(512), col, rows, rows, row(512), row(512), col),
        out_shape=(jax.ShapeDtypeStruct((m, SHIFT_WIDTH), F32), f32, bf_t, rows_shape, rows_shape,
                   f32, jax.ShapeDtypeStruct((m, 512), BF16), bf_t),
        input_output_aliases=aliases,
        compiler_params=pltpu.CompilerParams(dimension_semantics=("arbitrary",),
                                             vmem_limit_bytes=VMEM_LIMIT_V7X),
        name="inproj_prompt",
    )(*operands)


def _outproj_kernel(yr_ref, yd_ref, w_ref, g_ref, x_ref, o_ref):
    out = (jnp.dot(yr_ref[...], w_ref[0:RWKV_WIDTH, :], preferred_element_type=F32)
           + jnp.dot(yd_ref[...], w_ref[RWKV_WIDTH:, :], preferred_element_type=F32))
    ms = jnp.mean(out * out, axis=-1, keepdims=True)
    o_ref[...] = x_ref[...] + out * lax.rsqrt(ms + RMS_EPS) * g_ref[...]


def _outproj(yr, yd, w_bf16, g, x, tm):
    m, d = x.shape
    row = lambda width: pl.BlockSpec((tm, width), lambda i: (i, 0))
    return pl.pallas_call(
        _outproj_kernel,
        grid=(m // tm,),
        in_specs=[row(512), row(512), pl.BlockSpec(w_bf16.shape, lambda i: (0, 0)),
                  pl.BlockSpec((1, d), lambda i: (0, 0)), row(d)],
        out_specs=row(d),
        out_shape=jax.ShapeDtypeStruct((m, d), F32),
        compiler_params=pltpu.CompilerParams(dimension_semantics=("arbitrary",),
                                             vmem_limit_bytes=VMEM_LIMIT_V7X),
        name="outproj",
    )(yr, yd, w_bf16, g, x)


def _rwkv_kernel(zsh_ref, gr_ref, sprev_ref, s0_ref, mu_ref, w0_ref, w1_ref, a0_ref, w2_ref,
                 kk_ref, ka_ref, rk_ref, lnw_ref, lnb_ref, e_ref,
                 y_ref, sout_ref, shift_ref, s_scr, carry_scr, *, chunk, t_valid):
    C = chunk
    NB = zsh_ref.shape[0]
    R = GROUP_HEADS * C
    c_idx = pl.program_id(1)
    last = pl.num_programs(1) - 1

    @pl.when(c_idx == 0)
    def _():
        s_scr[...] = s0_ref[...]
        carry_scr[...] = sprev_ref[...]

    z = zsh_ref[...].reshape(NB * C, SHIFT_WIDTH)
    row = lax.broadcasted_iota(jnp.int32, (NB * C, 1), 0)
    tok = row % C
    prev = pltpu.roll(z, 1, axis=0)
    last_rows = []
    for b in range(NB):
        prev = jnp.where(row == b * C, carry_scr[b], prev)
        last_rows.append(z[b * C + t_valid - 1:b * C + t_valid, :])
        carry_scr[b] = last_rows[b]
    zm = z + mu_ref[...] * (prev - z)

    r = zm[:, 0:512]
    k = zm[:, 512:1024]
    v = zm[:, 1024:1536]
    zwa = zm[:, 1536:1664]
    lora_w = _dot(jnp.tanh(zwa), w1_ref[...])
    lora_a = _dot(zwa, w2_ref[...])
    w_raw = -_softplus(-(w0_ref[...] + lora_w)) - 0.5
    logw = -jnp.exp(w_raw)
    alr = _sigmoid(a0_ref[...] + lora_a)

    e_ones = e_ref[...]

    def head_sum(x):
        return jnp.concatenate(
            [_dot_exact_rhs(x[:, g * GROUP_W:(g + 1) * GROUP_W], e_ones)
             for g in range(N_GROUPS)], axis=1)

    kk = k * kk_ref[...]
    kk = kk / jnp.maximum(jnp.sqrt(head_sum(kk * kk)), 1e-12)
    k2 = k * (1.0 + (alr - 1.0) * ka_ref[...])
    a_vec = -kk
    b_vec = kk * alr
    if t_valid < C:
        valid = tok < t_valid
        logw = jnp.where(valid, logw, 0.0)
        b_vec = jnp.where(valid, b_vec, 0.0)
        k2 = jnp.where(valid, k2, 0.0)

    ti = lax.broadcasted_iota(jnp.int32, (NB * C, NB * C), 0)
    si = lax.broadcasted_iota(jnp.int32, (NB * C, NB * C), 1)
    ltri = jnp.where(((ti // C) == (si // C)) & (ti >= si), 1.0, 0.0).astype(BF16)
    cum = _dot_exact_lhs(ltri, logw)
    ew_incl = jnp.exp(cum)
    ew_excl = jnp.exp(cum - logw)
    e_inv = jnp.exp(-cum)
    a_t = a_vec * ew_excl
    r_t = r * ew_incl
    b_t = b_vec * e_inv
    k_t = k2 * e_inv

    ri = lax.broadcasted_iota(jnp.int32, (R, GROUP_W), 0)
    li = lax.broadcasted_iota(jnp.int32, (R, GROUP_W), 1)
    one_zero = lambda m: jnp.where(m, 1.0, 0.0).astype(BF16)
    head_mask = one_zero((ri // C) == (li // HEAD))
    wi = lax.broadcasted_iota(jnp.int32, (R, R), 0)
    wj = lax.broadcasted_iota(jnp.int32, (R, R), 1)
    wide_mask = one_zero((wi // C) == (wj // C))
    tt = lax.broadcasted_iota(jnp.int32, (C, R), 0)
    ss = lax.broadcasted_iota(jnp.int32, (C, R), 1) % C
    strict = tt > ss
    incl = tt >= ss
    eye_w = jnp.where(tt == ss, 1.0, 0.0)
    di = lax.broadcasted_iota(jnp.int32, (GROUP_W, GROUP_W), 0)
    dj = lax.broadcasted_iota(jnp.int32, (GROUP_W, GROUP_W), 1)
    state_mask = (di // HEAD) == (dj // HEAD)

    def bd(x, mask):
        if C % 16 == 0:
            return jnp.concatenate([x.astype(BF16)] * GROUP_HEADS, axis=0) * mask
        return (jnp.concatenate([x] * GROUP_HEADS, axis=0) * mask.astype(F32)).astype(BF16)

    def dot_cat(lhs, rhs):
        if all(a.shape[1] % LANES == 0 for a in lhs):
            return _dot(jnp.concatenate(lhs, axis=1), jnp.concatenate(rhs, axis=0))
        return sum(_dot(a, b) for a, b in zip(lhs, rhs))

    units = [(b, g) for b in range(NB) for g in range(N_GROUPS)]
    n_levels = int(math.log2(C))

    ops = []
    for b, g in units:
        rs = slice(b * C, (b + 1) * C)
        ls = slice(g * GROUP_W, (g + 1) * GROUP_W)
        g_c = ew_incl[b * C + C - 1:b * C + C, ls]
        ops.append(dict(
            a=a_t[rs, ls].astype(BF16), r=r_t[rs, ls].astype(BF16),
            bk_hat=jnp.concatenate([b_t[rs, ls] * g_c, k_t[rs, ls] * g_c], axis=0),
            b_bd=bd(b_t[rs, ls], head_mask), k_bd=bd(k_t[rs, ls], head_mask),
            v=v[rs, ls], v_bd=bd(v[rs, ls], head_mask), g_c=g_c, s_prev=s_scr[b, g]))
    for o in ops:
        o["s_bf"] = o["s_prev"].astype(BF16)
        ar = jnp.concatenate([o["a"], o["r"]], axis=0)
        nb = _dot_nt(ar, o["b_bd"])
        nk = _dot_nt(ar, o["k_bd"])
        o["n_ab"] = jnp.where(strict, nb[0:C], 0.0)
        o["m_rb"] = jnp.where(incl, nb[C:2 * C], 0.0)
        o["n_ak"] = jnp.where(strict, nk[0:C], 0.0)
        o["m_rk"] = jnp.where(incl, nk[C:2 * C], 0.0)
    for o in ops:
        o["rhs"] = dot_cat([o["a"], o["n_ak"]], [o["s_bf"], o["v_bd"]])
        o["t_inv"] = eye_w + o["n_ab"]
        o["p"] = _dot(o["n_ab"], bd(o["n_ab"], wide_mask))
    for j in range(1, n_levels - 1):
        for o in ops:
            both = _dot(jnp.concatenate([o["p"], o["t_inv"]], axis=0), bd(o["p"], wide_mask))
            o["p"] = both[0:C]
            o["t_inv"] = o["t_inv"] + both[C:2 * C]
    for o in ops:
        o["t_inv"] = o["t_inv"] + _dot(o["t_inv"], bd(o["p"], wide_mask))
    for o in ops:
        o["u"] = _dot(o["t_inv"], bd(o["rhs"], head_mask))
    y_units = []
    for o in ops:
        y_units.append(dot_cat([o["r"], o["m_rb"], o["m_rk"]],
                               [o["s_bf"], bd(o["u"], head_mask), o["v_bd"]]))
    for (b, g), o in zip(units, ops):
        g_col = jnp.transpose(jnp.broadcast_to(o["g_c"], (GROUP_W, GROUP_W)))
        upd = _dot_tn(o["bk_hat"], jnp.concatenate([o["u"], o["v"]], axis=0))
        s_scr[b, g] = o["s_prev"] * g_col + jnp.where(state_mask, upd, 0.0)

    y = jnp.concatenate(
        [jnp.concatenate(y_units[b * N_GROUPS:(b + 1) * N_GROUPS], axis=-1) for b in range(NB)],
        axis=0)
    inv_n = 1.0 / HEAD
    mean = head_sum(y) * inv_n
    d = y - mean
    var = head_sum(d * d) * inv_n
    yn = d * lax.rsqrt(var + GN_EPS) * lnw_ref[...] + lnb_ref[...]
    bonus = head_sum(r * k2 * rk_ref[...]) * v
    gate = gr_ref[...].reshape(NB * C, RWKV_WIDTH)
    out = (yn + bonus) * (gate * _sigmoid(gate))
    y_ref[...] = out.reshape(NB, C, RWKV_WIDTH).astype(y_ref.dtype)

    @pl.when(c_idx == last)
    def _():
        sout_ref[...] = s_scr[...]
        for b in range(NB):
            shift_ref[b] = last_rows[b]


def _rwkv(zsh, gr, shift_prev, s0_bd, p, chunk, t_valid, seqs_per_step):
    b, t, _ = zsh.shape
    nc = t // chunk
    nb = seqs_per_step
    tok = lambda width: pl.BlockSpec((nb, chunk, width), lambda i, c: (i, c, 0))
    per_b = lambda shape: pl.BlockSpec((nb,) + shape, lambda i, c: (i,) + (0,) * len(shape))
    const = lambda a: pl.BlockSpec(a.shape, lambda i, c: (0,) * a.ndim)
    params = (p["mu"], p["w0"], p["w1"], p["a0"], p["w2"], p["k_k"], p["k_a"], p["r_k"],
              p["lnx_w"], p["lnx_b"], p["e_ones"])
    return pl.pallas_call(
        functools.partial(_rwkv_kernel, chunk=chunk, t_valid=t_valid),
        grid=(b // nb, nc),
        in_specs=[tok(SHIFT_WIDTH), tok(512), per_b((1, SHIFT_WIDTH)),
                  per_b((N_GROUPS, GROUP_W, GROUP_W))] + [const(a) for a in params],
        out_specs=(tok(512), per_b((N_GROUPS, GROUP_W, GROUP_W)), per_b((1, SHIFT_WIDTH))),
        out_shape=(jax.ShapeDtypeStruct((b, t, 512), BF16),
                   jax.ShapeDtypeStruct((b, N_GROUPS, GROUP_W, GROUP_W), F32),
                   jax.ShapeDtypeStruct((b, 1, SHIFT_WIDTH), F32)),
        scratch_shapes=[pltpu.VMEM((nb, N_GROUPS, GROUP_W, GROUP_W), F32),
                        pltpu.VMEM((nb, 1, SHIFT_WIDTH), F32)],
        compiler_params=pltpu.CompilerParams(dimension_semantics=("arbitrary", "arbitrary"),
                                             vmem_limit_bytes=VMEM_LIMIT_V7X),
        name="rwkv7",
    )(zsh, gr, shift_prev, s0_bd, *params)


def _state_to_blockdiag(s):
    b = s.shape[0]
    st = jnp.swapaxes(s, -1, -2).reshape(b, N_GROUPS, GROUP_HEADS, HEAD, HEAD)
    eye = jnp.eye(GROUP_HEADS, dtype=s.dtype)
    bd = st[:, :, :, :, None, :] * eye[None, None, :, None, :, None]
    return bd.reshape(b, N_GROUPS, GROUP_W, GROUP_W)


def _blockdiag_to_state(bd):
    b = bd.shape[0]
    x = bd.reshape(b, N_GROUPS, GROUP_HEADS, HEAD, GROUP_HEADS, HEAD)
    idx = jnp.arange(GROUP_HEADS)
    blocks = x[:, :, idx, :, idx, :]
    blocks = jnp.moveaxis(blocks, 0, 2).reshape(b, RWKV_HEADS, HEAD, HEAD)
    return jnp.swapaxes(blocks, -1, -2)


def _lambda(lq1_ref, lk1_ref, lq2_ref, lk2_ref, lam_init):
    s1 = jnp.sum(lq1_ref[...] * lk1_ref[...], axis=-1, keepdims=True)
    s2 = jnp.sum(lq2_ref[...] * lk2_ref[...], axis=-1, keepdims=True)
    return jnp.exp(s1) - jnp.exp(s2) + lam_init


def _diff_epilogue(acc0, l0, acc1, l1, lam, lam_init, subln, gate):
    o = acc0 / l0 - lam * (acc1 / l1)
    ms = jnp.mean(o * o, axis=-1, keepdims=True)
    on = o * lax.rsqrt(ms + SUBLN_EPS) * subln * (1.0 - lam_init)
    return on * (gate * _sigmoid(gate))


def _attn_kernel(rb_ref, li_ref, qt_ref, k_ref, vt_ref, gd_ref, lq1_ref, lk1_ref, lq2_ref,
                 lk2_ref, sub_ref, o_ref, bias_scr, s_scr, acc_scr, *, tq, tk):
    h = pl.program_id(1)
    qi = pl.program_id(2)
    nsub = tq // tk
    first_near = nsub * qi - 1

    @pl.when(qi == 0)
    def _():
        key = lax.broadcasted_iota(jnp.int32, (tk, tq), 0)
        qry = lax.broadcasted_iota(jnp.int32, (tk, tq), 1)
        for i in range(nsub + 1):
            dist = qry - key - (i - 1) * tk
            bias = _rel_bias_minus_far(dist, rb_ref, h) * LOG2E
            bias = jnp.where(dist >= 0, bias, NEG_INF)
            bias_scr[i] = jnp.concatenate([bias, bias], axis=1)

    qt = qt_ref[...]
    feat = lax.broadcasted_iota(jnp.int32, (2 * HEAD, tq), 0)
    qt2 = jnp.concatenate([jnp.where(feat < HEAD, qt, jnp.zeros_like(qt)),
                           jnp.where(feat >= HEAD, qt, jnp.zeros_like(qt))], axis=1)

    acc_scr[...] = jnp.zeros(acc_scr.shape, F32)
    ones_rows = jnp.ones((ONES_ROWS, tk), BF16)

    def logits(kj):
        off = pl.multiple_of(kj * tk, tk)
        return jnp.dot(k_ref[pl.ds(off, tk), :], qt2, preferred_element_type=F32)

    def absorb(s, kj, m_old):
        off = pl.multiple_of(kj * tk, tk)
        vt_ext = jnp.concatenate([vt_ref[:, pl.ds(off, tk)], ones_rows], axis=0)
        m_new = jnp.maximum(m_old, jnp.max(s, axis=0, keepdims=True))
        pm = jnp.exp2(s - m_new).astype(BF16)
        acc_scr[...] = (jnp.exp2(m_old - m_new) * acc_scr[...]
                        + jnp.dot(vt_ext, pm, preferred_element_type=F32))
        return m_new

    assert nsub == 2
    m_run = jnp.full((1, 2 * tq), NEG_INF, F32)

    @pl.when(qi >= 1)
    def _():
        s_scr[0] = logits(0)

    def far_pair(i, m):
        s_scr[1] = logits(2 * i + 1)
        m = absorb(s_scr[0], 2 * i, m)
        s_scr[0] = logits(2 * i + 2)
        return absorb(s_scr[1], 2 * i + 1, m)

    m_run = lax.fori_loop(0, qi - 1, far_pair, m_run)

    def last_pair(i, m):
        s_scr[1] = logits(first_near)
        m = absorb(s_scr[0], first_near - 1, m)
        return absorb(s_scr[1] + bias_scr[0], first_near, m)

    m_run = lax.fori_loop(0, jnp.minimum(qi, 1), last_pair, m_run)
    s_diag = [logits(first_near + 1 + i) for i in range(nsub)]
    for i in range(nsub):
        m_run = absorb(s_diag[i] + bias_scr[1 + i], first_near + 1 + i, m_run)

    lam_init = li_ref[0]
    lam = _lambda(lq1_ref, lk1_ref, lq2_ref, lk2_ref, lam_init)
    l_fin = acc_scr[DIFF_V:DIFF_V + 1, :]
    o_t = (acc_scr[0:DIFF_V, 0:tq] / l_fin[:, 0:tq]
           - lam * (acc_scr[0:DIFF_V, tq:2 * tq] / l_fin[:, tq:2 * tq]))
    ms = jnp.mean(o_t * o_t, axis=0, keepdims=True)
    o = jnp.transpose(o_t * lax.rsqrt(ms + SUBLN_EPS))
    gate = gd_ref[...]
    out = o * sub_ref[...] * (1.0 - lam_init) * (gate * _sigmoid(gate))
    o_ref[...] = out.astype(o_ref.dtype)


def _attn_prompt(qt, kb, vt, gd, rel_bias, lam_init, lam_vecs, subln, batch, tq, tk):
    m = kb.shape[0]
    t = m // batch
    nq = t // tq
    assert tk >= MAX_DISTANCE and tq % tk == 0
    tile = pl.BlockSpec((tq, DIFF_V), lambda b, h, i: (b * nq + i, h))
    tile_t = pl.BlockSpec((DIFF_V, tq), lambda b, h, i: (h, b * nq + i))
    seq = pl.BlockSpec((t, DIFF_V), lambda b, h, i: (b, h))
    seq_t = pl.BlockSpec((DIFF_V, t), lambda b, h, i: (h, b))
    vec = lambda a: pl.BlockSpec(a.shape, lambda b, h, i: (0,) * a.ndim)
    smem = pl.BlockSpec(memory_space=pltpu.SMEM)
    return pl.pallas_call(
        functools.partial(_attn_kernel, tq=tq, tk=tk),
        grid=(batch, DIFF_HEADS, nq),
        in_specs=([smem, smem, tile_t, seq, seq_t, tile] + [vec(a) for a in lam_vecs]
                  + [vec(subln)]),
        out_specs=tile,
        out_shape=jax.ShapeDtypeStruct((m, DIFF_WIDTH), BF16),
        scratch_shapes=[pltpu.VMEM((tq // tk + 1, tk, 2 * tq), F32),
                        pltpu.VMEM((2, tk, 2 * tq), F32),
                        pltpu.VMEM((DIFF_V + ONES_ROWS, 2 * tq), F32)],
        compiler_params=pltpu.CompilerParams(
            dimension_semantics=("arbitrary", "arbitrary", "arbitrary"),
            vmem_limit_bytes=VMEM_LIMIT_V7X),
        name="diff_attn_prompt",
    )(rel_bias, lam_init, qt, kb, vt, gd, *lam_vecs, subln)


TOKENS_PADDED = 8
SCORE_ROWS = DIFF_HEADS * 2 * TOKENS_PADDED
PAGE_ROWS = PAGE * DIFF_HEADS
PAGES_PER_SOFTMAX_STEP = 2


def _decode_kernel(pt_ref, rb_ref, li_ref, q_ref, kn_ref, vn_ref, gd_ref, lq1_ref, lk1_ref,
                   lq2_ref, lk2_ref, sub_ref, *rest, pages_per_step):
    P = pages_per_step
    k_refs = rest[0:P]
    v_refs = rest[P:2 * P]
    o_ref = rest[2 * P]
    m_scr, acc_scr = rest[2 * P + 1:]
    del pt_ref
    j = pl.program_id(1)
    last = pl.num_programs(1) - 1
    TP = TOKENS_PADDED
    NR = SCORE_ROWS

    @pl.when(j == 0)
    def _():
        m_scr[...] = jnp.full(m_scr.shape, NEG_INF, F32)
        acc_scr[...] = jnp.zeros(acc_scr.shape, F32)

    q = q_ref[...]
    lane = lax.broadcasted_iota(jnp.int32, (TP, DIFF_V), 1)
    q_rows = jnp.concatenate(
        [jnp.where((lane // HEAD) == c, q[:, h * DIFF_V:(h + 1) * DIFF_V], 0.0)
         for h in range(DIFF_HEADS) for c in range(2)], axis=0)
    q_rows = q_rows.astype(BF16).astype(F32)
    row = lax.broadcasted_iota(jnp.int32, (NR, 1), 0)
    row_head = row // (2 * TP)
    row_tok = row % TP

    def scores(k_rows, bias=None, causal=False):
        n = k_rows.shape[0]
        s = lax.dot_general(q_rows, k_rows, (((1,), (1,)), ((), ())),
                            preferred_element_type=F32)
        col = lax.broadcasted_iota(jnp.int32, (1, n), 1)
        keep = (col % DIFF_HEADS) == row_head
        if causal:
            keep = keep & (row_tok >= col // DIFF_HEADS)
        if bias is not None:
            s = s + bias
        return jnp.where(keep, s, NEG_INF)

    def row_bias(n, dist_of):
        col_tok = lax.broadcasted_iota(jnp.int32, (1, n), 1) // DIFF_HEADS
        t_row = lax.broadcasted_iota(jnp.int32, (2 * TP, 1), 0) % TP
        dist = dist_of(t_row, col_tok)
        return jnp.concatenate([_rel_bias_minus_far(dist, rb_ref, h) for h in range(DIFF_HEADS)],
                               axis=0) * LOG2E

    def update(s_list, v_list):
        m_old = m_scr[...]
        m_new = m_old
        for s in s_list:
            m_new = jnp.maximum(m_new, jnp.max(s, axis=-1, keepdims=True))
        acc = jnp.exp2(m_old - m_new) * acc_scr[...]
        for s, v in zip(s_list, v_list):
            v_ext = jnp.concatenate([v, jnp.ones(v.shape, F32)], axis=-1)
            acc = acc + jnp.dot(jnp.exp2(s - m_new), v_ext, preferred_element_type=F32)
        acc_scr[...] = acc
        m_scr[...] = m_new

    page_k = lambda i: k_refs[i][...]
    page_v = lambda i: v_refs[i][...]
    G = PAGES_PER_SOFTMAX_STEP

    @pl.when(j != last)
    def _():
        for i0 in range(0, P, G):
            update([scores(page_k(i)) for i in range(i0, i0 + G)],
                   [page_v(i) for i in range(i0, i0 + G)])

    @pl.when(j == last)
    def _():
        bias_p = row_bias(PAGE_ROWS, lambda t_row, col_tok: (PAGE + t_row) - col_tok)
        n_new = TP * DIFF_HEADS
        bias_n = row_bias(n_new, lambda t_row, col_tok: t_row - col_tok)
        for i0 in range(0, P - G, G):
            update([scores(page_k(i)) for i in range(i0, i0 + G)],
                   [page_v(i) for i in range(i0, i0 + G)])
        s_list = [scores(page_k(i)) for i in range(P - G, P - 1)]
        s_list.append(scores(page_k(P - 1), bias=bias_p))
        s_list.append(scores(kn_ref[...], bias=bias_n, causal=True))
        update(s_list, [page_v(i) for i in range(P - G, P)] + [vn_ref[...]])

        lam_init = li_ref[0]
        lam = _lambda(lq1_ref, lk1_ref, lq2_ref, lk2_ref, lam_init)
        gate = gd_ref[...]
        outs = []
        for h in range(DIFF_HEADS):
            r0 = h * 2 * TP
            r1 = r0 + TP
            outs.append(_diff_epilogue(
                acc_scr[r0:r0 + TP, 0:DIFF_V], acc_scr[r0:r0 + TP, DIFF_V:DIFF_V + 1],
                acc_scr[r1:r1 + TP, 0:DIFF_V], acc_scr[r1:r1 + TP, DIFF_V:DIFF_V + 1],
                lam, lam_init, sub_ref[...], gate[:, h * DIFF_V:(h + 1) * DIFF_V]))
        o_ref[...] = jnp.concatenate(outs, axis=-1).astype(o_ref.dtype)


def _attn_sample(q, kn, vn, gd, cache_k, cache_v, page_ids, rel_bias, lam_init, lam_vecs, subln,
                 pages_per_step):
    b = q.shape[0]
    n_pages = page_ids.shape[0] // b
    P = pages_per_step
    per_b = lambda shape: pl.BlockSpec((None,) + shape, lambda i, j, pt: (i, 0, 0))
    vec = lambda a: pl.BlockSpec(a.shape, lambda i, j, pt: (0,) * a.ndim)
    smem = pl.BlockSpec(memory_space=pltpu.SMEM)
    tok = per_b((TOKENS_PADDED, DIFF_WIDTH))
    new = per_b((TOKENS_PADDED * DIFF_HEADS, DIFF_V))

    def page_spec(slot):
        return pl.BlockSpec((None, PAGE_ROWS, DIFF_V),
                            lambda i, j, pt: (pt[i * n_pages + j * P + slot], 0, 0))

    grid_spec = pltpu.PrefetchScalarGridSpec(
        num_scalar_prefetch=1,
        grid=(b, n_pages // P),
        in_specs=([smem, smem, tok, new, new, tok] + [vec(a) for a in lam_vecs]
                  + [vec(subln)] + [page_spec(s) for s in range(P)]
                  + [page_spec(s) for s in range(P)]),
        out_specs=tok,
        scratch_shapes=[pltpu.VMEM((SCORE_ROWS, 1), F32),
                        pltpu.VMEM((SCORE_ROWS, 2 * DIFF_V), F32)],
    )
    return pl.pallas_call(
        functools.partial(_decode_kernel, pages_per_step=P),
        grid_spec=grid_spec,
        out_shape=jax.ShapeDtypeStruct((b, TOKENS_PADDED, DIFF_WIDTH), BF16),
        compiler_params=pltpu.CompilerParams(dimension_semantics=("arbitrary", "arbitrary"),
                                             vmem_limit_bytes=VMEM_LIMIT_V7X),
        name="diff_attn_sample",
    )(page_ids, rel_bias, lam_init, q, kn, vn, gd, *lam_vecs, subln,
      *([cache_k] * P), *([cache_v] * P))


def _layer_params(l, pre_norm, post_norm, w_in, mu_shift, w0, w_up, a0, a_up, k_k, k_a, r_k,
                  lnx_w, lnx_b, lam_q1, lam_k1, lam_q2, lam_k2, subln, w_out):
    row = lambda a: a.reshape(1, -1)
    zeros = jnp.zeros((LORA, RWKV_WIDTH), F32)
    head_of = jnp.arange(GROUP_W) // HEAD
    return dict(
        pre_norm=row(pre_norm[l]), post_norm=row(post_norm[l]),
        w_in=w_in[l].astype(BF16), w_out=w_out[l].astype(BF16),
        mu=row(mu_shift[l]), w0=row(w0[l]), a0=row(a0[l]),
        w1=jnp.concatenate([w_up[l], zeros], axis=0).astype(BF16),
        w2=jnp.concatenate([zeros, a_up[l]], axis=0).astype(BF16),
        k_k=row(k_k[l]), k_a=row(k_a[l]), r_k=row(r_k[l]),
        lnx_w=row(lnx_w[l]), lnx_b=row(lnx_b[l]),
        e_ones=(head_of[:, None] == head_of[None, :]).astype(BF16),
        lam_vecs=(row(lam_q1[l]), row(lam_k1[l]), row(lam_q2[l]), row(lam_k2[l])),
        subln=row(subln[l]),
        lam_init=jnp.full((1,), 0.8 - 0.6 * math.exp(-0.3 * l), F32),
    )


def _mixer(x2d, batch, p, shift_prev, s0_bd, project, attend, tm_out, chunk, t_valid,
           seqs_per_step):
    m = x2d.shape[0]
    t = m // batch
    zsh, gr, gd, attn_operands = project(x2d)
    y_r, s_bd, shift = _rwkv(zsh.reshape(batch, t, SHIFT_WIDTH), gr.reshape(batch, t, 512),
                             shift_prev, s0_bd, p, chunk, t_valid, seqs_per_step)
    y_d = attend(gd, *attn_operands)
    x_new = _outproj(y_r.reshape(m, RWKV_WIDTH), y_d.reshape(m, DIFF_WIDTH), p["w_out"],
                     p["post_norm"], x2d, tm_out)
    return x_new, shift, s_bd, attn_operands


def kernel(x_prompt, x_sample, cache_k, cache_v, state_wkv, state_shift, page_table, rel_bias,
           pre_norm, post_norm, w_in, mu_shift, w0, w_up, a0, a_up, k_k, k_a, r_k, lnx_w, lnx_b,
           lam_q1, lam_k1, lam_q2, lam_k2, subln, w_out):
    return _forward(x_prompt, x_sample, cache_k, cache_v, state_wkv, state_shift, page_table,
                    rel_bias, pre_norm, post_norm, w_in, mu_shift, w0, w_up, a0, a_up, k_k, k_a,
                    r_k, lnx_w, lnx_b, lam_q1, lam_k1, lam_q2, lam_k2, subln, w_out,
                    PROMPT_TM=256, PROMPT_TM_OUT=512, PROMPT_CHUNK=64, PROMPT_TQ=512,
                    PROMPT_TK=256, PAGES_PER_STEP=16, SAMPLE_SEQS_PER_STEP=4)


def _forward(x_prompt, x_sample, cache_k, cache_v, state_wkv, state_shift, page_table, rel_bias,
             pre_norm, post_norm, w_in, mu_shift, w0, w_up, a0, a_up, k_k, k_a, r_k, lnx_w, lnx_b,
             lam_q1, lam_k1, lam_q2, lam_k2, subln, w_out, *, PROMPT_TM, PROMPT_TM_OUT,
             PROMPT_CHUNK, PROMPT_TQ, PROMPT_TK, PAGES_PER_STEP, SAMPLE_SEQS_PER_STEP):
    bp, tp, d = x_prompt.shape
    bs, ts, _ = x_sample.shape
    depth = w_in.shape[0]
    n_pool = cache_k.shape[1]
    n_pages = page_table.shape[1]
    TS_PAD = TOKENS_PADDED

    xp = x_prompt.reshape(bp * tp, d)
    xs = jnp.pad(x_sample, ((0, 0), (0, TS_PAD - ts), (0, 0))).reshape(bs * TS_PAD, d)
    ck = cache_k.reshape(depth * n_pool, PAGE_ROWS, DIFF_V)
    cv = cache_v.reshape(depth * n_pool, PAGE_ROWS, DIFF_V)
    zero_shift = jnp.zeros((bp, 1, SHIFT_WIDTH), F32)
    zero_state = jnp.zeros((bp, N_GROUPS, GROUP_W, GROUP_W), F32)

    outs = {name: [] for name in ("wp", "sp", "ks", "vs", "ws", "ss")}
    kv_rows = None
    for l in range(depth):
        p = _layer_params(l, pre_norm, post_norm, w_in, mu_shift, w0, w_up, a0, a_up, k_k, k_a,
                          r_k, lnx_w, lnx_b, lam_q1, lam_k1, lam_q2, lam_k2, subln, w_out)

        def project_prompt(x2d):
            zsh, gr, qt, k_rows, v_rows, gd, kb, vt = _inproj_prompt(
                x2d, p["pre_norm"], p["w_in"], PROMPT_TM, l, depth, kv_rows)
            return zsh, gr, gd, (qt, kb, vt, k_rows, v_rows)

        def attend_prompt(gd, qt, kb, vt, k_rows, v_rows):
            return _attn_prompt(qt, kb, vt, gd, rel_bias, p["lam_init"], p["lam_vecs"],
                                p["subln"], bp, PROMPT_TQ, PROMPT_TK)

        def project_sample(x2d):
            zsh, gr, q, k, v, gd = _inproj_sample(x2d, p["pre_norm"], p["w_in"], bs * TS_PAD)
            return zsh, gr, gd, (q, k, v)

        page_ids = (page_table + l * n_pool).reshape(-1).astype(jnp.int32)

        def attend_sample(gd, q, k, v):
            r3 = lambda a: a.reshape(bs, TS_PAD, DIFF_WIDTH)
            rows = lambda a: a.reshape(bs, TS_PAD * DIFF_HEADS, DIFF_V)
            return _attn_sample(r3(q), rows(k), rows(v), r3(gd), ck, cv, page_ids, rel_bias,
                                p["lam_init"], p["lam_vecs"], p["subln"], PAGES_PER_STEP)

        xp, sp, wp, prompt_ops = _mixer(xp, bp, p, zero_shift, zero_state, project_prompt,
                                        attend_prompt, PROMPT_TM_OUT, PROMPT_CHUNK, PROMPT_CHUNK,
                                        bp)
        kv_rows = prompt_ops[3:5]
        xs, ss, ws, (_, k_s, v_s) = _mixer(xs, bs, p, state_shift[l][:, None, :],
                                           _state_to_blockdiag(state_wkv[l]), project_sample,
                                           attend_sample, bs * TS_PAD, TS_PAD, ts,
                                           SAMPLE_SEQS_PER_STEP)
        outs["wp"].append(_blockdiag_to_state(wp))
        outs["sp"].append(sp.reshape(bp, SHIFT_WIDTH))
        outs["ks"].append(k_s.reshape(bs, TS_PAD, DIFF_HEADS, DIFF_V)[:, :ts])
        outs["vs"].append(v_s.reshape(bs, TS_PAD, DIFF_HEADS, DIFF_V)[:, :ts])
        outs["ws"].append(_blockdiag_to_state(ws))
        outs["ss"].append(ss.reshape(bs, SHIFT_WIDTH))

    st = lambda name: jnp.stack(outs[name])
    kv_shape = (depth, bp, tp, DIFF_HEADS, DIFF_V)
    return (xp.reshape(bp, tp, d), xs.reshape(bs, TS_PAD, d)[:, :ts], kv_rows[0].reshape(kv_shape),
            kv_rows[1].reshape(kv_shape), st("wp"), st("sp"), st("ks"), st("vs"), st("ws"),
            st("ss"))
```

```python
import functools
import math

import jax
import jax.numpy as jnp
from jax import lax
from jax.experimental import pallas as pl
from jax.experimental.pallas import tpu as pltpu

F32 = jnp.float32
BF16 = jnp.bfloat16

RWKV_WIDTH = 512
DIFF_WIDTH = 512
HEAD = 64
RWKV_HEADS = RWKV_WIDTH // HEAD
LORA = 64
DIFF_HEADS = 4
DIFF_V = 128
NUM_BUCKETS = 32
MAX_EXACT = NUM_BUCKETS // 2
MAX_DISTANCE = 128
PAGE = 128
RMS_EPS = 1e-6
SUBLN_EPS = 1e-5
GN_EPS = 64e-5
NEG_INF = -1e30
LOG2E = 1.4426950408889634
SHIFT_WIDTH = 3 * RWKV_WIDTH + 2 * LORA
SEG_SHIFT = (0, SHIFT_WIDTH)
SEG_GR = (SHIFT_WIDTH, SHIFT_WIDTH + 512)
SEG_Q = (SEG_GR[1], SEG_GR[1] + 512)
SEG_K = (SEG_Q[1], SEG_Q[1] + 512)
SEG_V = (SEG_K[1], SEG_K[1] + 512)
SEG_GD = (SEG_V[1], SEG_V[1] + 512)

VMEM_LIMIT_V7X = 48 * 1024 * 1024
LANES = 128
ONES_ROWS = 16
GROUP_HEADS = 4
GROUP_W = GROUP_HEADS * HEAD
N_GROUPS = RWKV_HEADS // GROUP_HEADS


def _dot(a, b):
    return jnp.dot(a.astype(BF16), b.astype(BF16), preferred_element_type=F32)


def _dot_nt(a, b):
    return lax.dot_general(a.astype(BF16), b.astype(BF16), (((1,), (1,)), ((), ())),
                           preferred_element_type=F32)


def _dot_tn(a, b):
    return lax.dot_general(a.astype(BF16), b.astype(BF16), (((0,), (0,)), ((), ())),
                           preferred_element_type=F32)


def _split(x):
    hi = x.astype(BF16)
    lo = (x - hi.astype(F32)).astype(BF16)
    return hi, lo


def _dot_exact_rhs(a, b_exact):
    hi, lo = _split(a)
    return (jnp.dot(hi, b_exact, preferred_element_type=F32)
            + jnp.dot(lo, b_exact, preferred_element_type=F32))


def _dot_exact_lhs(a_exact, b):
    hi, lo = _split(b)
    return (jnp.dot(a_exact, hi, preferred_element_type=F32)
            + jnp.dot(a_exact, lo, preferred_element_type=F32))


def _sigmoid(x):
    return 1.0 / (1.0 + jnp.exp(-x))


def _softplus(x):
    return jnp.maximum(x, 0.0) + jnp.log(1.0 + jnp.exp(-jnp.abs(x)))


def _rel_bias_minus_far(dist, rb_ref, head):
    n = jnp.maximum(dist, 0)
    nf = jnp.maximum(n, 1).astype(F32)
    large = MAX_EXACT + (jnp.log(nf / MAX_EXACT) / math.log(MAX_DISTANCE / MAX_EXACT)
                         * (NUM_BUCKETS - MAX_EXACT)).astype(jnp.int32)
    large = jnp.minimum(large, NUM_BUCKETS - 1)
    bucket = jnp.where(n < MAX_EXACT, n, large)
    far = rb_ref[NUM_BUCKETS - 1, head]
    out = jnp.zeros(dist.shape, F32)
    for b in range(NUM_BUCKETS - 1):
        out = jnp.where(bucket == b, rb_ref[b, head] - far, out)
    return out


def _projected_segments(x_ref, g_ref, w_ref):
    x = x_ref[...]
    ms = jnp.mean(x * x, axis=-1, keepdims=True)
    h = (x * lax.rsqrt(ms + RMS_EPS) * g_ref[...]).astype(BF16)

    def seg(lo_hi):
        lo, hi = lo_hi
        return jnp.dot(h, w_ref[:, lo:hi], preferred_element_type=F32)

    return seg


Q_SCALE = LOG2E * HEAD ** -0.5


def _inproj_sample_kernel(x_ref, g_ref, w_ref, zsh_ref, gr_ref, q_ref, k_ref, v_ref, gd_ref):
    seg = _projected_segments(x_ref, g_ref, w_ref)
    zsh_ref[...] = seg(SEG_SHIFT)
    gr_ref[...] = seg(SEG_GR)
    q_ref[...] = seg(SEG_Q) * Q_SCALE
    k_ref[...] = seg(SEG_K)
    v_ref[...] = seg(SEG_V)
    gd_ref[...] = seg(SEG_GD)


def _inproj_prompt_kernel(x_ref, g_ref, w_ref, *refs):
    zsh_ref, gr_ref, qt_ref, krows_ref, vrows_ref, gd_ref, kb_ref, vt_ref = refs[-8:]
    seg = _projected_segments(x_ref, g_ref, w_ref)
    tm = x_ref.shape[0]
    zsh_ref[...] = seg(SEG_SHIFT)
    gr_ref[...] = seg(SEG_GR)
    qt_ref[...] = jnp.transpose(seg(SEG_Q) * Q_SCALE).astype(BF16)
    gd_ref[...] = seg(SEG_GD)
    k = seg(SEG_K)
    v = seg(SEG_V)
    kb_ref[...] = k.astype(BF16)
    vt_ref[...] = jnp.transpose(v).astype(BF16)
    for hd in range(DIFF_HEADS):
        cols = slice(hd * DIFF_V, (hd + 1) * DIFF_V)
        krows_ref[pl.ds(hd, tm, stride=DIFF_HEADS), :] = k[:, cols]
        vrows_ref[pl.ds(hd, tm, stride=DIFF_HEADS), :] = v[:, cols]


def _inproj_sample(x, g, w_bf16, tm):
    m, d = x.shape
    row = lambda width: pl.BlockSpec((tm, width), lambda i: (i, 0))
    f32 = jax.ShapeDtypeStruct((m, 512), F32)
    return pl.pallas_call(
        _inproj_sample_kernel,
        grid=(m // tm,),
        in_specs=[row(d), pl.BlockSpec((1, d), lambda i: (0, 0)),
                  pl.BlockSpec(w_bf16.shape, lambda i: (0, 0))],
        out_specs=(row(SHIFT_WIDTH), row(512), row(512), row(512), row(512), row(512)),
        out_shape=(jax.ShapeDtypeStruct((m, SHIFT_WIDTH), F32), f32, f32, f32, f32, f32),
        compiler_params=pltpu.CompilerParams(dimension_semantics=("arbitrary",),
                                             vmem_limit_bytes=VMEM_LIMIT_V7X),
        name="inproj_sample",
    )(x, g, w_bf16)


def _inproj_prompt(x, g, w_bf16, tm, layer, depth, kv_rows):
    m, d = x.shape
    steps = m // tm
    row = lambda width: pl.BlockSpec((tm, width), lambda i: (i, 0))
    col = pl.BlockSpec((512, tm), lambda i: (0, i))
    rows = pl.BlockSpec((tm * DIFF_HEADS, DIFF_V), lambda i: (layer * steps + i, 0))
    f32 = jax.ShapeDtypeStruct((m, 512), F32)
    bf_t = jax.ShapeDtypeStruct((512, m), BF16)
    rows_shape = jax.ShapeDtypeStruct((depth * m * DIFF_HEADS, DIFF_V), F32)
    in_specs = [row(d), pl.BlockSpec((1, d), lambda i: (0, 0)),
                pl.BlockSpec(w_bf16.shape, lambda i: (0, 0))]
    operands = [x, g, w_bf16]
    aliases = {}
    if kv_rows is not None:
        in_specs += [pl.BlockSpec(memory_space=pl.ANY)] * 2
        operands += list(kv_rows)
        aliases = {3: 3, 4: 4}
    return pl.pallas_call(
        _inproj_prompt_kernel,
        grid=(steps,),
        in_specs=in_specs,
        out_specs=(row(SHIFT_WIDTH), row(512), col, rows, rows, row(512), row(512), col),
        out_shape=(jax.ShapeDtypeStruct((m, SHIFT_WIDTH), F32), f32, bf_t, rows_shape, rows_shape,
                   f32, jax.ShapeDtypeStruct((m, 512), BF16), bf_t),
        input_output_aliases=aliases,
        compiler_params=pltpu.CompilerParams(dimension_semantics=("arbitrary",),
                                             vmem_limit_bytes=VMEM_LIMIT_V7X),
        name="inproj_prompt",
    )(*operands)


def _outproj_kernel(yr_ref, yd_ref, w_ref, g_ref, x_ref, o_ref):
    out = (jnp.dot(yr_ref[...], w_ref[0:RWKV_WIDTH, :], preferred_element_type=F32)
           + jnp.dot(yd_ref[...], w_ref[RWKV_WIDTH:, :], preferred_element_type=F32))
    ms = jnp.mean(out * out, axis=-1, keepdims=True)
    o_ref[...] = x_ref[...] + out * lax.rsqrt(ms + RMS_EPS) * g_ref[...]


def _outproj(yr, yd, w_bf16, g, x, tm):
    m, d = x.shape
    row = lambda width: pl.BlockSpec((tm, width), lambda i: (i, 0))
    return pl.pallas_call(
        _outproj_kernel,
        grid=(m // tm,),
        in_specs=[row(512), row(512), pl.BlockSpec(w_bf16.shape, lambda i: (0, 0)),
                  pl.BlockSpec((1, d), lambda i: (0, 0)), row(d)],
        out_specs=row(d),
        out_shape=jax.ShapeDtypeStruct((m, d), F32),
        compiler_params=pltpu.CompilerParams(dimension_semantics=("arbitrary",),
                                             vmem_limit_bytes=VMEM_LIMIT_V7X),
        name="outproj",
    )(yr, yd, w_bf16, g, x)


def _rwkv_kernel(zsh_ref, gr_ref, sprev_ref, s0_ref, mu_ref, w0_ref, w1_ref, a0_ref, w2_ref,
                 kk_ref, ka_ref, rk_ref, lnw_ref, lnb_ref, e_ref,
                 y_ref, sout_ref, shift_ref, s_scr, carry_scr, *, chunk, t_valid):
    C = chunk
    NB = zsh_ref.shape[0]
    R = GROUP_HEADS * C
    c_idx = pl.program_id(1)
    last = pl.num_programs(1) - 1

    ii = lax.broadcasted_iota(jnp.int32, (GROUP_W, GROUP_W), 0)
    jj = lax.broadcasted_iota(jnp.int32, (GROUP_W, GROUP_W), 1)
    eye_g = jnp.where(ii == jj, 1.0, 0.0).astype(BF16)
    block_g = (ii // HEAD) == (jj // HEAD)

    def exact_nt(lhs_exact, x):
        hi = x.astype(BF16)
        r1 = x - hi.astype(F32)
        mid = r1.astype(BF16)
        lo = (r1 - mid.astype(F32)).astype(BF16)
        return _dot_nt(lhs_exact, hi) + _dot_nt(lhs_exact, mid) + _dot_nt(lhs_exact, lo)

    @pl.when(c_idx == 0)
    def _():
        carry_scr[...] = sprev_ref[...]
        for b in range(NB):
            for g in range(N_GROUPS):
                heads = s0_ref[b, g * GROUP_HEADS:(g + 1) * GROUP_HEADS]
                s_t = exact_nt(eye_g[0:HEAD, 0:HEAD], heads.reshape(GROUP_W, HEAD))
                s_scr[b, g] = jnp.where(block_g, jnp.concatenate([s_t] * GROUP_HEADS, axis=0),
                                        0.0)

    CH = zsh_ref.shape[1] // C
    NS = NB * CH
    assert CH == 1 or t_valid == C
    z = zsh_ref[...].reshape(NS * C, SHIFT_WIDTH)
    row = lax.broadcasted_iota(jnp.int32, (NS * C, 1), 0)
    tok = row % C
    prev = pltpu.roll(z, 1, axis=0)
    last_rows = []
    for b in range(NB):
        prev = jnp.where(row == b * CH * C, carry_scr[b], prev)
        last_tok = b * CH * C + (CH - 1) * C + t_valid - 1
        last_rows.append(z[last_tok:last_tok + 1, :])
        carry_scr[b] = last_rows[b]
    zm = z + mu_ref[...] * (prev - z)

    r = zm[:, 0:512]
    k = zm[:, 512:1024]
    v = zm[:, 1024:1536]
    zwa = zm[:, 1536:1664]
    lora_w = _dot(jnp.tanh(zwa), w1_ref[...])
    lora_a = _dot(zwa, w2_ref[...])
    w_raw = -_softplus(-(w0_ref[...] + lora_w)) - 0.5
    logw = -jnp.exp(w_raw)
    alr = _sigmoid(a0_ref[...] + lora_a)

    e_ones = e_ref[...]

    def head_sum(x):
        return jnp.concatenate(
            [jnp.dot(x[:, g * GROUP_W:(g + 1) * GROUP_W].astype(BF16), e_ones,
                     preferred_element_type=F32) for g in range(N_GROUPS)], axis=1)

    kk = k * kk_ref[...]
    kk = kk / jnp.maximum(jnp.sqrt(head_sum(kk * kk)), 1e-12)
    k2 = k * (1.0 + (alr - 1.0) * ka_ref[...])
    a_vec = -kk
    b_vec = kk * alr
    if t_valid < C:
        valid = tok < t_valid
        logw = jnp.where(valid, logw, 0.0)
        b_vec = jnp.where(valid, b_vec, 0.0)
        k2 = jnp.where(valid, k2, 0.0)

    ti = lax.broadcasted_iota(jnp.int32, (NS * C, NS * C), 0)
    si = lax.broadcasted_iota(jnp.int32, (NS * C, NS * C), 1)
    ltri = jnp.where(((ti // C) == (si // C)) & (ti >= si), 1.0, 0.0).astype(BF16)
    cum = _dot_exact_lhs(ltri, logw)
    ew_incl = jnp.exp(cum)
    ew_excl = jnp.exp(cum - logw)
    e_inv = jnp.exp(-cum)
    a_t = a_vec * ew_excl
    r_t = r * ew_incl
    b_t = b_vec * e_inv
    k_t = k2 * e_inv

    ri = lax.broadcasted_iota(jnp.int32, (R, GROUP_W), 0)
    li = lax.broadcasted_iota(jnp.int32, (R, GROUP_W), 1)
    one_zero = lambda m: jnp.where(m, 1.0, 0.0).astype(BF16)
    head_mask = one_zero((ri // C) == (li // HEAD))
    wi = lax.broadcasted_iota(jnp.int32, (R, R), 0)
    wj = lax.broadcasted_iota(jnp.int32, (R, R), 1)
    wide_mask = one_zero((wi // C) == (wj // C))
    tt = lax.broadcasted_iota(jnp.int32, (C, R), 0)
    ss = lax.broadcasted_iota(jnp.int32, (C, R), 1) % C
    strict = tt > ss
    incl = tt >= ss
    eye_w = jnp.where(tt == ss, 1.0, 0.0)
    di = lax.broadcasted_iota(jnp.int32, (GROUP_W, GROUP_W), 0)
    dj = lax.broadcasted_iota(jnp.int32, (GROUP_W, GROUP_W), 1)
    state_mask = (di // HEAD) == (dj // HEAD)

    def bd(x, mask):
        if C % 16 == 0:
            return jnp.concatenate([x.astype(BF16)] * GROUP_HEADS, axis=0) * mask
        return (jnp.concatenate([x] * GROUP_HEADS, axis=0) * mask.astype(F32)).astype(BF16)

    def dot_cat(lhs, rhs):
        if all(a.shape[1] % LANES == 0 for a in lhs):
            return _dot(jnp.concatenate(lhs, axis=1), jnp.concatenate(rhs, axis=0))
        return sum(_dot(a, b) for a, b in zip(lhs, rhs))

    n_levels = int(math.log2(C))

    ops = {}
    for b in range(NB):
        for ch in range(CH):
            for g in range(N_GROUPS):
                r0 = (b * CH + ch) * C
                rs = slice(r0, r0 + C)
                ls = slice(g * GROUP_W, (g + 1) * GROUP_W)
                g_c = ew_incl[r0 + C - 1:r0 + C, ls]
                ops[b, ch, g] = dict(
                    a=a_t[rs, ls].astype(BF16), r=r_t[rs, ls].astype(BF16),
                    bk_hat=jnp.concatenate([b_t[rs, ls] * g_c, k_t[rs, ls] * g_c], axis=0),
                    b_bd=bd(b_t[rs, ls], head_mask), k_bd=bd(k_t[rs, ls], head_mask),
                    v=v[rs, ls], v_bd=bd(v[rs, ls], head_mask), g_c=g_c)
    for o in ops.values():
        ar = jnp.concatenate([o["a"], o["r"]], axis=0)
        nb = _dot_nt(ar, o["b_bd"])
        nk = _dot_nt(ar, o["k_bd"])
        o["n_ab"] = jnp.where(strict, nb[0:C], 0.0)
        o["m_rb"] = jnp.where(incl, nb[C:2 * C], 0.0)
        o["n_ak"] = jnp.where(strict, nk[0:C], 0.0)
        o["m_rk"] = jnp.where(incl, nk[C:2 * C], 0.0)
    for o in ops.values():
        o["t_inv"] = eye_w + o["n_ab"]
        o["p"] = _dot(o["n_ab"], bd(o["n_ab"], wide_mask))
    for j in range(1, n_levels - 1):
        for o in ops.values():
            both = _dot(jnp.concatenate([o["p"], o["t_inv"]], axis=0), bd(o["p"], wide_mask))
            o["p"] = both[0:C]
            o["t_inv"] = o["t_inv"] + both[C:2 * C]
    for o in ops.values():
        o["t_inv"] = o["t_inv"] + _dot(o["t_inv"], bd(o["p"], wide_mask))

    for ch in range(CH):
        now = [(b, g, ops[b, ch, g]) for b in range(NB) for g in range(N_GROUPS)]
        for b, g, o in now:
            o["s_prev"] = s_scr[b, g]
            o["s_bf"] = o["s_prev"].astype(BF16)
            o["rhs"] = dot_cat([o["a"], o["n_ak"]], [o["s_bf"], o["v_bd"]])
        for b, g, o in now:
            o["u"] = _dot(o["t_inv"], bd(o["rhs"], head_mask))
        for b, g, o in now:
            o["y"] = dot_cat([o["r"], o["m_rb"], o["m_rk"]],
                             [o["s_bf"], bd(o["u"], head_mask), o["v_bd"]])
        for b, g, o in now:
            g_col = jnp.transpose(jnp.broadcast_to(o["g_c"], (GROUP_W, GROUP_W)))
            upd = _dot_tn(o["bk_hat"], jnp.concatenate([o["u"], o["v"]], axis=0))
            s_scr[b, g] = o["s_prev"] * g_col + jnp.where(state_mask, upd, 0.0)

    y = jnp.concatenate(
        [jnp.concatenate([ops[b, ch, g]["y"] for g in range(N_GROUPS)], axis=-1)
         for b in range(NB) for ch in range(CH)], axis=0)
    inv_n = 1.0 / HEAD
    mean = head_sum(y) * inv_n
    d = y - mean
    var = head_sum(d * d) * inv_n
    yn = d * lax.rsqrt(var + GN_EPS) * lnw_ref[...] + lnb_ref[...]
    bonus = head_sum(r * k2 * rk_ref[...]) * v
    gate = gr_ref[...].reshape(NS * C, RWKV_WIDTH)
    out = (yn + bonus) * (gate * _sigmoid(gate))
    y_ref[...] = out.reshape(NB, CH * C, RWKV_WIDTH).astype(y_ref.dtype)

    @pl.when(c_idx == last)
    def _():
        for b in range(NB):
            shift_ref[b] = last_rows[b]
            for g in range(N_GROUPS):
                s_bd = s_scr[b, g]
                s_t = s_bd[0:HEAD]
                for h in range(1, GROUP_HEADS):
                    s_t = s_t + s_bd[h * HEAD:(h + 1) * HEAD]
                heads = exact_nt(eye_g, s_t)
                sout_ref[b, g * GROUP_HEADS:(g + 1) * GROUP_HEADS] = heads.reshape(
                    GROUP_HEADS, HEAD, HEAD)


def _rwkv(zsh, gr, shift_prev, s0, p, chunk, t_valid, seqs_per_step, chunks_per_step):
    b, t, _ = zsh.shape
    step_tokens = chunk * chunks_per_step
    nc = t // step_tokens
    nb = seqs_per_step
    tok = lambda width: pl.BlockSpec((nb, step_tokens, width), lambda i, c: (i, c, 0))
    per_b = lambda shape: pl.BlockSpec((nb,) + shape, lambda i, c: (i,) + (0,) * len(shape))
    const = lambda a: pl.BlockSpec(a.shape, lambda i, c: (0,) * a.ndim)
    params = (p["mu"], p["w0"], p["w1"], p["a0"], p["w2"], p["k_k"], p["k_a"], p["r_k"],
              p["lnx_w"], p["lnx_b"], p["e_ones"])
    return pl.pallas_call(
        functools.partial(_rwkv_kernel, chunk=chunk, t_valid=t_valid),
        grid=(b // nb, nc),
        in_specs=[tok(SHIFT_WIDTH), tok(512), per_b((1, SHIFT_WIDTH)),
                  per_b((RWKV_HEADS, HEAD, HEAD))] + [const(a) for a in params],
        out_specs=(tok(512), per_b((RWKV_HEADS, HEAD, HEAD)), per_b((1, SHIFT_WIDTH))),
        out_shape=(jax.ShapeDtypeStruct((b, t, 512), BF16),
                   jax.ShapeDtypeStruct((b, RWKV_HEADS, HEAD, HEAD), F32),
                   jax.ShapeDtypeStruct((b, 1, SHIFT_WIDTH), F32)),
        scratch_shapes=[pltpu.VMEM((nb, N_GROUPS, GROUP_W, GROUP_W), F32),
                        pltpu.VMEM((nb, 1, SHIFT_WIDTH), F32)],
        compiler_params=pltpu.CompilerParams(dimension_semantics=("arbitrary", "arbitrary"),
                                             vmem_limit_bytes=VMEM_LIMIT_V7X),
        name="rwkv7",
    )(zsh, gr, shift_prev, s0, *params)


def _lambda(lq1_ref, lk1_ref, lq2_ref, lk2_ref, lam_init):
    s1 = jnp.sum(lq1_ref[...] * lk1_ref[...], axis=-1, keepdims=True)
    s2 = jnp.sum(lq2_ref[...] * lk2_ref[...], axis=-1, keepdims=True)
    return jnp.exp(s1) - jnp.exp(s2) + lam_init


def _diff_epilogue(acc0, l0, acc1, l1, lam, lam_init, subln, gate):
    o = acc0 / l0 - lam * (acc1 / l1)
    ms = jnp.mean(o * o, axis=-1, keepdims=True)
    on = o * lax.rsqrt(ms + SUBLN_EPS) * subln * (1.0 - lam_init)
    return on * (gate * _sigmoid(gate))


def _attn_kernel(rb_ref, li_ref, qt_ref, k_ref, vt_ref, gd_ref, lq1_ref, lk1_ref, lq2_ref,
                 lk2_ref, sub_ref, o_ref, bias_scr, s_scr, acc_scr, *, tq, tk):
    h = pl.program_id(1)
    qi = pl.program_id(2)
    nsub = tq // tk
    first_near = nsub * qi - 1

    @pl.when(qi == 0)
    def _():
        key = lax.broadcasted_iota(jnp.int32, (tk, tq), 0)
        qry = lax.broadcasted_iota(jnp.int32, (tk, tq), 1)
        for i in range(nsub + 1):
            dist = qry - key - (i - 1) * tk
            bias = _rel_bias_minus_far(dist, rb_ref, h) * LOG2E
            bias = jnp.where(dist >= 0, bias, NEG_INF)
            bias_scr[i] = jnp.concatenate([bias, bias], axis=1)

    qt = qt_ref[...]
    feat = lax.broadcasted_iota(jnp.int32, (2 * HEAD, tq), 0)
    qt2 = jnp.concatenate([jnp.where(feat < HEAD, qt, jnp.zeros_like(qt)),
                           jnp.where(feat >= HEAD, qt, jnp.zeros_like(qt))], axis=1)

    acc_scr[...] = jnp.zeros(acc_scr.shape, F32)
    ones_rows = jnp.ones((ONES_ROWS, tk), BF16)

    def logits(kj):
        off = pl.multiple_of(kj * tk, tk)
        return jnp.dot(k_ref[pl.ds(off, tk), :], qt2, preferred_element_type=F32)

    def absorb(s, kj, m_old):
        off = pl.multiple_of(kj * tk, tk)
        vt_ext = jnp.concatenate([vt_ref[:, pl.ds(off, tk)], ones_rows], axis=0)
        m_new = jnp.maximum(m_old, jnp.max(s, axis=0, keepdims=True))
        pm = jnp.exp2(s - m_new).astype(BF16)
        acc_scr[...] = (jnp.exp2(m_old - m_new) * acc_scr[...]
                        + jnp.dot(vt_ext, pm, preferred_element_type=F32))
        return m_new

    assert nsub == 2
    m_run = jnp.full((1, 2 * tq), NEG_INF, F32)

    @pl.when(qi >= 1)
    def _():
        s_scr[0] = logits(0)

    def far_pair(i, m):
        s_scr[1] = logits(2 * i + 1)
        m = absorb(s_scr[0], 2 * i, m)
        s_scr[0] = logits(2 * i + 2)
        return absorb(s_scr[1], 2 * i + 1, m)

    m_run = lax.fori_loop(0, qi - 1, far_pair, m_run)

    def last_pair(i, m):
        s_scr[1] = logits(first_near)
        m = absorb(s_scr[0], first_near - 1, m)
        return absorb(s_scr[1] + bias_scr[0], first_near, m)

    m_run = lax.fori_loop(0, jnp.minimum(qi, 1), last_pair, m_run)
    s_diag = [logits(first_near + 1 + i) for i in range(nsub)]
    for i in range(nsub):
        m_run = absorb(s_diag[i] + bias_scr[1 + i], first_near + 1 + i, m_run)

    lam_init = li_ref[0]
    lam = _lambda(lq1_ref, lk1_ref, lq2_ref, lk2_ref, lam_init)
    l_fin = acc_scr[DIFF_V:DIFF_V + 1, :]
    o_t = (acc_scr[0:DIFF_V, 0:tq] / l_fin[:, 0:tq]
           - lam * (acc_scr[0:DIFF_V, tq:2 * tq] / l_fin[:, tq:2 * tq]))
    ms = jnp.mean(o_t * o_t, axis=0, keepdims=True)
    o = jnp.transpose(o_t * lax.rsqrt(ms + SUBLN_EPS))
    gate = gd_ref[...]
    out = o * sub_ref[...] * (1.0 - lam_init) * (gate * _sigmoid(gate))
    o_ref[...] = out.astype(o_ref.dtype)


def _attn_prompt(qt, kb, vt, gd, rel_bias, lam_init, lam_vecs, subln, batch, tq, tk):
    m = kb.shape[0]
    t = m // batch
    nq = t // tq
    assert tk >= MAX_DISTANCE and tq % tk == 0
    tile = pl.BlockSpec((tq, DIFF_V), lambda b, h, i: (b * nq + i, h))
    tile_t = pl.BlockSpec((DIFF_V, tq), lambda b, h, i: (h, b * nq + i))
    seq = pl.BlockSpec((t, DIFF_V), lambda b, h, i: (b, h))
    seq_t = pl.BlockSpec((DIFF_V, t), lambda b, h, i: (h, b))
    vec = lambda a: pl.BlockSpec(a.shape, lambda b, h, i: (0,) * a.ndim)
    smem = pl.BlockSpec(memory_space=pltpu.SMEM)
    return pl.pallas_call(
        functools.partial(_attn_kernel, tq=tq, tk=tk),
        grid=(batch, DIFF_HEADS, nq),
        in_specs=([smem, smem, tile_t, seq, seq_t, tile] + [vec(a) for a in lam_vecs]
                  + [vec(subln)]),
        out_specs=tile,
        out_shape=jax.ShapeDtypeStruct((m, DIFF_WIDTH), BF16),
        scratch_shapes=[pltpu.VMEM((tq // tk + 1, tk, 2 * tq), F32),
                        pltpu.VMEM((2, tk, 2 * tq), F32),
                        pltpu.VMEM((DIFF_V + ONES_ROWS, 2 * tq), F32)],
        compiler_params=pltpu.CompilerParams(
            dimension_semantics=("arbitrary", "arbitrary", "arbitrary"),
            vmem_limit_bytes=VMEM_LIMIT_V7X),
        name="diff_attn_prompt",
    )(rel_bias, lam_init, qt, kb, vt, gd, *lam_vecs, subln)


TOKENS_PADDED = 8
SCORE_ROWS = DIFF_HEADS * 2 * TOKENS_PADDED
PAGE_ROWS = PAGE * DIFF_HEADS
PAGES_PER_SOFTMAX_STEP = 2


def _decode_kernel(pt_ref, rb_ref, li_ref, q_ref, kn_ref, vn_ref, gd_ref, lq1_ref, lk1_ref,
                   lq2_ref, lk2_ref, sub_ref, *rest, pages_per_step):
    P = pages_per_step
    k_refs = rest[0:P]
    v_refs = rest[P:2 * P]
    o_ref = rest[2 * P]
    m_scr, acc_scr = rest[2 * P + 1:]
    del pt_ref
    j = pl.program_id(1)
    last = pl.num_programs(1) - 1
    TP = TOKENS_PADDED
    NR = SCORE_ROWS

    @pl.when(j == 0)
    def _():
        m_scr[...] = jnp.full(m_scr.shape, NEG_INF, F32)
        acc_scr[...] = jnp.zeros(acc_scr.shape, F32)

    q = q_ref[...]
    lane = lax.broadcasted_iota(jnp.int32, (TP, DIFF_V), 1)
    q_rows = jnp.concatenate(
        [jnp.where((lane // HEAD) == c, q[:, h * DIFF_V:(h + 1) * DIFF_V], 0.0)
         for h in range(DIFF_HEADS) for c in range(2)], axis=0)
    q_rows = q_rows.astype(BF16).astype(F32)
    row = lax.broadcasted_iota(jnp.int32, (NR, 1), 0)
    row_head = row // (2 * TP)
    row_tok = row % TP

    def scores(k_rows, bias=None, causal=False):
        n = k_rows.shape[0]
        s = lax.dot_general(q_rows, k_rows, (((1,), (1,)), ((), ())),
                            preferred_element_type=F32)
        col = lax.broadcasted_iota(jnp.int32, (1, n), 1)
        keep = (col % DIFF_HEADS) == row_head
        if causal:
            keep = keep & (row_tok >= col // DIFF_HEADS)
        if bias is not None:
            s = s + bias
        return jnp.where(keep, s, NEG_INF)

    def row_bias(n, dist_of):
        col_tok = lax.broadcasted_iota(jnp.int32, (1, n), 1) // DIFF_HEADS
        t_row = lax.broadcasted_iota(jnp.int32, (2 * TP, 1), 0) % TP
        dist = dist_of(t_row, col_tok)
        return jnp.concatenate([_rel_bias_minus_far(dist, rb_ref, h) for h in range(DIFF_HEADS)],
                               axis=0) * LOG2E

    def update(s_list, v_list):
        m_old = m_scr[...]
        m_new = m_old
        for s in s_list:
            m_new = jnp.maximum(m_new, jnp.max(s, axis=-1, keepdims=True))
        acc = jnp.exp2(m_old - m_new) * acc_scr[...]
        for s, v in zip(s_list, v_list):
            v_ext = jnp.concatenate([v, jnp.ones(v.shape, F32)], axis=-1)
            acc = acc + jnp.dot(jnp.exp2(s - m_new), v_ext, preferred_element_type=F32)
        acc_scr[...] = acc
        m_scr[...] = m_new

    page_k = lambda i: k_refs[i][...]
    page_v = lambda i: v_refs[i][...]
    G = PAGES_PER_SOFTMAX_STEP

    @pl.when(j != last)
    def _():
        for i0 in range(0, P, G):
            update([scores(page_k(i)) for i in range(i0, i0 + G)],
                   [page_v(i) for i in range(i0, i0 + G)])

    @pl.when(j == last)
    def _():
        bias_p = row_bias(PAGE_ROWS, lambda t_row, col_tok: (PAGE + t_row) - col_tok)
        n_new = TP * DIFF_HEADS
        bias_n = row_bias(n_new, lambda t_row, col_tok: t_row - col_tok)
        for i0 in range(0, P - G, G):
            update([scores(page_k(i)) for i in range(i0, i0 + G)],
                   [page_v(i) for i in range(i0, i0 + G)])
        s_list = [scores(page_k(i)) for i in range(P - G, P - 1)]
        s_list.append(scores(page_k(P - 1), bias=bias_p))
        s_list.append(scores(kn_ref[...], bias=bias_n, causal=True))
        update(s_list, [page_v(i) for i in range(P - G, P)] + [vn_ref[...]])

        lam_init = li_ref[0]
        lam = _lambda(lq1_ref, lk1_ref, lq2_ref, lk2_ref, lam_init)
        gate = gd_ref[...]
        outs = []
        for h in range(DIFF_HEADS):
            r0 = h * 2 * TP
            r1 = r0 + TP
            outs.append(_diff_epilogue(
                acc_scr[r0:r0 + TP, 0:DIFF_V], acc_scr[r0:r0 + TP, DIFF_V:DIFF_V + 1],
                acc_scr[r1:r1 + TP, 0:DIFF_V], acc_scr[r1:r1 + TP, DIFF_V:DIFF_V + 1],
                lam, lam_init, sub_ref[...], gate[:, h * DIFF_V:(h + 1) * DIFF_V]))
        o_ref[...] = jnp.concatenate(outs, axis=-1).astype(o_ref.dtype)


def _attn_sample(q, kn, vn, gd, cache_k, cache_v, page_ids, rel_bias, lam_init, lam_vecs, subln,
                 pages_per_step):
    b = q.shape[0]
    n_pages = page_ids.shape[0] // b
    P = pages_per_step
    per_b = lambda shape: pl.BlockSpec((None,) + shape, lambda i, j, pt: (i, 0, 0))
    vec = lambda a: pl.BlockSpec(a.shape, lambda i, j, pt: (0,) * a.ndim)
    smem = pl.BlockSpec(memory_space=pltpu.SMEM)
    tok = per_b((TOKENS_PADDED, DIFF_WIDTH))
    new = per_b((TOKENS_PADDED * DIFF_HEADS, DIFF_V))

    def page_spec(slot):
        return pl.BlockSpec((None, PAGE_ROWS, DIFF_V),
                            lambda i, j, pt: (pt[i * n_pages + j * P + slot], 0, 0))

    grid_spec = pltpu.PrefetchScalarGridSpec(
        num_scalar_prefetch=1,
        grid=(b, n_pages // P),
        in_specs=([smem, smem, tok, new, new, tok] + [vec(a) for a in lam_vecs]
                  + [vec(subln)] + [page_spec(s) for s in range(P)]
                  + [page_spec(s) for s in range(P)]),
        out_specs=tok,
        scratch_shapes=[pltpu.VMEM((SCORE_ROWS, 1), F32),
                        pltpu.VMEM((SCORE_ROWS, 2 * DIFF_V), F32)],
    )
    return pl.pallas_call(
        functools.partial(_decode_kernel, pages_per_step=P),
        grid_spec=grid_spec,
        out_shape=jax.ShapeDtypeStruct((b, TOKENS_PADDED, DIFF_WIDTH), BF16),
        compiler_params=pltpu.CompilerParams(dimension_semantics=("arbitrary", "arbitrary"),
                                             vmem_limit_bytes=VMEM_LIMIT_V7X),
        name="diff_attn_sample",
    )(page_ids, rel_bias, lam_init, q, kn, vn, gd, *lam_vecs, subln,
      *([cache_k] * P), *([cache_v] * P))


def _layer_params(l, pre_norm, post_norm, w_in, mu_shift, w0, w_up, a0, a_up, k_k, k_a, r_k,
                  lnx_w, lnx_b, lam_q1, lam_k1, lam_q2, lam_k2, subln, w_out):
    row = lambda a: a.reshape(1, -1)
    zeros = jnp.zeros((LORA, RWKV_WIDTH), F32)
    head_of = jnp.arange(GROUP_W) // HEAD
    return dict(
        pre_norm=row(pre_norm[l]), post_norm=row(post_norm[l]),
        w_in=w_in[l].astype(BF16), w_out=w_out[l].astype(BF16),
        mu=row(mu_shift[l]), w0=row(w0[l]), a0=row(a0[l]),
        w1=jnp.concatenate([w_up[l], zeros], axis=0).astype(BF16),
        w2=jnp.concatenate([zeros, a_up[l]], axis=0).astype(BF16),
        k_k=row(k_k[l]), k_a=row(k_a[l]), r_k=row(r_k[l]),
        lnx_w=row(lnx_w[l]), lnx_b=row(lnx_b[l]),
        e_ones=(head_of[:, None] == head_of[None, :]).astype(BF16),
        lam_vecs=(row(lam_q1[l]), row(lam_k1[l]), row(lam_q2[l]), row(lam_k2[l])),
        subln=row(subln[l]),
        lam_init=jnp.full((1,), 0.8 - 0.6 * math.exp(-0.3 * l), F32),
    )


def _mixer(x2d, batch, p, shift_prev, s0, project, attend, tm_out, chunk, t_valid,
           seqs_per_step, chunks_per_step=1):
    m = x2d.shape[0]
    t = m // batch
    zsh, gr, gd, attn_operands = project(x2d)
    y_r, s_bd, shift = _rwkv(zsh.reshape(batch, t, SHIFT_WIDTH), gr.reshape(batch, t, 512),
                             shift_prev, s0, p, chunk, t_valid, seqs_per_step, chunks_per_step)
    y_d = attend(gd, *attn_operands)
    x_new = _outproj(y_r.reshape(m, RWKV_WIDTH), y_d.reshape(m, DIFF_WIDTH), p["w_out"],
                     p["post_norm"], x2d, tm_out)
    return x_new, shift, s_bd, attn_operands


def kernel(x_prompt, x_sample, cache_k, cache_v, state_wkv, state_shift, page_table, rel_bias,
           pre_norm, post_norm, w_in, mu_shift, w0, w_up, a0, a_up, k_k, k_a, r_k, lnx_w, lnx_b,
           lam_q1, lam_k1, lam_q2, lam_k2, subln, w_out):
    return _forward(x_prompt, x_sample, cache_k, cache_v, state_wkv, state_shift, page_table,
                    rel_bias, pre_norm, post_norm, w_in, mu_shift, w0, w_up, a0, a_up, k_k, k_a,
                    r_k, lnx_w, lnx_b, lam_q1, lam_k1, lam_q2, lam_k2, subln, w_out,
                    PROMPT_TM=256, PROMPT_TM_OUT=512, PROMPT_CHUNK=64, PROMPT_TQ=512,
                    PROMPT_TK=256, PAGES_PER_STEP=16, SAMPLE_SEQS_PER_STEP=4,
                    PROMPT_CHUNKS_PER_STEP=2)


def _forward(x_prompt, x_sample, cache_k, cache_v, state_wkv, state_shift, page_table, rel_bias,
             pre_norm, post_norm, w_in, mu_shift, w0, w_up, a0, a_up, k_k, k_a, r_k, lnx_w, lnx_b,
             lam_q1, lam_k1, lam_q2, lam_k2, subln, w_out, *, PROMPT_TM, PROMPT_TM_OUT,
             PROMPT_CHUNK, PROMPT_TQ, PROMPT_TK, PAGES_PER_STEP, SAMPLE_SEQS_PER_STEP,
             PROMPT_CHUNKS_PER_STEP):
    bp, tp, d = x_prompt.shape
    bs, ts, _ = x_sample.shape
    depth = w_in.shape[0]
    n_pool = cache_k.shape[1]
    n_pages = page_table.shape[1]
    TS_PAD = TOKENS_PADDED

    xp = x_prompt.reshape(bp * tp, d)
    xs = jnp.pad(x_sample, ((0, 0), (0, TS_PAD - ts), (0, 0))).reshape(bs * TS_PAD, d)
    ck = cache_k.reshape(depth * n_pool, PAGE_ROWS, DIFF_V)
    cv = cache_v.reshape(depth * n_pool, PAGE_ROWS, DIFF_V)
    zero_shift = jnp.zeros((bp, 1, SHIFT_WIDTH), F32)
    zero_state = jnp.zeros((bp, RWKV_HEADS, HEAD, HEAD), F32)

    outs = {name: [] for name in ("wp", "sp", "ks", "vs", "ws", "ss")}
    kv_rows = None
    for l in range(depth):
        p = _layer_params(l, pre_norm, post_norm, w_in, mu_shift, w0, w_up, a0, a_up, k_k, k_a,
                          r_k, lnx_w, lnx_b, lam_q1, lam_k1, lam_q2, lam_k2, subln, w_out)

        def project_prompt(x2d):
            zsh, gr, qt, k_rows, v_rows, gd, kb, vt = _inproj_prompt(
                x2d, p["pre_norm"], p["w_in"], PROMPT_TM, l, depth, kv_rows)
            return zsh, gr, gd, (qt, kb, vt, k_rows, v_rows)

        def attend_prompt(gd, qt, kb, vt, k_rows, v_rows):
            return _attn_prompt(qt, kb, vt, gd, rel_bias, p["lam_init"], p["lam_vecs"],
                                p["subln"], bp, PROMPT_TQ, PROMPT_TK)

        def project_sample(x2d):
            zsh, gr, q, k, v, gd = _inproj_sample(x2d, p["pre_norm"], p["w_in"], bs * TS_PAD)
            return zsh, gr, gd, (q, k, v)

        page_ids = (page_table + l * n_pool).reshape(-1).astype(jnp.int32)

        def attend_sample(gd, q, k, v):
            r3 = lambda a: a.reshape(bs, TS_PAD, DIFF_WIDTH)
            rows = lambda a: a.reshape(bs, TS_PAD * DIFF_HEADS, DIFF_V)
            return _attn_sample(r3(q), rows(k), rows(v), r3(gd), ck, cv, page_ids, rel_bias,
                                p["lam_init"], p["lam_vecs"], p["subln"], PAGES_PER_STEP)

        xp, sp, wp, prompt_ops = _mixer(xp, bp, p, zero_shift, zero_state, project_prompt,
                                        attend_prompt, PROMPT_TM_OUT, PROMPT_CHUNK, PROMPT_CHUNK,
                                        bp, PROMPT_CHUNKS_PER_STEP)
        kv_rows = prompt_ops[3:5]
        xs, ss, ws, (_, k_s, v_s) = _mixer(xs, bs, p, state_shift[l][:, None, :],
                                           state_wkv[l], project_sample,
                                           attend_sample, bs * TS_PAD, TS_PAD, ts,
                                           SAMPLE_SEQS_PER_STEP)
        outs["wp"].append(wp)
        outs["sp"].append(sp.reshape(bp, SHIFT_WIDTH))
        outs["ks"].append(k_s.reshape(bs, TS_PAD, DIFF_HEADS, DIFF_V)[:, :ts])
        outs["vs"].append(v_s.reshape(bs, TS_PAD, DIFF_HEADS, DIFF_V)[:, :ts])
        outs["ws"].append(ws)
        outs["ss"].append(ss.reshape(bs, SHIFT_WIDTH))

    st = lambda name: jnp.stack(outs[name])
    kv_shape = (depth, bp, tp, DIFF_HEADS, DIFF_V)
    return (xp.reshape(bp, tp, d), xs.reshape(bs, TS_PAD, d)[:, :ts], kv_rows[0].reshape(kv_shape),
            kv_rows[1].reshape(kv_shape), st("wp"), st("sp"), st("ks"), st("vs"), st("ws"),
            st("ss"))
```

```python
import functools
import math

import jax
import jax.numpy as jnp
from jax import lax
from jax.experimental import pallas as pl
from jax.experimental.pallas import tpu as pltpu

F32 = jnp.float32
BF16 = jnp.bfloat16

RWKV_WIDTH = 512
DIFF_WIDTH = 512
HEAD = 64
RWKV_HEADS = RWKV_WIDTH // HEAD
LORA = 64
DIFF_HEADS = 4
DIFF_V = 128
NUM_BUCKETS = 32
MAX_EXACT = NUM_BUCKETS // 2
MAX_DISTANCE = 128
PAGE = 128
RMS_EPS = 1e-6
SUBLN_EPS = 1e-5
GN_EPS = 64e-5
NEG_INF = -1e30
LOG2E = 1.4426950408889634
SHIFT_WIDTH = 3 * RWKV_WIDTH + 2 * LORA
SEG_SHIFT = (0, SHIFT_WIDTH)
SEG_GR = (SHIFT_WIDTH, SHIFT_WIDTH + 512)
SEG_Q = (SEG_GR[1], SEG_GR[1] + 512)
SEG_K = (SEG_Q[1], SEG_Q[1] + 512)
SEG_V = (SEG_K[1], SEG_K[1] + 512)
SEG_GD = (SEG_V[1], SEG_V[1] + 512)

VMEM_LIMIT_V7X = 48 * 1024 * 1024
LANES = 128
ONES_ROWS = 16
GROUP_HEADS = 4
GROUP_W = GROUP_HEADS * HEAD
N_GROUPS = RWKV_HEADS // GROUP_HEADS


def _dot(a, b):
    return jnp.dot(a.astype(BF16), b.astype(BF16), preferred_element_type=F32)


def _dot_nt(a, b):
    return lax.dot_general(a.astype(BF16), b.astype(BF16), (((1,), (1,)), ((), ())),
                           preferred_element_type=F32)


def _dot_tn(a, b):
    return lax.dot_general(a.astype(BF16), b.astype(BF16), (((0,), (0,)), ((), ())),
                           preferred_element_type=F32)


def _split(x):
    hi = x.astype(BF16)
    lo = (x - hi.astype(F32)).astype(BF16)
    return hi, lo


def _dot_exact_lhs(a_exact, b):
    hi, lo = _split(b)
    return (jnp.dot(a_exact, hi, preferred_element_type=F32)
            + jnp.dot(a_exact, lo, preferred_element_type=F32))


def _sigmoid(x):
    return 1.0 / (1.0 + jnp.exp(-x))


def _rel_bias_minus_far(dist, rb_ref, head):
    n = jnp.maximum(dist, 0)
    nf = jnp.maximum(n, 1).astype(F32)
    large = MAX_EXACT + (jnp.log(nf / MAX_EXACT) / math.log(MAX_DISTANCE / MAX_EXACT)
                         * (NUM_BUCKETS - MAX_EXACT)).astype(jnp.int32)
    large = jnp.minimum(large, NUM_BUCKETS - 1)
    bucket = jnp.where(n < MAX_EXACT, n, large)
    far = rb_ref[NUM_BUCKETS - 1, head]
    out = jnp.zeros(dist.shape, F32)
    for b in range(NUM_BUCKETS - 1):
        out = jnp.where(bucket == b, rb_ref[b, head] - far, out)
    return out


def _projected_segments(x_ref, g_ref, w_ref):
    x = x_ref[...]
    ms = jnp.mean(x * x, axis=-1, keepdims=True)
    h = (x * lax.rsqrt(ms + RMS_EPS) * g_ref[...]).astype(BF16)

    def seg(lo_hi):
        lo, hi = lo_hi
        return jnp.dot(h, w_ref[:, lo:hi], preferred_element_type=F32)

    return seg


Q_SCALE = LOG2E * HEAD ** -0.5


def _inproj_sample_kernel(x_ref, g_ref, w_ref, zsh_ref, gr_ref, q_ref, k_ref, v_ref, gd_ref):
    seg = _projected_segments(x_ref, g_ref, w_ref)
    zsh_ref[...] = seg(SEG_SHIFT)
    gr_ref[...] = seg(SEG_GR)
    q_ref[...] = seg(SEG_Q) * Q_SCALE
    k_ref[...] = seg(SEG_K)
    v_ref[...] = seg(SEG_V)
    gd_ref[...] = seg(SEG_GD)


def _inproj_prompt_kernel(x_ref, g_ref, w_ref, *refs):
    zsh_ref, gr_ref, qt_ref, krows_ref, vrows_ref, gd_ref, kb_ref, vt_ref = refs[-8:]
    seg = _projected_segments(x_ref, g_ref, w_ref)
    tm = x_ref.shape[0]
    zsh_ref[...] = seg(SEG_SHIFT)
    gr_ref[...] = seg(SEG_GR)
    qt_ref[...] = jnp.transpose(seg(SEG_Q) * Q_SCALE).astype(BF16)
    gd_ref[...] = seg(SEG_GD)
    k = seg(SEG_K)
    v = seg(SEG_V)
    kb_ref[...] = k.astype(BF16)
    vt_ref[...] = jnp.transpose(v).astype(BF16)
    for hd in range(DIFF_HEADS):
        cols = slice(hd * DIFF_V, (hd + 1) * DIFF_V)
        krows_ref[pl.ds(hd, tm, stride=DIFF_HEADS), :] = k[:, cols]
        vrows_ref[pl.ds(hd, tm, stride=DIFF_HEADS), :] = v[:, cols]


def _inproj_sample(x, g, w_bf16, tm):
    m, d = x.shape
    row = lambda width: pl.BlockSpec((tm, width), lambda i: (i, 0))
    f32 = jax.ShapeDtypeStruct((m, 512), F32)
    return pl.pallas_call(
        _inproj_sample_kernel,
        grid=(m // tm,),
        in_specs=[row(d), pl.BlockSpec((1, d), lambda i: (0, 0)),
                  pl.BlockSpec(w_bf16.shape, lambda i: (0, 0))],
        out_specs=(row(SHIFT_WIDTH), row(512), row(512), row(512), row(512), row(512)),
        out_shape=(jax.ShapeDtypeStruct((m, SHIFT_WIDTH), F32), f32, f32, f32, f32, f32),
        compiler_params=pltpu.CompilerParams(dimension_semantics=("arbitrary",),
                                             vmem_limit_bytes=VMEM_LIMIT_V7X),
        name="inproj_sample",
    )(x, g, w_bf16)


def _inproj_prompt(x, g, w_bf16, tm, layer, depth, kv_rows):
    m, d = x.shape
    steps = m // tm
    row = lambda width: pl.BlockSpec((tm, width), lambda i: (i, 0))
    col = pl.BlockSpec((512, tm), lambda i: (0, i))
    rows = pl.BlockSpec((tm * DIFF_HEADS, DIFF_V), lambda i: (layer * steps + i, 0))
    f32 = jax.ShapeDtypeStruct((m, 512), F32)
    bf_t = jax.ShapeDtypeStruct((512, m), BF16)
    rows_shape = jax.ShapeDtypeStruct((depth * m * DIFF_HEADS, DIFF_V), F32)
    in_specs = [row(d), pl.BlockSpec((1, d), lambda i: (0, 0)),
                pl.BlockSpec(w_bf16.shape, lambda i: (0, 0))]
    operands = [x, g, w_bf16]
    aliases = {}
    if kv_rows is not None:
        in_specs += [pl.BlockSpec(memory_space=pl.ANY)] * 2
        operands += list(kv_rows)
        aliases = {3: 3, 4: 4}
    return pl.pallas_call(
        _inproj_prompt_kernel,
        grid=(steps,),
        in_specs=in_specs,
        out_specs=(row(SHIFT_WIDTH), row(512), col, rows, rows, row(512), row(512), col),
        out_shape=(jax.ShapeDtypeStruct((m, SHIFT_WIDTH), F32), f32, bf_t, rows_shape, rows_shape,
                   f32, jax.ShapeDtypeStruct((m, 512), BF16), bf_t),
        input_output_aliases=aliases,
        compiler_params=pltpu.CompilerParams(dimension_semantics=("arbitrary",),
                                             vmem_limit_bytes=VMEM_LIMIT_V7X),
        name="inproj_prompt",
    )(*operands)


def _outproj_kernel(yr_ref, yd_ref, w_ref, g_ref, x_ref, o_ref):
    out = (jnp.dot(yr_ref[...], w_ref[0:RWKV_WIDTH, :], preferred_element_type=F32)
           + jnp.dot(yd_ref[...], w_ref[RWKV_WIDTH:, :], preferred_element_type=F32))
    ms = jnp.mean(out * out, axis=-1, keepdims=True)
    o_ref[...] = x_ref[...] + out * lax.rsqrt(ms + RMS_EPS) * g_ref[...]


def _outproj(yr, yd, w_bf16, g, x, tm):
    m, d = x.shape
    row = lambda width: pl.BlockSpec((tm, width), lambda i: (i, 0))
    return pl.pallas_call(
        _outproj_kernel,
        grid=(m // tm,),
        in_specs=[row(512), row(512), pl.BlockSpec(w_bf16.shape, lambda i: (0, 0)),
                  pl.BlockSpec((1, d), lambda i: (0, 0)), row(d)],
        out_specs=row(d),
        out_shape=jax.ShapeDtypeStruct((m, d), F32),
        compiler_params=pltpu.CompilerParams(dimension_semantics=("arbitrary",),
                                             vmem_limit_bytes=VMEM_LIMIT_V7X),
        name="outproj",
    )(yr, yd, w_bf16, g, x)


def _rwkv_kernel(zsh_ref, gr_ref, sprev_ref, s0_ref, mu_ref, w0_ref, w1_ref, a0_ref, w2_ref,
                 kk_ref, ka_ref, rk_ref, lnw_ref, lnb_ref, e_ref,
                 y_ref, sout_ref, shift_ref, s_scr, carry_scr, *, chunk, t_valid):
    C = chunk
    NB = zsh_ref.shape[0]
    R = GROUP_HEADS * C
    c_idx = pl.program_id(1)
    last = pl.num_programs(1) - 1

    ii = lax.broadcasted_iota(jnp.int32, (GROUP_W, GROUP_W), 0)
    jj = lax.broadcasted_iota(jnp.int32, (GROUP_W, GROUP_W), 1)
    eye_g = jnp.where(ii == jj, 1.0, 0.0).astype(BF16)
    block_g = (ii // HEAD) == (jj // HEAD)

    def exact_nt(lhs_exact, x):
        hi = x.astype(BF16)
        r1 = x - hi.astype(F32)
        mid = r1.astype(BF16)
        lo = (r1 - mid.astype(F32)).astype(BF16)
        return _dot_nt(lhs_exact, hi) + _dot_nt(lhs_exact, mid) + _dot_nt(lhs_exact, lo)

    @pl.when(c_idx == 0)
    def _():
        carry_scr[...] = sprev_ref[...]
        for b in range(NB):
            for g in range(N_GROUPS):
                heads = s0_ref[b, g * GROUP_HEADS:(g + 1) * GROUP_HEADS]
                s_t = exact_nt(eye_g[0:HEAD, 0:HEAD], heads.reshape(GROUP_W, HEAD))
                s_scr[b, g] = jnp.where(block_g, jnp.concatenate([s_t] * GROUP_HEADS, axis=0),
                                        0.0)

    CH = zsh_ref.shape[1] // C
    NS = NB * CH
    assert CH == 1 or t_valid == C
    z = zsh_ref[...].reshape(NS * C, SHIFT_WIDTH)
    row = lax.broadcasted_iota(jnp.int32, (NS * C, 1), 0)
    tok = row % C
    prev = pltpu.roll(z, 1, axis=0)
    last_rows = []
    for b in range(NB):
        prev = jnp.where(row == b * CH * C, carry_scr[b], prev)
        last_tok = b * CH * C + (CH - 1) * C + t_valid - 1
        last_rows.append(z[last_tok:last_tok + 1, :])
        carry_scr[b] = last_rows[b]
    zm = z + mu_ref[...] * (prev - z)

    r = zm[:, 0:512]
    k = zm[:, 512:1024]
    v = zm[:, 1024:1536]
    zwa = zm[:, 1536:1664]
    lora_w = _dot(jnp.tanh(zwa), w1_ref[...])
    lora_a = _dot(zwa, w2_ref[...])
    logw = -math.exp(-0.5) * _sigmoid(w0_ref[...] + lora_w)
    alr = _sigmoid(a0_ref[...] + lora_a)

    e_ones = e_ref[...]

    def head_sum(x):
        return jnp.concatenate(
            [jnp.dot(x[:, g * GROUP_W:(g + 1) * GROUP_W].astype(BF16), e_ones,
                     preferred_element_type=F32) for g in range(N_GROUPS)], axis=1)

    kk = k * kk_ref[...]
    kk = kk * lax.rsqrt(jnp.maximum(head_sum(kk * kk), 1e-24))
    k2 = k * (1.0 + (alr - 1.0) * ka_ref[...])
    a_vec = -kk
    b_vec = kk * alr
    if t_valid < C:
        valid = tok < t_valid
        logw = jnp.where(valid, logw, 0.0)
        b_vec = jnp.where(valid, b_vec, 0.0)
        k2 = jnp.where(valid, k2, 0.0)

    ti = lax.broadcasted_iota(jnp.int32, (NS * C, NS * C), 0)
    si = lax.broadcasted_iota(jnp.int32, (NS * C, NS * C), 1)
    ltri = jnp.where(((ti // C) == (si // C)) & (ti >= si), 1.0, 0.0).astype(BF16)
    cum = _dot_exact_lhs(ltri, logw)
    ew_incl = jnp.exp(cum)
    ew_excl = jnp.exp(cum - logw)
    e_inv = jnp.exp(-cum)
    a_t = a_vec * ew_excl
    r_t = r * ew_incl
    b_t = b_vec * e_inv
    k_t = k2 * e_inv

    ri = lax.broadcasted_iota(jnp.int32, (R, GROUP_W), 0)
    li = lax.broadcasted_iota(jnp.int32, (R, GROUP_W), 1)
    one_zero = lambda m: jnp.where(m, 1.0, 0.0).astype(BF16)
    head_mask = one_zero((ri // C) == (li // HEAD))
    wi = lax.broadcasted_iota(jnp.int32, (R, R), 0)
    wj = lax.broadcasted_iota(jnp.int32, (R, R), 1)
    wide_mask = one_zero((wi // C) == (wj // C))
    tt = lax.broadcasted_iota(jnp.int32, (C, R), 0)
    ss = lax.broadcasted_iota(jnp.int32, (C, R), 1) % C
    strict = tt > ss
    incl = tt >= ss
    eye_w = jnp.where(tt == ss, 1.0, 0.0)
    di = lax.broadcasted_iota(jnp.int32, (GROUP_W, GROUP_W), 0)
    dj = lax.broadcasted_iota(jnp.int32, (GROUP_W, GROUP_W), 1)
    state_mask = (di // HEAD) == (dj // HEAD)

    def bd(x, mask):
        if C % 16 == 0:
            return jnp.concatenate([x.astype(BF16)] * GROUP_HEADS, axis=0) * mask
        return (jnp.concatenate([x] * GROUP_HEADS, axis=0) * mask.astype(F32)).astype(BF16)

    def dot_cat(lhs, rhs):
        if all(a.shape[1] % LANES == 0 for a in lhs):
            return _dot(jnp.concatenate(lhs, axis=1), jnp.concatenate(rhs, axis=0))
        return sum(_dot(a, b) for a, b in zip(lhs, rhs))

    n_levels = int(math.log2(C))

    ops = {}
    for b in range(NB):
        for ch in range(CH):
            for g in range(N_GROUPS):
                r0 = (b * CH + ch) * C
                rs = slice(r0, r0 + C)
                ls = slice(g * GROUP_W, (g + 1) * GROUP_W)
                g_c = ew_incl[r0 + C - 1:r0 + C, ls]
                ops[b, ch, g] = dict(
                    a=a_t[rs, ls].astype(BF16), r=r_t[rs, ls].astype(BF16),
                    bk_hat=jnp.concatenate([b_t[rs, ls] * g_c, k_t[rs, ls] * g_c], axis=0),
                    b_bd=bd(b_t[rs, ls], head_mask), k_bd=bd(k_t[rs, ls], head_mask),
                    v=v[rs, ls], v_bd=bd(v[rs, ls], head_mask), g_c=g_c)
    for o in ops.values():
        ar = jnp.concatenate([o["a"], o["r"]], axis=0)
        nb = _dot_nt(ar, o["b_bd"])
        nk = _dot_nt(ar, o["k_bd"])
        o["n_ab"] = jnp.where(strict, nb[0:C], 0.0)
        o["m_rb"] = jnp.where(incl, nb[C:2 * C], 0.0)
        o["n_ak"] = jnp.where(strict, nk[0:C], 0.0)
        o["m_rk"] = jnp.where(incl, nk[C:2 * C], 0.0)
    for o in ops.values():
        o["t_inv"] = eye_w + o["n_ab"]
        o["p"] = _dot(o["n_ab"], bd(o["n_ab"], wide_mask))
    for j in range(1, n_levels - 1):
        for o in ops.values():
            both = _dot(jnp.concatenate([o["p"], o["t_inv"]], axis=0), bd(o["p"], wide_mask))
            o["p"] = both[0:C]
            o["t_inv"] = o["t_inv"] + both[C:2 * C]
    for o in ops.values():
        o["t_inv"] = o["t_inv"] + _dot(o["t_inv"], bd(o["p"], wide_mask))

    for ch in range(CH):
        now = [(b, g, ops[b, ch, g]) for b in range(NB) for g in range(N_GROUPS)]
        for b, g, o in now:
            o["s_prev"] = s_scr[b, g]
            o["s_bf"] = o["s_prev"].astype(BF16)
            o["rhs"] = dot_cat([o["a"], o["n_ak"]], [o["s_bf"], o["v_bd"]])
        for b, g, o in now:
            o["u"] = _dot(o["t_inv"], bd(o["rhs"], head_mask))
        for b, g, o in now:
            o["y"] = dot_cat([o["r"], o["m_rb"], o["m_rk"]],
                             [o["s_bf"], bd(o["u"], head_mask), o["v_bd"]])
        for b, g, o in now:
            g_col = jnp.transpose(jnp.broadcast_to(o["g_c"], (GROUP_W, GROUP_W)))
            upd = _dot_tn(o["bk_hat"], jnp.concatenate([o["u"], o["v"]], axis=0))
            s_scr[b, g] = o["s_prev"] * g_col + jnp.where(state_mask, upd, 0.0)

    y = jnp.concatenate(
        [jnp.concatenate([ops[b, ch, g]["y"] for g in range(N_GROUPS)], axis=-1)
         for b in range(NB) for ch in range(CH)], axis=0)
    inv_n = 1.0 / HEAD
    mean = head_sum(y) * inv_n
    d = y - mean
    var = head_sum(d * d) * inv_n
    yn = d * lax.rsqrt(var + GN_EPS) * lnw_ref[...] + lnb_ref[...]
    bonus = head_sum(r * k2 * rk_ref[...]) * v
    gate = gr_ref[...].reshape(NS * C, RWKV_WIDTH)
    out = (yn + bonus) * (gate * _sigmoid(gate))
    y_ref[...] = out.reshape(NB, CH * C, RWKV_WIDTH).astype(y_ref.dtype)

    @pl.when(c_idx == last)
    def _():
        for b in range(NB):
            shift_ref[b] = last_rows[b]
            for g in range(N_GROUPS):
                s_bd = s_scr[b, g]
                s_t = s_bd[0:HEAD]
                for h in range(1, GROUP_HEADS):
                    s_t = s_t + s_bd[h * HEAD:(h + 1) * HEAD]
                heads = exact_nt(eye_g, s_t)
                sout_ref[b, g * GROUP_HEADS:(g + 1) * GROUP_HEADS] = heads.reshape(
                    GROUP_HEADS, HEAD, HEAD)


def _rwkv(zsh, gr, shift_prev, s0, p, chunk, t_valid, seqs_per_step, chunks_per_step):
    b, t, _ = zsh.shape
    step_tokens = chunk * chunks_per_step
    nc = t // step_tokens
    nb = seqs_per_step
    tok = lambda width: pl.BlockSpec((nb, step_tokens, width), lambda i, c: (i, c, 0))
    per_b = lambda shape: pl.BlockSpec((nb,) + shape, lambda i, c: (i,) + (0,) * len(shape))
    const = lambda a: pl.BlockSpec(a.shape, lambda i, c: (0,) * a.ndim)
    params = (p["mu"], p["w0"], p["w1"], p["a0"], p["w2"], p["k_k"], p["k_a"], p["r_k"],
              p["lnx_w"], p["lnx_b"], p["e_ones"])
    return pl.pallas_call(
        functools.partial(_rwkv_kernel, chunk=chunk, t_valid=t_valid),
        grid=(b // nb, nc),
        in_specs=[tok(SHIFT_WIDTH), tok(512), per_b((1, SHIFT_WIDTH)),
                  per_b((RWKV_HEADS, HEAD, HEAD))] + [const(a) for a in params],
        out_specs=(tok(512), per_b((RWKV_HEADS, HEAD, HEAD)), per_b((1, SHIFT_WIDTH))),
        out_shape=(jax.ShapeDtypeStruct((b, t, 512), BF16),
                   jax.ShapeDtypeStruct((b, RWKV_HEADS, HEAD, HEAD), F32),
                   jax.ShapeDtypeStruct((b, 1, SHIFT_WIDTH), F32)),
        scratch_shapes=[pltpu.VMEM((nb, N_GROUPS, GROUP_W, GROUP_W), F32),
                        pltpu.VMEM((nb, 1, SHIFT_WIDTH), F32)],
        compiler_params=pltpu.CompilerParams(dimension_semantics=("arbitrary", "arbitrary"),
                                             vmem_limit_bytes=VMEM_LIMIT_V7X),
        name="rwkv7",
    )(zsh, gr, shift_prev, s0, *params)


def _lambda(lq1_ref, lk1_ref, lq2_ref, lk2_ref, lam_init):
    s1 = jnp.sum(lq1_ref[...] * lk1_ref[...], axis=-1, keepdims=True)
    s2 = jnp.sum(lq2_ref[...] * lk2_ref[...], axis=-1, keepdims=True)
    return jnp.exp(s1) - jnp.exp(s2) + lam_init


def _diff_epilogue(acc0, l0, acc1, l1, lam, lam_init, subln, gate):
    o = acc0 / l0 - lam * (acc1 / l1)
    ms = jnp.mean(o * o, axis=-1, keepdims=True)
    on = o * lax.rsqrt(ms + SUBLN_EPS) * subln * (1.0 - lam_init)
    return on * (gate * _sigmoid(gate))


def _attn_kernel(rb_ref, li_ref, qt_ref, k_ref, vt_ref, gd_ref, lq1_ref, lk1_ref, lq2_ref,
                 lk2_ref, sub_ref, o_ref, bias_scr, s_scr, acc_scr, *, tq, tk):
    h = pl.program_id(1)
    qi = pl.program_id(2)
    nsub = tq // tk
    first_near = nsub * qi - 1

    @pl.when(qi == 0)
    def _():
        key = lax.broadcasted_iota(jnp.int32, (tk, tq), 0)
        qry = lax.broadcasted_iota(jnp.int32, (tk, tq), 1)
        for i in range(nsub + 1):
            dist = qry - key - (i - 1) * tk
            bias = _rel_bias_minus_far(dist, rb_ref, h) * LOG2E
            bias = jnp.where(dist >= 0, bias, NEG_INF)
            bias_scr[i] = jnp.concatenate([bias, bias], axis=1)

    qt = qt_ref[...]
    feat = lax.broadcasted_iota(jnp.int32, (2 * HEAD, tq), 0)
    qt2 = jnp.concatenate([jnp.where(feat < HEAD, qt, jnp.zeros_like(qt)),
                           jnp.where(feat >= HEAD, qt, jnp.zeros_like(qt))], axis=1)

    acc_scr[...] = jnp.zeros(acc_scr.shape, F32)
    ones_rows = jnp.ones((ONES_ROWS, tk), BF16)

    def logits(kj):
        off = pl.multiple_of(kj * tk, tk)
        return jnp.dot(k_ref[pl.ds(off, tk), :], qt2, preferred_element_type=F32)

    def absorb(s, kj, m_old):
        off = pl.multiple_of(kj * tk, tk)
        vt_ext = jnp.concatenate([vt_ref[:, pl.ds(off, tk)], ones_rows], axis=0)
        m_new = jnp.maximum(m_old, jnp.max(s, axis=0, keepdims=True))
        pm = jnp.exp2(s - m_new).astype(BF16)
        acc_scr[...] = (jnp.exp2(m_old - m_new) * acc_scr[...]
                        + jnp.dot(vt_ext, pm, preferred_element_type=F32))
        return m_new

    assert nsub == 2
    m_run = jnp.full((1, 2 * tq), NEG_INF, F32)

    @pl.when(qi >= 1)
    def _():
        s_scr[0] = logits(0)

    def far_pair(i, m):
        s_scr[1] = logits(2 * i + 1)
        m = absorb(s_scr[0], 2 * i, m)
        s_scr[0] = logits(2 * i + 2)
        return absorb(s_scr[1], 2 * i + 1, m)

    m_run = lax.fori_loop(0, qi - 1, far_pair, m_run)

    def finish_below_diagonal(m):
        s_scr[1] = logits(first_near)
        m = absorb(s_scr[0], first_near - 1, m)
        s_scr[0] = logits(first_near + 1)
        m = absorb(s_scr[1] + bias_scr[0], first_near, m)
        s_scr[1] = logits(first_near + 2)
        m = absorb(s_scr[0] + bias_scr[1], first_near + 1, m)
        return absorb(s_scr[1] + bias_scr[2], first_near + 2, m)

    def first_query_tile(m):
        s_scr[0] = logits(0)
        s_scr[1] = logits(1)
        m = absorb(s_scr[0] + bias_scr[1], 0, m)
        return absorb(s_scr[1] + bias_scr[2], 1, m)

    lax.cond(qi >= 1, finish_below_diagonal, first_query_tile, m_run)

    lam_init = li_ref[0]
    lam = _lambda(lq1_ref, lk1_ref, lq2_ref, lk2_ref, lam_init)
    l_fin = acc_scr[DIFF_V:DIFF_V + 1, :]
    o_t = (acc_scr[0:DIFF_V, 0:tq] / l_fin[:, 0:tq]
           - lam * (acc_scr[0:DIFF_V, tq:2 * tq] / l_fin[:, tq:2 * tq]))
    ms = jnp.mean(o_t * o_t, axis=0, keepdims=True)
    o = jnp.transpose(o_t * lax.rsqrt(ms + SUBLN_EPS))
    gate = gd_ref[...]
    out = o * sub_ref[...] * (1.0 - lam_init) * (gate * _sigmoid(gate))
    o_ref[...] = out.astype(o_ref.dtype)


def _attn_prompt(qt, kb, vt, gd, rel_bias, lam_init, lam_vecs, subln, batch, tq, tk):
    m = kb.shape[0]
    t = m // batch
    nq = t // tq
    assert tk >= MAX_DISTANCE and tq % tk == 0
    tile = pl.BlockSpec((tq, DIFF_V), lambda b, h, i: (b * nq + i, h))
    tile_t = pl.BlockSpec((DIFF_V, tq), lambda b, h, i: (h, b * nq + i))
    seq = pl.BlockSpec((t, DIFF_V), lambda b, h, i: (b, h))
    seq_t = pl.BlockSpec((DIFF_V, t), lambda b, h, i: (h, b))
    vec = lambda a: pl.BlockSpec(a.shape, lambda b, h, i: (0,) * a.ndim)
    smem = pl.BlockSpec(memory_space=pltpu.SMEM)
    return pl.pallas_call(
        functools.partial(_attn_kernel, tq=tq, tk=tk),
        grid=(batch, DIFF_HEADS, nq),
        in_specs=([smem, smem, tile_t, seq, seq_t, tile] + [vec(a) for a in lam_vecs]
                  + [vec(subln)]),
        out_specs=tile,
        out_shape=jax.ShapeDtypeStruct((m, DIFF_WIDTH), BF16),
        scratch_shapes=[pltpu.VMEM((tq // tk + 1, tk, 2 * tq), F32),
                        pltpu.VMEM((2, tk, 2 * tq), F32),
                        pltpu.VMEM((DIFF_V + ONES_ROWS, 2 * tq), F32)],
        compiler_params=pltpu.CompilerParams(
            dimension_semantics=("arbitrary", "arbitrary", "arbitrary"),
            vmem_limit_bytes=VMEM_LIMIT_V7X),
        name="diff_attn_prompt",
    )(rel_bias, lam_init, qt, kb, vt, gd, *lam_vecs, subln)


TOKENS_PADDED = 8
SCORE_ROWS = DIFF_HEADS * 2 * TOKENS_PADDED
PAGE_ROWS = PAGE * DIFF_HEADS
PAGES_PER_SOFTMAX_STEP = 2


def _decode_kernel(pt_ref, rb_ref, li_ref, q_ref, kn_ref, vn_ref, gd_ref, lq1_ref, lk1_ref,
                   lq2_ref, lk2_ref, sub_ref, *rest, pages_per_step):
    P = pages_per_step
    k_refs = rest[0:P]
    v_refs = rest[P:2 * P]
    o_ref = rest[2 * P]
    m_scr, acc_scr = rest[2 * P + 1:]
    del pt_ref
    j = pl.program_id(1)
    last = pl.num_programs(1) - 1
    TP = TOKENS_PADDED
    NR = SCORE_ROWS

    @pl.when(j == 0)
    def _():
        m_scr[...] = jnp.full(m_scr.shape, NEG_INF, F32)
        acc_scr[...] = jnp.zeros(acc_scr.shape, F32)

    q = q_ref[...]
    lane = lax.broadcasted_iota(jnp.int32, (TP, DIFF_V), 1)
    q_rows = jnp.concatenate(
        [jnp.where((lane // HEAD) == c, q[:, h * DIFF_V:(h + 1) * DIFF_V], 0.0)
         for h in range(DIFF_HEADS) for c in range(2)], axis=0)
    q_rows = q_rows.astype(BF16).astype(F32)
    row = lax.broadcasted_iota(jnp.int32, (NR, 1), 0)
    row_head = row // (2 * TP)
    row_tok = row % TP

    def scores(k_rows, bias=None, causal=False):
        n = k_rows.shape[0]
        s = lax.dot_general(q_rows, k_rows, (((1,), (1,)), ((), ())),
                            preferred_element_type=F32)
        col = lax.broadcasted_iota(jnp.int32, (1, n), 1)
        keep = (col % DIFF_HEADS) == row_head
        if causal:
            keep = keep & (row_tok >= col // DIFF_HEADS)
        if bias is not None:
            s = s + bias
        return jnp.where(keep, s, NEG_INF)

    def row_bias(n, dist_of):
        col_tok = lax.broadcasted_iota(jnp.int32, (1, n), 1) // DIFF_HEADS
        t_row = lax.broadcasted_iota(jnp.int32, (2 * TP, 1), 0) % TP
        dist = dist_of(t_row, col_tok)
        return jnp.concatenate([_rel_bias_minus_far(dist, rb_ref, h) for h in range(DIFF_HEADS)],
                               axis=0) * LOG2E

    def update(s_list, v_list):
        m_old = m_scr[...]
        m_new = m_old
        for s in s_list:
            m_new = jnp.maximum(m_new, jnp.max(s, axis=-1, keepdims=True))
        acc = jnp.exp2(m_old - m_new) * acc_scr[...]
        for s, v in zip(s_list, v_list):
            v_ext = jnp.concatenate([v, jnp.ones(v.shape, F32)], axis=-1)
            acc = acc + jnp.dot(jnp.exp2(s - m_new), v_ext, preferred_element_type=F32)
        acc_scr[...] = acc
        m_scr[...] = m_new

    page_k = lambda i: k_refs[i][...]
    page_v = lambda i: v_refs[i][...]
    G = PAGES_PER_SOFTMAX_STEP

    @pl.when(j != last)
    def _():
        for i0 in range(0, P, G):
            update([scores(page_k(i)) for i in range(i0, i0 + G)],
                   [page_v(i) for i in range(i0, i0 + G)])

    @pl.when(j == last)
    def _():
        bias_p = row_bias(PAGE_ROWS, lambda t_row, col_tok: (PAGE + t_row) - col_tok)
        n_new = TP * DIFF_HEADS
        bias_n = row_bias(n_new, lambda t_row, col_tok: t_row - col_tok)
        for i0 in range(0, P - G, G):
            update([scores(page_k(i)) for i in range(i0, i0 + G)],
                   [page_v(i) for i in range(i0, i0 + G)])
        s_list = [scores(page_k(i)) for i in range(P - G, P - 1)]
        s_list.append(scores(page_k(P - 1), bias=bias_p))
        s_list.append(scores(kn_ref[...], bias=bias_n, causal=True))
        update(s_list, [page_v(i) for i in range(P - G, P)] + [vn_ref[...]])

        lam_init = li_ref[0]
        lam = _lambda(lq1_ref, lk1_ref, lq2_ref, lk2_ref, lam_init)
        gate = gd_ref[...]
        outs = []
        for h in range(DIFF_HEADS):
            r0 = h * 2 * TP
            r1 = r0 + TP
            outs.append(_diff_epilogue(
                acc_scr[r0:r0 + TP, 0:DIFF_V], acc_scr[r0:r0 + TP, DIFF_V:DIFF_V + 1],
                acc_scr[r1:r1 + TP, 0:DIFF_V], acc_scr[r1:r1 + TP, DIFF_V:DIFF_V + 1],
                lam, lam_init, sub_ref[...], gate[:, h * DIFF_V:(h + 1) * DIFF_V]))
        o_ref[...] = jnp.concatenate(outs, axis=-1).astype(o_ref.dtype)


def _attn_sample(q, kn, vn, gd, cache_k, cache_v, page_ids, rel_bias, lam_init, lam_vecs, subln,
                 pages_per_step):
    b = q.shape[0]
    n_pages = page_ids.shape[0] // b
    P = pages_per_step
    per_b = lambda shape: pl.BlockSpec((None,) + shape, lambda i, j, pt: (i, 0, 0))
    vec = lambda a: pl.BlockSpec(a.shape, lambda i, j, pt: (0,) * a.ndim)
    smem = pl.BlockSpec(memory_space=pltpu.SMEM)
    tok = per_b((TOKENS_PADDED, DIFF_WIDTH))
    new = per_b((TOKENS_PADDED * DIFF_HEADS, DIFF_V))

    def page_spec(slot):
        return pl.BlockSpec((None, PAGE_ROWS, DIFF_V),
                            lambda i, j, pt: (pt[i * n_pages + j * P + slot], 0, 0))

    grid_spec = pltpu.PrefetchScalarGridSpec(
        num_scalar_prefetch=1,
        grid=(b, n_pages // P),
        in_specs=([smem, smem, tok, new, new, tok] + [vec(a) for a in lam_vecs]
                  + [vec(subln)] + [page_spec(s) for s in range(P)]
                  + [page_spec(s) for s in range(P)]),
        out_specs=tok,
        scratch_shapes=[pltpu.VMEM((SCORE_ROWS, 1), F32),
                        pltpu.VMEM((SCORE_ROWS, 2 * DIFF_V), F32)],
    )
    return pl.pallas_call(
        functools.partial(_decode_kernel, pages_per_step=P),
        grid_spec=grid_spec,
        out_shape=jax.ShapeDtypeStruct((b, TOKENS_PADDED, DIFF_WIDTH), BF16),
        compiler_params=pltpu.CompilerParams(dimension_semantics=("arbitrary", "arbitrary"),
                                             vmem_limit_bytes=VMEM_LIMIT_V7X),
        name="diff_attn_sample",
    )(page_ids, rel_bias, lam_init, q, kn, vn, gd, *lam_vecs, subln,
      *([cache_k] * P), *([cache_v] * P))


def _layer_params(l, pre_norm, post_norm, w_in, mu_shift, w0, w_up, a0, a_up, k_k, k_a, r_k,
                  lnx_w, lnx_b, lam_q1, lam_k1, lam_q2, lam_k2, subln, w_out):
    row = lambda a: a.reshape(1, -1)
    zeros = jnp.zeros((LORA, RWKV_WIDTH), F32)
    head_of = jnp.arange(GROUP_W) // HEAD
    return dict(
        pre_norm=row(pre_norm[l]), post_norm=row(post_norm[l]),
        w_in=w_in[l].astype(BF16), w_out=w_out[l].astype(BF16),
        mu=row(mu_shift[l]), w0=row(w0[l]), a0=row(a0[l]),
        w1=jnp.concatenate([w_up[l], zeros], axis=0).astype(BF16),
        w2=jnp.concatenate([zeros, a_up[l]], axis=0).astype(BF16),
        k_k=row(k_k[l]), k_a=row(k_a[l]), r_k=row(r_k[l]),
        lnx_w=row(lnx_w[l]), lnx_b=row(lnx_b[l]),
        e_ones=(head_of[:, None] == head_of[None, :]).astype(BF16),
        lam_vecs=(row(lam_q1[l]), row(lam_k1[l]), row(lam_q2[l]), row(lam_k2[l])),
        subln=row(subln[l]),
        lam_init=jnp.full((1,), 0.8 - 0.6 * math.exp(-0.3 * l), F32),
    )


def _mixer(x2d, batch, p, shift_prev, s0, project, attend, tm_out, chunk, t_valid,
           seqs_per_step, chunks_per_step=1):
    m = x2d.shape[0]
    t = m // batch
    zsh, gr, gd, attn_operands = project(x2d)
    y_r, s_bd, shift = _rwkv(zsh.reshape(batch, t, SHIFT_WIDTH), gr.reshape(batch, t, 512),
                             shift_prev, s0, p, chunk, t_valid, seqs_per_step, chunks_per_step)
    y_d = attend(gd, *attn_operands)
    x_new = _outproj(y_r.reshape(m, RWKV_WIDTH), y_d.reshape(m, DIFF_WIDTH), p["w_out"],
                     p["post_norm"], x2d, tm_out)
    return x_new, shift, s_bd, attn_operands


def kernel(x_prompt, x_sample, cache_k, cache_v, state_wkv, state_shift, page_table, rel_bias,
           pre_norm, post_norm, w_in, mu_shift, w0, w_up, a0, a_up, k_k, k_a, r_k, lnx_w, lnx_b,
           lam_q1, lam_k1, lam_q2, lam_k2, subln, w_out):
    return _forward(x_prompt, x_sample, cache_k, cache_v, state_wkv, state_shift, page_table,
                    rel_bias, pre_norm, post_norm, w_in, mu_shift, w0, w_up, a0, a_up, k_k, k_a,
                    r_k, lnx_w, lnx_b, lam_q1, lam_k1, lam_q2, lam_k2, subln, w_out,
                    PROMPT_TM=256, PROMPT_TM_OUT=512, PROMPT_CHUNK=64, PROMPT_TQ=512,
                    PROMPT_TK=256, PAGES_PER_STEP=32, SAMPLE_SEQS_PER_STEP=4,
                    PROMPT_CHUNKS_PER_STEP=2)


def _forward(x_prompt, x_sample, cache_k, cache_v, state_wkv, state_shift, page_table, rel_bias,
             pre_norm, post_norm, w_in, mu_shift, w0, w_up, a0, a_up, k_k, k_a, r_k, lnx_w, lnx_b,
             lam_q1, lam_k1, lam_q2, lam_k2, subln, w_out, *, PROMPT_TM, PROMPT_TM_OUT,
             PROMPT_CHUNK, PROMPT_TQ, PROMPT_TK, PAGES_PER_STEP, SAMPLE_SEQS_PER_STEP,
             PROMPT_CHUNKS_PER_STEP):
    bp, tp, d = x_prompt.shape
    bs, ts, _ = x_sample.shape
    depth = w_in.shape[0]
    n_pool = cache_k.shape[1]
    n_pages = page_table.shape[1]
    TS_PAD = TOKENS_PADDED

    xp = x_prompt.reshape(bp * tp, d)
    xs = jnp.pad(x_sample, ((0, 0), (0, TS_PAD - ts), (0, 0))).reshape(bs * TS_PAD, d)
    ck = cache_k.reshape(depth * n_pool, PAGE_ROWS, DIFF_V)
    cv = cache_v.reshape(depth * n_pool, PAGE_ROWS, DIFF_V)
    zero_shift = jnp.zeros((bp, 1, SHIFT_WIDTH), F32)
    zero_state = jnp.zeros((bp, RWKV_HEADS, HEAD, HEAD), F32)

    outs = {name: [] for name in ("wp", "sp", "ks", "vs", "ws", "ss")}
    kv_rows = None
    for l in range(depth):
        p = _layer_params(l, pre_norm, post_norm, w_in, mu_shift, w0, w_up, a0, a_up, k_k, k_a,
                          r_k, lnx_w, lnx_b, lam_q1, lam_k1, lam_q2, lam_k2, subln, w_out)

        def project_prompt(x2d):
            zsh, gr, qt, k_rows, v_rows, gd, kb, vt = _inproj_prompt(
                x2d, p["pre_norm"], p["w_in"], PROMPT_TM, l, depth, kv_rows)
            return zsh, gr, gd, (qt, kb, vt, k_rows, v_rows)

        def attend_prompt(gd, qt, kb, vt, k_rows, v_rows):
            return _attn_prompt(qt, kb, vt, gd, rel_bias, p["lam_init"], p["lam_vecs"],
                                p["subln"], bp, PROMPT_TQ, PROMPT_TK)

        def project_sample(x2d):
            zsh, gr, q, k, v, gd = _inproj_sample(x2d, p["pre_norm"], p["w_in"], bs * TS_PAD)
            return zsh, gr, gd, (q, k, v)

        page_ids = (page_table + l * n_pool).reshape(-1).astype(jnp.int32)

        def attend_sample(gd, q, k, v):
            r3 = lambda a: a.reshape(bs, TS_PAD, DIFF_WIDTH)
            rows = lambda a: a.reshape(bs, TS_PAD * DIFF_HEADS, DIFF_V)
            return _attn_sample(r3(q), rows(k), rows(v), r3(gd), ck, cv, page_ids, rel_bias,
                                p["lam_init"], p["lam_vecs"], p["subln"], PAGES_PER_STEP)

        xp, sp, wp, prompt_ops = _mixer(xp, bp, p, zero_shift, zero_state, project_prompt,
                                        attend_prompt, PROMPT_TM_OUT, PROMPT_CHUNK, PROMPT_CHUNK,
                                        bp, PROMPT_CHUNKS_PER_STEP)
        kv_rows = prompt_ops[3:5]
        xs, ss, ws, (_, k_s, v_s) = _mixer(xs, bs, p, state_shift[l][:, None, :],
                                           state_wkv[l], project_sample,
                                           attend_sample, bs * TS_PAD, TS_PAD, ts,
                                           SAMPLE_SEQS_PER_STEP)
        outs["wp"].append(wp)
        outs["sp"].append(sp.reshape(bp, SHIFT_WIDTH))
        outs["ks"].append(k_s.reshape(bs, TS_PAD, DIFF_HEADS, DIFF_V)[:, :ts])
        outs["vs"].append(v_s.reshape(bs, TS_PAD, DIFF_HEADS, DIFF_V)[:, :ts])
        outs["ws"].append(ws)
        outs["ss"].append(ss.reshape(bs, SHIFT_WIDTH))

    st = lambda name: jnp.stack(outs[name])
    kv_shape = (depth, bp, tp, DIFF_HEADS, DIFF_V)
    return (xp.reshape(bp, tp, d), xs.reshape(bs, TS_PAD, d)[:, :ts], kv_rows[0].reshape(kv_shape),
            kv_rows[1].reshape(kv_shape), st("wp"), st("sp"), st("ks"), st("vs"), st("ws"),
            st("ss"))
```

```python
import functools
import math

import jax
import jax.numpy as jnp
from jax import lax
from jax.experimental import pallas as pl
from jax.experimental.pallas import tpu as pltpu

F32 = jnp.float32
BF16 = jnp.bfloat16

RWKV_WIDTH = 512
DIFF_WIDTH = 512
HEAD = 64
RWKV_HEADS = RWKV_WIDTH // HEAD
LORA = 64
DIFF_HEADS = 4
DIFF_V = 128
NUM_BUCKETS = 32
MAX_EXACT = NUM_BUCKETS // 2
MAX_DISTANCE = 128
PAGE = 128
RMS_EPS = 1e-6
SUBLN_EPS = 1e-5
GN_EPS = 64e-5
NEG_INF = -1e30
LOG2E = 1.4426950408889634
SHIFT_WIDTH = 3 * RWKV_WIDTH + 2 * LORA
SEG_SHIFT = (0, SHIFT_WIDTH)
SEG_GR = (SHIFT_WIDTH, SHIFT_WIDTH + 512)
SEG_Q = (SEG_GR[1], SEG_GR[1] + 512)
SEG_K = (SEG_Q[1], SEG_Q[1] + 512)
SEG_V = (SEG_K[1], SEG_K[1] + 512)
SEG_GD = (SEG_V[1], SEG_V[1] + 512)

VMEM_LIMIT_V7X = 48 * 1024 * 1024
LANES = 128
ONES_ROWS = 16
GROUP_HEADS = 4
GROUP_W = GROUP_HEADS * HEAD
N_GROUPS = RWKV_HEADS // GROUP_HEADS


def _dot(a, b):
    return jnp.dot(a.astype(BF16), b.astype(BF16), preferred_element_type=F32)


def _dot_nt(a, b):
    return lax.dot_general(a.astype(BF16), b.astype(BF16), (((1,), (1,)), ((), ())),
                           preferred_element_type=F32)


def _dot_tn(a, b):
    return lax.dot_general(a.astype(BF16), b.astype(BF16), (((0,), (0,)), ((), ())),
                           preferred_element_type=F32)


def _split(x):
    hi = x.astype(BF16)
    lo = (x - hi.astype(F32)).astype(BF16)
    return hi, lo


def _dot_exact_lhs(a_exact, b):
    hi, lo = _split(b)
    return (jnp.dot(a_exact, hi, preferred_element_type=F32)
            + jnp.dot(a_exact, lo, preferred_element_type=F32))


def _sigmoid(x):
    return 1.0 / (1.0 + jnp.exp(-x))


def _rel_bias_minus_far(dist, rb_ref, head):
    n = jnp.maximum(dist, 0)
    nf = jnp.maximum(n, 1).astype(F32)
    large = MAX_EXACT + (jnp.log(nf / MAX_EXACT) / math.log(MAX_DISTANCE / MAX_EXACT)
                         * (NUM_BUCKETS - MAX_EXACT)).astype(jnp.int32)
    large = jnp.minimum(large, NUM_BUCKETS - 1)
    bucket = jnp.where(n < MAX_EXACT, n, large)
    far = rb_ref[NUM_BUCKETS - 1, head]
    out = jnp.zeros(dist.shape, F32)
    for b in range(NUM_BUCKETS - 1):
        out = jnp.where(bucket == b, rb_ref[b, head] - far, out)
    return out


def _projected_segments(x_ref, g_ref, w_ref):
    x = x_ref[...]
    ms = jnp.mean(x * x, axis=-1, keepdims=True)
    h = (x * lax.rsqrt(ms + RMS_EPS) * g_ref[...]).astype(BF16)

    def seg(lo_hi):
        lo, hi = lo_hi
        return jnp.dot(h, w_ref[:, lo:hi], preferred_element_type=F32)

    return seg


Q_SCALE = LOG2E * HEAD ** -0.5


def _inproj_sample_kernel(x_ref, g_ref, w_ref, zsh_ref, gr_ref, q_ref, k_ref, v_ref, gd_ref):
    seg = _projected_segments(x_ref, g_ref, w_ref)
    zsh_ref[...] = seg(SEG_SHIFT)
    gr_ref[...] = seg(SEG_GR)
    q_ref[...] = seg(SEG_Q) * Q_SCALE
    k_ref[...] = seg(SEG_K)
    v_ref[...] = seg(SEG_V)
    gd_ref[...] = seg(SEG_GD)


def _inproj_prompt_kernel(x_ref, g_ref, w_ref, *refs):
    zsh_ref, gr_ref, qt_ref, krows_ref, vrows_ref, gd_ref, kb_ref, vt_ref = refs[-8:]
    seg = _projected_segments(x_ref, g_ref, w_ref)
    tm = x_ref.shape[0]
    zsh_ref[...] = seg(SEG_SHIFT)
    gr_ref[...] = seg(SEG_GR)
    qt_ref[...] = jnp.transpose(seg(SEG_Q) * Q_SCALE).astype(BF16)
    gd_ref[...] = seg(SEG_GD)
    k = seg(SEG_K)
    v = seg(SEG_V)
    kb_ref[...] = k.astype(BF16)
    vt_ref[...] = jnp.transpose(v).astype(BF16)
    for hd in range(DIFF_HEADS):
        cols = slice(hd * DIFF_V, (hd + 1) * DIFF_V)
        krows_ref[pl.ds(hd, tm, stride=DIFF_HEADS), :] = k[:, cols]
        vrows_ref[pl.ds(hd, tm, stride=DIFF_HEADS), :] = v[:, cols]


def _inproj_sample(x, g, w_bf16, tm):
    m, d = x.shape
    row = lambda width: pl.BlockSpec((tm, width), lambda i: (i, 0))
    f32 = jax.ShapeDtypeStruct((m, 512), F32)
    return pl.pallas_call(
        _inproj_sample_kernel,
        grid=(m // tm,),
        in_specs=[row(d), pl.BlockSpec((1, d), lambda i: (0, 0)),
                  pl.BlockSpec(w_bf16.shape, lambda i: (0, 0))],
        out_specs=(row(SHIFT_WIDTH), row(512), row(512), row(512), row(512), row(512)),
        out_shape=(jax.ShapeDtypeStruct((m, SHIFT_WIDTH), F32), f32, f32, f32, f32, f32),
        compiler_params=pltpu.CompilerParams(dimension_semantics=("arbitrary",),
                                             vmem_limit_bytes=VMEM_LIMIT_V7X),
        name="inproj_sample",
    )(x, g, w_bf16)


def _inproj_prompt(x, g, w_bf16, tm, layer, depth, kv_rows):
    m, d = x.shape
    steps = m // tm
    row = lambda width: pl.BlockSpec((tm, width), lambda i: (i, 0))
    col = pl.BlockSpec((512, tm), lambda i: (0, i))
    rows = pl.BlockSpec((tm * DIFF_HEADS, DIFF_V), lambda i: (layer * steps + i, 0))
    f32 = jax.ShapeDtypeStruct((m, 512), F32)
    bf_t = jax.ShapeDtypeStruct((512, m), BF16)
    rows_shape = jax.ShapeDtypeStruct((depth * m * DIFF_HEADS, DIFF_V), F32)
    in_specs = [row(d), pl.BlockSpec((1, d), lambda i: (0, 0)),
                pl.BlockSpec(w_bf16.shape, lambda i: (0, 0), pipeline_mode=pl.Buffered(1))]
    operands = [x, g, w_bf16]
    aliases = {}
    if kv_rows is not None:
        in_specs += [pl.BlockSpec(memory_space=pl.ANY)] * 2
        operands += list(kv_rows)
        aliases = {3: 3, 4: 4}
    return pl.pallas_call(
        _inproj_prompt_kernel,
        grid=(steps,),
        in_specs=in_specs,
        out_specs=(row(SHIFT_WIDTH), row(512), col, rows, rows, row(512), row(512), col),
        out_shape=(jax.ShapeDtypeStruct((m, SHIFT_WIDTH), F32), f32, bf_t, rows_shape, rows_shape,
                   f32, jax.ShapeDtypeStruct((m, 512), BF16), bf_t),
        input_output_aliases=aliases,
        compiler_params=pltpu.CompilerParams(dimension_semantics=("arbitrary",),
                                             vmem_limit_bytes=VMEM_LIMIT_V7X),
        name="inproj_prompt",
    )(*operands)


def _outproj_kernel(yr_ref, yd_ref, w_ref, g_ref, x_ref, o_ref):
    out = (jnp.dot(yr_ref[...], w_ref[0:RWKV_WIDTH, :], preferred_element_type=F32)
           + jnp.dot(yd_ref[...], w_ref[RWKV_WIDTH:, :], preferred_element_type=F32))
    ms = jnp.mean(out * out, axis=-1, keepdims=True)
    o_ref[...] = x_ref[...] + out * lax.rsqrt(ms + RMS_EPS) * g_ref[...]


def _outproj(yr, yd, w_bf16, g, x, tm):
    m, d = x.shape
    row = lambda width: pl.BlockSpec((tm, width), lambda i: (i, 0))
    return pl.pallas_call(
        _outproj_kernel,
        grid=(m // tm,),
        in_specs=[row(512), row(512), pl.BlockSpec(w_bf16.shape, lambda i: (0, 0)),
                  pl.BlockSpec((1, d), lambda i: (0, 0)), row(d)],
        out_specs=row(d),
        out_shape=jax.ShapeDtypeStruct((m, d), F32),
        compiler_params=pltpu.CompilerParams(dimension_semantics=("arbitrary",),
                                             vmem_limit_bytes=VMEM_LIMIT_V7X),
        name="outproj",
    )(yr, yd, w_bf16, g, x)


def _rwkv_kernel(zsh_ref, gr_ref, sprev_ref, s0_ref, mu_ref, w0_ref, w1_ref, a0_ref, w2_ref,
                 kk_ref, ka_ref, rk_ref, lnw_ref, lnb_ref, e_ref,
                 y_ref, sout_ref, shift_ref, s_scr, carry_scr, *, chunk, t_valid):
    C = chunk
    NB = zsh_ref.shape[0]
    R = GROUP_HEADS * C
    c_idx = pl.program_id(1)
    last = pl.num_programs(1) - 1

    ii = lax.broadcasted_iota(jnp.int32, (GROUP_W, GROUP_W), 0)
    jj = lax.broadcasted_iota(jnp.int32, (GROUP_W, GROUP_W), 1)
    eye_g = jnp.where(ii == jj, 1.0, 0.0).astype(BF16)
    block_g = (ii // HEAD) == (jj // HEAD)

    def exact_nt(lhs_exact, x):
        hi = x.astype(BF16)
        r1 = x - hi.astype(F32)
        mid = r1.astype(BF16)
        lo = (r1 - mid.astype(F32)).astype(BF16)
        return _dot_nt(lhs_exact, hi) + _dot_nt(lhs_exact, mid) + _dot_nt(lhs_exact, lo)

    @pl.when(c_idx == 0)
    def _():
        carry_scr[...] = sprev_ref[...]
        for b in range(NB):
            for g in range(N_GROUPS):
                heads = s0_ref[b, g * GROUP_HEADS:(g + 1) * GROUP_HEADS]
                s_t = exact_nt(eye_g[0:HEAD, 0:HEAD], heads.reshape(GROUP_W, HEAD))
                s_scr[b, g] = jnp.where(block_g, jnp.concatenate([s_t] * GROUP_HEADS, axis=0),
                                        0.0)

    CH = zsh_ref.shape[1] // C
    NS = NB * CH
    assert CH == 1 or t_valid == C
    z = zsh_ref[...].reshape(NS * C, SHIFT_WIDTH)
    row = lax.broadcasted_iota(jnp.int32, (NS * C, 1), 0)
    tok = row % C
    prev = pltpu.roll(z, 1, axis=0)
    last_rows = []
    for b in range(NB):
        prev = jnp.where(row == b * CH * C, carry_scr[b], prev)
        last_tok = b * CH * C + (CH - 1) * C + t_valid - 1
        last_rows.append(z[last_tok:last_tok + 1, :])
        carry_scr[b] = last_rows[b]
    zm = z + mu_ref[...] * (prev - z)

    r = zm[:, 0:512]
    k = zm[:, 512:1024]
    v = zm[:, 1024:1536]
    zwa = zm[:, 1536:1664]
    lora_w = _dot(jnp.tanh(zwa), w1_ref[...])
    lora_a = _dot(zwa, w2_ref[...])
    logw = -math.exp(-0.5) * _sigmoid(w0_ref[...] + lora_w)
    alr = _sigmoid(a0_ref[...] + lora_a)

    e_ones = e_ref[...]

    def head_sum(x):
        return jnp.concatenate(
            [jnp.dot(x[:, g * GROUP_W:(g + 1) * GROUP_W].astype(BF16), e_ones,
                     preferred_element_type=F32) for g in range(N_GROUPS)], axis=1)

    kk = k * kk_ref[...]
    kk = kk * lax.rsqrt(jnp.maximum(head_sum(kk * kk), 1e-24))
    k2 = k * (1.0 + (alr - 1.0) * ka_ref[...])
    a_vec = -kk
    b_vec = kk * alr
    if t_valid < C:
        valid = tok < t_valid
        logw = jnp.where(valid, logw, 0.0)
        b_vec = jnp.where(valid, b_vec, 0.0)
        k2 = jnp.where(valid, k2, 0.0)

    ti = lax.broadcasted_iota(jnp.int32, (NS * C, NS * C), 0)
    si = lax.broadcasted_iota(jnp.int32, (NS * C, NS * C), 1)
    ltri = jnp.where(((ti // C) == (si // C)) & (ti >= si), 1.0, 0.0).astype(BF16)
    cum = _dot_exact_lhs(ltri, logw)
    ew_incl = jnp.exp(cum)
    ew_excl = jnp.exp(cum - logw)
    e_inv = jnp.exp(-cum)
    a_t = a_vec * ew_excl
    r_t = r * ew_incl
    b_t = b_vec * e_inv
    k_t = k2 * e_inv

    ri = lax.broadcasted_iota(jnp.int32, (R, GROUP_W), 0)
    li = lax.broadcasted_iota(jnp.int32, (R, GROUP_W), 1)
    one_zero = lambda m: jnp.where(m, 1.0, 0.0).astype(BF16)
    head_mask = one_zero((ri // C) == (li // HEAD))
    wi = lax.broadcasted_iota(jnp.int32, (R, R), 0)
    wj = lax.broadcasted_iota(jnp.int32, (R, R), 1)
    wide_mask = one_zero((wi // C) == (wj // C))
    tt = lax.broadcasted_iota(jnp.int32, (C, R), 0)
    ss = lax.broadcasted_iota(jnp.int32, (C, R), 1) % C
    strict = tt > ss
    incl = tt >= ss
    eye_w = jnp.where(tt == ss, 1.0, 0.0)
    di = lax.broadcasted_iota(jnp.int32, (GROUP_W, GROUP_W), 0)
    dj = lax.broadcasted_iota(jnp.int32, (GROUP_W, GROUP_W), 1)
    state_mask = (di // HEAD) == (dj // HEAD)

    def bd(x, mask):
        if C % 16 == 0:
            return jnp.concatenate([x.astype(BF16)] * GROUP_HEADS, axis=0) * mask
        return (jnp.concatenate([x] * GROUP_HEADS, axis=0) * mask.astype(F32)).astype(BF16)

    def dot_cat(lhs, rhs):
        if all(a.shape[1] % LANES == 0 for a in lhs):
            return _dot(jnp.concatenate(lhs, axis=1), jnp.concatenate(rhs, axis=0))
        return sum(_dot(a, b) for a, b in zip(lhs, rhs))

    n_levels = int(math.log2(C))

    ops = {}
    for b in range(NB):
        for ch in range(CH):
            for g in range(N_GROUPS):
                r0 = (b * CH + ch) * C
                rs = slice(r0, r0 + C)
                ls = slice(g * GROUP_W, (g + 1) * GROUP_W)
                g_c = ew_incl[r0 + C - 1:r0 + C, ls]
                ops[b, ch, g] = dict(
                    a=a_t[rs, ls].astype(BF16), r=r_t[rs, ls].astype(BF16),
                    bk_hat=jnp.concatenate([b_t[rs, ls] * g_c, k_t[rs, ls] * g_c], axis=0),
                    b_bd=bd(b_t[rs, ls], head_mask), k_bd=bd(k_t[rs, ls], head_mask),
                    v=v[rs, ls], v_bd=bd(v[rs, ls], head_mask), g_c=g_c)
    for o in ops.values():
        ar = jnp.concatenate([o["a"], o["r"]], axis=0)
        nb = _dot_nt(ar, o["b_bd"])
        nk = _dot_nt(ar, o["k_bd"])
        o["n_ab"] = jnp.where(strict, nb[0:C], 0.0)
        o["m_rb"] = jnp.where(incl, nb[C:2 * C], 0.0)
        o["n_ak"] = jnp.where(strict, nk[0:C], 0.0)
        o["m_rk"] = jnp.where(incl, nk[C:2 * C], 0.0)
    for o in ops.values():
        o["t_inv"] = eye_w + o["n_ab"]
        o["p"] = _dot(o["n_ab"], bd(o["n_ab"], wide_mask))
    for j in range(1, n_levels - 1):
        for o in ops.values():
            both = _dot(jnp.concatenate([o["p"], o["t_inv"]], axis=0), bd(o["p"], wide_mask))
            o["p"] = both[0:C]
            o["t_inv"] = o["t_inv"] + both[C:2 * C]
    for o in ops.values():
        o["t_inv"] = o["t_inv"] + _dot(o["t_inv"], bd(o["p"], wide_mask))

    for ch in range(CH):
        now = [(b, g, ops[b, ch, g]) for b in range(NB) for g in range(N_GROUPS)]
        for b, g, o in now:
            o["s_prev"] = s_scr[b, g]
            o["s_bf"] = o["s_prev"].astype(BF16)
            o["rhs"] = dot_cat([o["a"], o["n_ak"]], [o["s_bf"], o["v_bd"]])
        for b, g, o in now:
            o["u"] = _dot(o["t_inv"], bd(o["rhs"], head_mask))
        for b, g, o in now:
            o["y"] = dot_cat([o["r"], o["m_rb"], o["m_rk"]],
                             [o["s_bf"], bd(o["u"], head_mask), o["v_bd"]])
        for b, g, o in now:
            g_col = jnp.transpose(jnp.broadcast_to(o["g_c"], (GROUP_W, GROUP_W)))
            upd = _dot_tn(o["bk_hat"], jnp.concatenate([o["u"], o["v"]], axis=0))
            s_scr[b, g] = o["s_prev"] * g_col + jnp.where(state_mask, upd, 0.0)

    y = jnp.concatenate(
        [jnp.concatenate([ops[b, ch, g]["y"] for g in range(N_GROUPS)], axis=-1)
         for b in range(NB) for ch in range(CH)], axis=0)
    inv_n = 1.0 / HEAD
    mean = head_sum(y) * inv_n
    d = y - mean
    var = head_sum(d * d) * inv_n
    yn = d * lax.rsqrt(var + GN_EPS) * lnw_ref[...] + lnb_ref[...]
    bonus = head_sum(r * k2 * rk_ref[...]) * v
    gate = gr_ref[...].reshape(NS * C, RWKV_WIDTH)
    out = (yn + bonus) * (gate * _sigmoid(gate))
    y_ref[...] = out.reshape(NB, CH * C, RWKV_WIDTH).astype(y_ref.dtype)

    @pl.when(c_idx == last)
    def _():
        for b in range(NB):
            shift_ref[b] = last_rows[b]
            for g in range(N_GROUPS):
                s_bd = s_scr[b, g]
                s_t = s_bd[0:HEAD]
                for h in range(1, GROUP_HEADS):
                    s_t = s_t + s_bd[h * HEAD:(h + 1) * HEAD]
                heads = exact_nt(eye_g, s_t)
                sout_ref[b, g * GROUP_HEADS:(g + 1) * GROUP_HEADS] = heads.reshape(
                    GROUP_HEADS, HEAD, HEAD)


def _rwkv(zsh, gr, shift_prev, s0, p, chunk, t_valid, seqs_per_step, chunks_per_step):
    b, t, _ = zsh.shape
    step_tokens = chunk * chunks_per_step
    nc = t // step_tokens
    nb = seqs_per_step
    tok = lambda width: pl.BlockSpec((nb, step_tokens, width), lambda i, c: (i, c, 0))
    per_b = lambda shape: pl.BlockSpec((nb,) + shape, lambda i, c: (i,) + (0,) * len(shape))
    const = lambda a: pl.BlockSpec(a.shape, lambda i, c: (0,) * a.ndim)
    params = (p["mu"], p["w0"], p["w1"], p["a0"], p["w2"], p["k_k"], p["k_a"], p["r_k"],
              p["lnx_w"], p["lnx_b"], p["e_ones"])
    return pl.pallas_call(
        functools.partial(_rwkv_kernel, chunk=chunk, t_valid=t_valid),
        grid=(b // nb, nc),
        in_specs=[tok(SHIFT_WIDTH), tok(512), per_b((1, SHIFT_WIDTH)),
                  per_b((RWKV_HEADS, HEAD, HEAD))] + [const(a) for a in params],
        out_specs=(tok(512), per_b((RWKV_HEADS, HEAD, HEAD)), per_b((1, SHIFT_WIDTH))),
        out_shape=(jax.ShapeDtypeStruct((b, t, 512), BF16),
                   jax.ShapeDtypeStruct((b, RWKV_HEADS, HEAD, HEAD), F32),
                   jax.ShapeDtypeStruct((b, 1, SHIFT_WIDTH), F32)),
        scratch_shapes=[pltpu.VMEM((nb, N_GROUPS, GROUP_W, GROUP_W), F32),
                        pltpu.VMEM((nb, 1, SHIFT_WIDTH), F32)],
        compiler_params=pltpu.CompilerParams(dimension_semantics=("arbitrary", "arbitrary"),
                                             vmem_limit_bytes=VMEM_LIMIT_V7X),
        name="rwkv7",
    )(zsh, gr, shift_prev, s0, *params)


def _lambda(lq1_ref, lk1_ref, lq2_ref, lk2_ref, lam_init):
    s1 = jnp.sum(lq1_ref[...] * lk1_ref[...], axis=-1, keepdims=True)
    s2 = jnp.sum(lq2_ref[...] * lk2_ref[...], axis=-1, keepdims=True)
    return jnp.exp(s1) - jnp.exp(s2) + lam_init


def _diff_epilogue(acc0, l0, acc1, l1, lam, lam_init, subln, gate):
    o = acc0 / l0 - lam * (acc1 / l1)
    ms = jnp.mean(o * o, axis=-1, keepdims=True)
    on = o * lax.rsqrt(ms + SUBLN_EPS) * subln * (1.0 - lam_init)
    return on * (gate * _sigmoid(gate))


def _attn_kernel(rb_ref, li_ref, qt_ref, k_ref, vt_ref, gd_ref, lq1_ref, lk1_ref, lq2_ref,
                 lk2_ref, sub_ref, o_ref, bias_scr, s_scr, acc_scr, *, tq, tk):
    h = pl.program_id(1)
    qi = pl.program_id(2)
    nsub = tq // tk
    first_near = nsub * qi - 1

    @pl.when(qi == 0)
    def _():
        key = lax.broadcasted_iota(jnp.int32, (tk, tq), 0)
        qry = lax.broadcasted_iota(jnp.int32, (tk, tq), 1)
        for i in range(nsub + 1):
            dist = qry - key - (i - 1) * tk
            bias = _rel_bias_minus_far(dist, rb_ref, h) * LOG2E
            bias = jnp.where(dist >= 0, bias, NEG_INF)
            bias_scr[i] = jnp.concatenate([bias, bias], axis=1)

    qt = qt_ref[...]
    feat = lax.broadcasted_iota(jnp.int32, (2 * HEAD, tq), 0)
    qt2 = jnp.concatenate([jnp.where(feat < HEAD, qt, jnp.zeros_like(qt)),
                           jnp.where(feat >= HEAD, qt, jnp.zeros_like(qt))], axis=1)

    acc_scr[...] = jnp.zeros(acc_scr.shape, F32)
    ones_rows = jnp.ones((ONES_ROWS, tk), BF16)

    def logits(kj):
        off = pl.multiple_of(kj * tk, tk)
        return jnp.dot(k_ref[pl.ds(off, tk), :], qt2, preferred_element_type=F32)

    def absorb(s, kj, m_old):
        off = pl.multiple_of(kj * tk, tk)
        vt_ext = jnp.concatenate([vt_ref[:, pl.ds(off, tk)], ones_rows], axis=0)
        m_new = jnp.maximum(m_old, jnp.max(s, axis=0, keepdims=True))
        pm = jnp.exp2(s - m_new).astype(BF16)
        acc_scr[...] = (jnp.exp2(m_old - m_new) * acc_scr[...]
                        + jnp.dot(vt_ext, pm, preferred_element_type=F32))
        return m_new

    assert nsub == 2
    m_run = jnp.full((1, 2 * tq), NEG_INF, F32)

    @pl.when(qi >= 1)
    def _():
        s_scr[0] = logits(0)

    def far_pair(i, m):
        s_scr[1] = logits(2 * i + 1)
        m = absorb(s_scr[0], 2 * i, m)
        s_scr[0] = logits(2 * i + 2)
        return absorb(s_scr[1], 2 * i + 1, m)

    m_run = lax.fori_loop(0, qi - 1, far_pair, m_run)

    def finish_below_diagonal(m):
        s_scr[1] = logits(first_near)
        m = absorb(s_scr[0], first_near - 1, m)
        s_scr[0] = logits(first_near + 1)
        m = absorb(s_scr[1] + bias_scr[0], first_near, m)
        s_scr[1] = logits(first_near + 2)
        m = absorb(s_scr[0] + bias_scr[1], first_near + 1, m)
        return absorb(s_scr[1] + bias_scr[2], first_near + 2, m)

    def first_query_tile(m):
        s_scr[0] = logits(0)
        s_scr[1] = logits(1)
        m = absorb(s_scr[0] + bias_scr[1], 0, m)
        return absorb(s_scr[1] + bias_scr[2], 1, m)

    lax.cond(qi >= 1, finish_below_diagonal, first_query_tile, m_run)

    lam_init = li_ref[0]
    lam = _lambda(lq1_ref, lk1_ref, lq2_ref, lk2_ref, lam_init)
    l_fin = acc_scr[DIFF_V:DIFF_V + 1, :]
    o_t = (acc_scr[0:DIFF_V, 0:tq] / l_fin[:, 0:tq]
           - lam * (acc_scr[0:DIFF_V, tq:2 * tq] / l_fin[:, tq:2 * tq]))
    ms = jnp.mean(o_t * o_t, axis=0, keepdims=True)
    o = jnp.transpose(o_t * lax.rsqrt(ms + SUBLN_EPS))
    gate = gd_ref[...]
    out = o * sub_ref[...] * (1.0 - lam_init) * (gate * _sigmoid(gate))
    o_ref[...] = out.astype(o_ref.dtype)


def _attn_prompt(qt, kb, vt, gd, rel_bias, lam_init, lam_vecs, subln, batch, tq, tk):
    m = kb.shape[0]
    t = m // batch
    nq = t // tq
    assert tk >= MAX_DISTANCE and tq % tk == 0
    tile = pl.BlockSpec((tq, DIFF_V), lambda b, h, i: (b * nq + i, h))
    tile_t = pl.BlockSpec((DIFF_V, tq), lambda b, h, i: (h, b * nq + i))
    seq = pl.BlockSpec((t, DIFF_V), lambda b, h, i: (b, h))
    seq_t = pl.BlockSpec((DIFF_V, t), lambda b, h, i: (h, b))
    vec = lambda a: pl.BlockSpec(a.shape, lambda b, h, i: (0,) * a.ndim)
    smem = pl.BlockSpec(memory_space=pltpu.SMEM)
    return pl.pallas_call(
        functools.partial(_attn_kernel, tq=tq, tk=tk),
        grid=(batch, DIFF_HEADS, nq),
        in_specs=([smem, smem, tile_t, seq, seq_t, tile] + [vec(a) for a in lam_vecs]
                  + [vec(subln)]),
        out_specs=tile,
        out_shape=jax.ShapeDtypeStruct((m, DIFF_WIDTH), BF16),
        scratch_shapes=[pltpu.VMEM((tq // tk + 1, tk, 2 * tq), F32),
                        pltpu.VMEM((2, tk, 2 * tq), F32),
                        pltpu.VMEM((DIFF_V + ONES_ROWS, 2 * tq), F32)],
        compiler_params=pltpu.CompilerParams(
            dimension_semantics=("arbitrary", "arbitrary", "arbitrary"),
            vmem_limit_bytes=VMEM_LIMIT_V7X),
        name="diff_attn_prompt",
    )(rel_bias, lam_init, qt, kb, vt, gd, *lam_vecs, subln)


TOKENS_PADDED = 8
SCORE_ROWS = DIFF_HEADS * 2 * TOKENS_PADDED
PAGE_ROWS = PAGE * DIFF_HEADS
PAGES_PER_SOFTMAX_STEP = 2


def _decode_kernel(pt_ref, rb_ref, li_ref, q_ref, kn_ref, vn_ref, gd_ref, lq1_ref, lk1_ref,
                   lq2_ref, lk2_ref, sub_ref, *rest, pages_per_step):
    P = pages_per_step
    k_refs = rest[0:P]
    v_refs = rest[P:2 * P]
    o_ref = rest[2 * P]
    m_scr, acc_scr = rest[2 * P + 1:]
    del pt_ref
    j = pl.program_id(1)
    last = pl.num_programs(1) - 1
    TP = TOKENS_PADDED
    NR = SCORE_ROWS

    @pl.when(j == 0)
    def _():
        m_scr[...] = jnp.full(m_scr.shape, NEG_INF, F32)
        acc_scr[...] = jnp.zeros(acc_scr.shape, F32)

    q = q_ref[...]
    lane = lax.broadcasted_iota(jnp.int32, (TP, DIFF_V), 1)
    q_rows = jnp.concatenate(
        [jnp.where((lane // HEAD) == c, q[:, h * DIFF_V:(h + 1) * DIFF_V], 0.0)
         for h in range(DIFF_HEADS) for c in range(2)], axis=0)
    q_rows = q_rows.astype(BF16).astype(F32)
    row = lax.broadcasted_iota(jnp.int32, (NR, 1), 0)
    row_head = row // (2 * TP)
    row_tok = row % TP

    def scores(k_rows, bias=None, causal=False):
        n = k_rows.shape[0]
        s = lax.dot_general(q_rows, k_rows, (((1,), (1,)), ((), ())),
                            preferred_element_type=F32)
        col = lax.broadcasted_iota(jnp.int32, (1, n), 1)
        keep = (col % DIFF_HEADS) == row_head
        if causal:
            keep = keep & (row_tok >= col // DIFF_HEADS)
        if bias is not None:
            s = s + bias
        return jnp.where(keep, s, NEG_INF)

    def row_bias(n, dist_of):
        col_tok = lax.broadcasted_iota(jnp.int32, (1, n), 1) // DIFF_HEADS
        t_row = lax.broadcasted_iota(jnp.int32, (2 * TP, 1), 0) % TP
        dist = dist_of(t_row, col_tok)
        return jnp.concatenate([_rel_bias_minus_far(dist, rb_ref, h) for h in range(DIFF_HEADS)],
                               axis=0) * LOG2E

    def update(s_list, v_list):
        m_old = m_scr[...]
        m_new = m_old
        for s in s_list:
            m_new = jnp.maximum(m_new, jnp.max(s, axis=-1, keepdims=True))
        acc = jnp.exp2(m_old - m_new) * acc_scr[...]
        for s, v in zip(s_list, v_list):
            v_ext = jnp.concatenate([v, jnp.ones(v.shape, F32)], axis=-1)
            acc = acc + jnp.dot(jnp.exp2(s - m_new), v_ext, preferred_element_type=F32)
        acc_scr[...] = acc
        m_scr[...] = m_new

    page_k = lambda i: k_refs[i][...]
    page_v = lambda i: v_refs[i][...]
    G = PAGES_PER_SOFTMAX_STEP

    @pl.when(j != last)
    def _():
        for i0 in range(0, P, G):
            update([scores(page_k(i)) for i in range(i0, i0 + G)],
                   [page_v(i) for i in range(i0, i0 + G)])

    @pl.when(j == last)
    def _():
        bias_p = row_bias(PAGE_ROWS, lambda t_row, col_tok: (PAGE + t_row) - col_tok)
        n_new = TP * DIFF_HEADS
        bias_n = row_bias(n_new, lambda t_row, col_tok: t_row - col_tok)
        for i0 in range(0, P - G, G):
            update([scores(page_k(i)) for i in range(i0, i0 + G)],
                   [page_v(i) for i in range(i0, i0 + G)])
        s_list = [scores(page_k(i)) for i in range(P - G, P - 1)]
        s_list.append(scores(page_k(P - 1), bias=bias_p))
        s_list.append(scores(kn_ref[...], bias=bias_n, causal=True))
        update(s_list, [page_v(i) for i in range(P - G, P)] + [vn_ref[...]])

        lam_init = li_ref[0]
        lam = _lambda(lq1_ref, lk1_ref, lq2_ref, lk2_ref, lam_init)
        gate = gd_ref[...]
        outs = []
        for h in range(DIFF_HEADS):
            r0 = h * 2 * TP
            r1 = r0 + TP
            outs.append(_diff_epilogue(
                acc_scr[r0:r0 + TP, 0:DIFF_V], acc_scr[r0:r0 + TP, DIFF_V:DIFF_V + 1],
                acc_scr[r1:r1 + TP, 0:DIFF_V], acc_scr[r1:r1 + TP, DIFF_V:DIFF_V + 1],
                lam, lam_init, sub_ref[...], gate[:, h * DIFF_V:(h + 1) * DIFF_V]))
        o_ref[...] = jnp.concatenate(outs, axis=-1).astype(o_ref.dtype)


def _attn_sample(q, kn, vn, gd, cache_k, cache_v, page_ids, rel_bias, lam_init, lam_vecs, subln,
                 pages_per_step):
    b = q.shape[0]
    n_pages = page_ids.shape[0] // b
    P = pages_per_step
    per_b = lambda shape: pl.BlockSpec((None,) + shape, lambda i, j, pt: (i, 0, 0))
    vec = lambda a: pl.BlockSpec(a.shape, lambda i, j, pt: (0,) * a.ndim)
    smem = pl.BlockSpec(memory_space=pltpu.SMEM)
    tok = per_b((TOKENS_PADDED, DIFF_WIDTH))
    new = per_b((TOKENS_PADDED * DIFF_HEADS, DIFF_V))

    def page_spec(slot):
        return pl.BlockSpec((None, PAGE_ROWS, DIFF_V),
                            lambda i, j, pt: (pt[i * n_pages + j * P + slot], 0, 0))

    grid_spec = pltpu.PrefetchScalarGridSpec(
        num_scalar_prefetch=1,
        grid=(b, n_pages // P),
        in_specs=([smem, smem, tok, new, new, tok] + [vec(a) for a in lam_vecs]
                  + [vec(subln)] + [page_spec(s) for s in range(P)]
                  + [page_spec(s) for s in range(P)]),
        out_specs=tok,
        scratch_shapes=[pltpu.VMEM((SCORE_ROWS, 1), F32),
                        pltpu.VMEM((SCORE_ROWS, 2 * DIFF_V), F32)],
    )
    return pl.pallas_call(
        functools.partial(_decode_kernel, pages_per_step=P),
        grid_spec=grid_spec,
        out_shape=jax.ShapeDtypeStruct((b, TOKENS_PADDED, DIFF_WIDTH), BF16),
        compiler_params=pltpu.CompilerParams(dimension_semantics=("arbitrary", "arbitrary"),
                                             vmem_limit_bytes=VMEM_LIMIT_V7X),
        name="diff_attn_sample",
    )(page_ids, rel_bias, lam_init, q, kn, vn, gd, *lam_vecs, subln,
      *([cache_k] * P), *([cache_v] * P))


def _layer_params(l, pre_norm, post_norm, w_in, mu_shift, w0, w_up, a0, a_up, k_k, k_a, r_k,
                  lnx_w, lnx_b, lam_q1, lam_k1, lam_q2, lam_k2, subln, w_out):
    row = lambda a: a.reshape(1, -1)
    zeros = jnp.zeros((LORA, RWKV_WIDTH), F32)
    head_of = jnp.arange(GROUP_W) // HEAD
    return dict(
        pre_norm=row(pre_norm[l]), post_norm=row(post_norm[l]),
        w_in=w_in[l].astype(BF16), w_out=w_out[l].astype(BF16),
        mu=row(mu_shift[l]), w0=row(w0[l]), a0=row(a0[l]),
        w1=jnp.concatenate([w_up[l], zeros], axis=0).astype(BF16),
        w2=jnp.concatenate([zeros, a_up[l]], axis=0).astype(BF16),
        k_k=row(k_k[l]), k_a=row(k_a[l]), r_k=row(r_k[l]),
        lnx_w=row(lnx_w[l]), lnx_b=row(lnx_b[l]),
        e_ones=(head_of[:, None] == head_of[None, :]).astype(BF16),
        lam_vecs=(row(lam_q1[l]), row(lam_k1[l]), row(lam_q2[l]), row(lam_k2[l])),
        subln=row(subln[l]),
        lam_init=jnp.full((1,), 0.8 - 0.6 * math.exp(-0.3 * l), F32),
    )


def _mixer(x2d, batch, p, shift_prev, s0, project, attend, tm_out, chunk, t_valid,
           seqs_per_step, chunks_per_step=1):
    m = x2d.shape[0]
    t = m // batch
    zsh, gr, gd, attn_operands = project(x2d)
    y_r, s_bd, shift = _rwkv(zsh.reshape(batch, t, SHIFT_WIDTH), gr.reshape(batch, t, 512),
                             shift_prev, s0, p, chunk, t_valid, seqs_per_step, chunks_per_step)
    y_d = attend(gd, *attn_operands)
    x_new = _outproj(y_r.reshape(m, RWKV_WIDTH), y_d.reshape(m, DIFF_WIDTH), p["w_out"],
                     p["post_norm"], x2d, tm_out)
    return x_new, shift, s_bd, attn_operands


def kernel(x_prompt, x_sample, cache_k, cache_v, state_wkv, state_shift, page_table, rel_bias,
           pre_norm, post_norm, w_in, mu_shift, w0, w_up, a0, a_up, k_k, k_a, r_k, lnx_w, lnx_b,
           lam_q1, lam_k1, lam_q2, lam_k2, subln, w_out):
    return _forward(x_prompt, x_sample, cache_k, cache_v, state_wkv, state_shift, page_table,
                    rel_bias, pre_norm, post_norm, w_in, mu_shift, w0, w_up, a0, a_up, k_k, k_a,
                    r_k, lnx_w, lnx_b, lam_q1, lam_k1, lam_q2, lam_k2, subln, w_out,
                    PROMPT_TM=512, PROMPT_TM_OUT=1024, PROMPT_CHUNK=64, PROMPT_TQ=512,
                    PROMPT_TK=256, PAGES_PER_STEP=32, SAMPLE_SEQS_PER_STEP=8,
                    PROMPT_CHUNKS_PER_STEP=2)


def _forward(x_prompt, x_sample, cache_k, cache_v, state_wkv, state_shift, page_table, rel_bias,
             pre_norm, post_norm, w_in, mu_shift, w0, w_up, a0, a_up, k_k, k_a, r_k, lnx_w, lnx_b,
             lam_q1, lam_k1, lam_q2, lam_k2, subln, w_out, *, PROMPT_TM, PROMPT_TM_OUT,
             PROMPT_CHUNK, PROMPT_TQ, PROMPT_TK, PAGES_PER_STEP, SAMPLE_SEQS_PER_STEP,
             PROMPT_CHUNKS_PER_STEP):
    bp, tp, d = x_prompt.shape
    bs, ts, _ = x_sample.shape
    depth = w_in.shape[0]
    n_pool = cache_k.shape[1]
    n_pages = page_table.shape[1]
    TS_PAD = TOKENS_PADDED

    xp = x_prompt.reshape(bp * tp, d)
    xs = jnp.pad(x_sample, ((0, 0), (0, TS_PAD - ts), (0, 0))).reshape(bs * TS_PAD, d)
    ck = cache_k.reshape(depth * n_pool, PAGE_ROWS, DIFF_V)
    cv = cache_v.reshape(depth * n_pool, PAGE_ROWS, DIFF_V)
    zero_shift = jnp.zeros((bp, 1, SHIFT_WIDTH), F32)
    zero_state = jnp.zeros((bp, RWKV_HEADS, HEAD, HEAD), F32)

    outs = {name: [] for name in ("wp", "sp", "ks", "vs", "ws", "ss")}
    kv_rows = None
    for l in range(depth):
        p = _layer_params(l, pre_norm, post_norm, w_in, mu_shift, w0, w_up, a0, a_up, k_k, k_a,
                          r_k, lnx_w, lnx_b, lam_q1, lam_k1, lam_q2, lam_k2, subln, w_out)

        def project_prompt(x2d):
            zsh, gr, qt, k_rows, v_rows, gd, kb, vt = _inproj_prompt(
                x2d, p["pre_norm"], p["w_in"], PROMPT_TM, l, depth, kv_rows)
            return zsh, gr, gd, (qt, kb, vt, k_rows, v_rows)

        def attend_prompt(gd, qt, kb, vt, k_rows, v_rows):
            return _attn_prompt(qt, kb, vt, gd, rel_bias, p["lam_init"], p["lam_vecs"],
                                p["subln"], bp, PROMPT_TQ, PROMPT_TK)

        def project_sample(x2d):
            zsh, gr, q, k, v, gd = _inproj_sample(x2d, p["pre_norm"], p["w_in"], bs * TS_PAD)
            return zsh, gr, gd, (q, k, v)

        page_ids = (page_table + l * n_pool).reshape(-1).astype(jnp.int32)

        def attend_sample(gd, q, k, v):
            r3 = lambda a: a.reshape(bs, TS_PAD, DIFF_WIDTH)
            rows = lambda a: a.reshape(bs, TS_PAD * DIFF_HEADS, DIFF_V)
            return _attn_sample(r3(q), rows(k), rows(v), r3(gd), ck, cv, page_ids, rel_bias,
                                p["lam_init"], p["lam_vecs"], p["subln"], PAGES_PER_STEP)

        xp, sp, wp, prompt_ops = _mixer(xp, bp, p, zero_shift, zero_state, project_prompt,
                                        attend_prompt, PROMPT_TM_OUT, PROMPT_CHUNK, PROMPT_CHUNK,
                                        bp, PROMPT_CHUNKS_PER_STEP)
        kv_rows = prompt_ops[3:5]
        xs, ss, ws, (_, k_s, v_s) = _mixer(xs, bs, p, state_shift[l][:, None, :],
                                           state_wkv[l], project_sample,
                                           attend_sample, bs * TS_PAD, TS_PAD, ts,
                                           SAMPLE_SEQS_PER_STEP)
        outs["wp"].append(wp)
        outs["sp"].append(sp.reshape(bp, SHIFT_WIDTH))
        outs["ks"].append(k_s.reshape(bs, TS_PAD, DIFF_HEADS, DIFF_V)[:, :ts])
        outs["vs"].append(v_s.reshape(bs, TS_PAD, DIFF_HEADS, DIFF_V)[:, :ts])
        outs["ws"].append(ws)
        outs["ss"].append(ss.reshape(bs, SHIFT_WIDTH))

    st = lambda name: jnp.stack(outs[name])
    kv_shape = (depth, bp, tp, DIFF_HEADS, DIFF_V)
    return (xp.reshape(bp, tp, d), xs.reshape(bs, TS_PAD, d)[:, :ts], kv_rows[0].reshape(kv_shape),
            kv_rows[1].reshape(kv_shape), st("wp"), st("sp"), st("ks"), st("vs"), st("ws"),
            st("ss"))
```

```python
import functools
import math

import jax
import jax.numpy as jnp
from jax import lax
from jax.experimental import pallas as pl
from jax.experimental.pallas import tpu as pltpu

F32 = jnp.float32
BF16 = jnp.bfloat16

RWKV_WIDTH = 512
DIFF_WIDTH = 512
HEAD = 64
RWKV_HEADS = RWKV_WIDTH // HEAD
LORA = 64
DIFF_HEADS = 4
DIFF_V = 128
NUM_BUCKETS = 32
MAX_EXACT = NUM_BUCKETS // 2
MAX_DISTANCE = 128
PAGE = 128
RMS_EPS = 1e-6
SUBLN_EPS = 1e-5
GN_EPS = 64e-5
NEG_INF = -1e30
LOG2E = 1.4426950408889634
SHIFT_WIDTH = 3 * RWKV_WIDTH + 2 * LORA
SEG_SHIFT = (0, SHIFT_WIDTH)
SEG_GR = (SHIFT_WIDTH, SHIFT_WIDTH + 512)
SEG_Q = (SEG_GR[1], SEG_GR[1] + 512)
SEG_K = (SEG_Q[1], SEG_Q[1] + 512)
SEG_V = (SEG_K[1], SEG_K[1] + 512)
SEG_GD = (SEG_V[1], SEG_V[1] + 512)

VMEM_LIMIT_V7X = 48 * 1024 * 1024
LANES = 128
ONES_ROWS = 16
GROUP_HEADS = 4
GROUP_W = GROUP_HEADS * HEAD
N_GROUPS = RWKV_HEADS // GROUP_HEADS


def _dot(a, b):
    return jnp.dot(a.astype(BF16), b.astype(BF16), preferred_element_type=F32)


def _dot_nt(a, b):
    return lax.dot_general(a.astype(BF16), b.astype(BF16), (((1,), (1,)), ((), ())),
                           preferred_element_type=F32)


def _dot_tn(a, b):
    return lax.dot_general(a.astype(BF16), b.astype(BF16), (((0,), (0,)), ((), ())),
                           preferred_element_type=F32)


def _split(x):
    hi = x.astype(BF16)
    lo = (x - hi.astype(F32)).astype(BF16)
    return hi, lo


def _dot_exact_lhs(a_exact, b):
    hi, lo = _split(b)
    return (jnp.dot(a_exact, hi, preferred_element_type=F32)
            + jnp.dot(a_exact, lo, preferred_element_type=F32))


def _sigmoid(x):
    return 1.0 / (1.0 + jnp.exp(-x))


def _rel_bias_minus_far(dist, rb_ref, head):
    n = jnp.maximum(dist, 0)
    nf = jnp.maximum(n, 1).astype(F32)
    large = MAX_EXACT + (jnp.log(nf / MAX_EXACT) / math.log(MAX_DISTANCE / MAX_EXACT)
                         * (NUM_BUCKETS - MAX_EXACT)).astype(jnp.int32)
    large = jnp.minimum(large, NUM_BUCKETS - 1)
    bucket = jnp.where(n < MAX_EXACT, n, large)
    far = rb_ref[NUM_BUCKETS - 1, head]
    out = jnp.zeros(dist.shape, F32)
    for b in range(NUM_BUCKETS - 1):
        out = jnp.where(bucket == b, rb_ref[b, head] - far, out)
    return out


def _projected_segments(x_ref, g_ref, w_ref):
    x = x_ref[...]
    ms = jnp.mean(x * x, axis=-1, keepdims=True)
    h = (x * lax.rsqrt(ms + RMS_EPS) * g_ref[...]).astype(BF16)

    def seg(lo_hi):
        lo, hi = lo_hi
        return jnp.dot(h, w_ref[:, lo:hi], preferred_element_type=F32)

    return seg


Q_SCALE = LOG2E * HEAD ** -0.5


def _inproj_sample_kernel(x_ref, g_ref, w_ref, zsh_ref, gr_ref, q_ref, k_ref, v_ref, gd_ref):
    seg = _projected_segments(x_ref, g_ref, w_ref)
    zsh_ref[...] = seg(SEG_SHIFT)
    gr_ref[...] = seg(SEG_GR)
    q_ref[...] = seg(SEG_Q) * Q_SCALE
    k_ref[...] = seg(SEG_K)
    v_ref[...] = seg(SEG_V)
    gd_ref[...] = seg(SEG_GD)


def _inproj_prompt_kernel(x_ref, g_ref, w_ref, *refs):
    zsh_ref, gr_ref, qt_ref, krows_ref, vrows_ref, gd_ref, kb_ref, vt_ref = refs[-8:]
    seg = _projected_segments(x_ref, g_ref, w_ref)
    tm = x_ref.shape[0]
    zsh_ref[...] = seg(SEG_SHIFT)
    gr_ref[...] = seg(SEG_GR)
    qt_ref[...] = jnp.transpose(seg(SEG_Q) * Q_SCALE).astype(BF16)
    gd_ref[...] = seg(SEG_GD)
    k = seg(SEG_K)
    v = seg(SEG_V)
    kb_ref[...] = k.astype(BF16)
    vt_ref[...] = jnp.transpose(v).astype(BF16)
    for hd in range(DIFF_HEADS):
        cols = slice(hd * DIFF_V, (hd + 1) * DIFF_V)
        krows_ref[pl.ds(hd, tm, stride=DIFF_HEADS), :] = k[:, cols]
        vrows_ref[pl.ds(hd, tm, stride=DIFF_HEADS), :] = v[:, cols]


def _inproj_sample(x, g, w_bf16, tm):
    m, d = x.shape
    row = lambda width: pl.BlockSpec((tm, width), lambda i: (i, 0))
    f32 = jax.ShapeDtypeStruct((m, 512), F32)
    return pl.pallas_call(
        _inproj_sample_kernel,
        grid=(m // tm,),
        in_specs=[row(d), pl.BlockSpec((1, d), lambda i: (0, 0)),
                  pl.BlockSpec(w_bf16.shape, lambda i: (0, 0))],
        out_specs=(row(SHIFT_WIDTH), row(512), row(512), row(512), row(512), row(512)),
        out_shape=(jax.ShapeDtypeStruct((m, SHIFT_WIDTH), F32), f32, f32, f32, f32, f32),
        compiler_params=pltpu.CompilerParams(dimension_semantics=("arbitrary",),
                                             vmem_limit_bytes=VMEM_LIMIT_V7X),
        name="inproj_sample",
    )(x, g, w_bf16)


def _inproj_prompt(x, g, w_bf16, tm, layer, depth, kv_rows):
    m, d = x.shape
    steps = m // tm
    row = lambda width: pl.BlockSpec((tm, width), lambda i: (i, 0))
    col = pl.BlockSpec((512, tm), lambda i: (0, i))
    rows = pl.BlockSpec((tm * DIFF_HEADS, DIFF_V), lambda i: (layer * steps + i, 0))
    f32 = jax.ShapeDtypeStruct((m, 512), F32)
    bf_t = jax.ShapeDtypeStruct((512, m), BF16)
    rows_shape = jax.ShapeDtypeStruct((depth * m * DIFF_HEADS, DIFF_V), F32)
    in_specs = [row(d), pl.BlockSpec((1, d), lambda i: (0, 0)),
                pl.BlockSpec(w_bf16.shape, lambda i: (0, 0), pipeline_mode=pl.Buffered(1))]
    operands = [x, g, w_bf16]
    aliases = {}
    if kv_rows is not None:
        in_specs += [pl.BlockSpec(memory_space=pl.ANY)] * 2
        operands += list(kv_rows)
        aliases = {3: 3, 4: 4}
    return pl.pallas_call(
        _inproj_prompt_kernel,
        grid=(steps,),
        in_specs=in_specs,
        out_specs=(row(SHIFT_WIDTH), row(512), col, rows, rows, row(512), row(512), col),
        out_shape=(jax.ShapeDtypeStruct((m, SHIFT_WIDTH), F32), f32, bf_t, rows_shape, rows_shape,
                   f32, jax.ShapeDtypeStruct((m, 512), BF16), bf_t),
        input_output_aliases=aliases,
        compiler_params=pltpu.CompilerParams(dimension_semantics=("arbitrary",),
                                             vmem_limit_bytes=VMEM_LIMIT_V7X),
        name="inproj_prompt",
    )(*operands)


def _outproj_kernel(yr_ref, yd_ref, w_ref, g_ref, x_ref, o_ref):
    out = (jnp.dot(yr_ref[...], w_ref[0:RWKV_WIDTH, :], preferred_element_type=F32)
           + jnp.dot(yd_ref[...], w_ref[RWKV_WIDTH:, :], preferred_element_type=F32))
    ms = jnp.mean(out * out, axis=-1, keepdims=True)
    o_ref[...] = x_ref[...] + out * lax.rsqrt(ms + RMS_EPS) * g_ref[...]


def _outproj(yr, yd, w_bf16, g, x, tm):
    m, d = x.shape
    row = lambda width: pl.BlockSpec((tm, width), lambda i: (i, 0))
    return pl.pallas_call(
        _outproj_kernel,
        grid=(m // tm,),
        in_specs=[row(512), row(512), pl.BlockSpec(w_bf16.shape, lambda i: (0, 0)),
                  pl.BlockSpec((1, d), lambda i: (0, 0)), row(d)],
        out_specs=row(d),
        out_shape=jax.ShapeDtypeStruct((m, d), F32),
        compiler_params=pltpu.CompilerParams(dimension_semantics=("arbitrary",),
                                             vmem_limit_bytes=VMEM_LIMIT_V7X),
        name="outproj",
    )(yr, yd, w_bf16, g, x)


def _rwkv_kernel(zsh_ref, gr_ref, sprev_ref, s0_ref, mu_ref, w0_ref, w1_ref, a0_ref, w2_ref,
                 kk_ref, ka_ref, rk_ref, lnw_ref, lnb_ref, e_ref,
                 y_ref, sout_ref, shift_ref, s_scr, carry_scr, *, chunk, t_valid):
    C = chunk
    NB = zsh_ref.shape[0]
    R = GROUP_HEADS * C
    c_idx = pl.program_id(1)
    last = pl.num_programs(1) - 1

    ii = lax.broadcasted_iota(jnp.int32, (GROUP_W, GROUP_W), 0)
    jj = lax.broadcasted_iota(jnp.int32, (GROUP_W, GROUP_W), 1)
    eye_g = jnp.where(ii == jj, 1.0, 0.0).astype(BF16)
    block_g = (ii // HEAD) == (jj // HEAD)

    def exact_nt(lhs_exact, x):
        hi = x.astype(BF16)
        r1 = x - hi.astype(F32)
        mid = r1.astype(BF16)
        lo = (r1 - mid.astype(F32)).astype(BF16)
        return _dot_nt(lhs_exact, hi) + _dot_nt(lhs_exact, mid) + _dot_nt(lhs_exact, lo)

    @pl.when(c_idx == 0)
    def _():
        carry_scr[...] = sprev_ref[...]
        for b in range(NB):
            for g in range(N_GROUPS):
                heads = s0_ref[b, g * GROUP_HEADS:(g + 1) * GROUP_HEADS]
                s_t = exact_nt(eye_g[0:HEAD, 0:HEAD], heads.reshape(GROUP_W, HEAD))
                s_scr[b, g] = jnp.where(block_g, jnp.concatenate([s_t] * GROUP_HEADS, axis=0),
                                        0.0)

    CH = zsh_ref.shape[1] // C
    NS = NB * CH
    assert CH == 1 or t_valid == C
    z = zsh_ref[...].reshape(NS * C, SHIFT_WIDTH)
    row = lax.broadcasted_iota(jnp.int32, (NS * C, 1), 0)
    tok = row % C
    prev = pltpu.roll(z, 1, axis=0)
    last_rows = []
    for b in range(NB):
        prev = jnp.where(row == b * CH * C, carry_scr[b], prev)
        last_tok = b * CH * C + (CH - 1) * C + t_valid - 1
        last_rows.append(z[last_tok:last_tok + 1, :])
        carry_scr[b] = last_rows[b]
    zm = z + mu_ref[...] * (prev - z)

    r = zm[:, 0:512]
    k = zm[:, 512:1024]
    v = zm[:, 1024:1536]
    zwa = zm[:, 1536:1664]
    lora_w = _dot(jnp.tanh(zwa), w1_ref[...])
    lora_a = _dot(zwa, w2_ref[...])
    logw = -math.exp(-0.5) * _sigmoid(w0_ref[...] + lora_w)
    alr = _sigmoid(a0_ref[...] + lora_a)

    e_ones = e_ref[...]

    def head_sum(x):
        return jnp.concatenate(
            [jnp.dot(x[:, g * GROUP_W:(g + 1) * GROUP_W].astype(BF16), e_ones,
                     preferred_element_type=F32) for g in range(N_GROUPS)], axis=1)

    kk = k * kk_ref[...]
    kk = kk * lax.rsqrt(jnp.maximum(head_sum(kk * kk), 1e-24))
    k2 = k * (1.0 + (alr - 1.0) * ka_ref[...])
    a_vec = -kk
    b_vec = kk * alr
    if t_valid < C:
        valid = tok < t_valid
        logw = jnp.where(valid, logw, 0.0)
        b_vec = jnp.where(valid, b_vec, 0.0)
        k2 = jnp.where(valid, k2, 0.0)

    ti = lax.broadcasted_iota(jnp.int32, (NS * C, NS * C), 0)
    si = lax.broadcasted_iota(jnp.int32, (NS * C, NS * C), 1)
    ltri = jnp.where(((ti // C) == (si // C)) & (ti >= si), 1.0, 0.0).astype(BF16)
    cum = _dot_exact_lhs(ltri, logw)
    ew_incl = jnp.exp(cum)
    ew_excl = jnp.exp(cum - logw)
    e_inv = jnp.exp(-cum)
    a_t = a_vec * ew_excl
    r_t = r * ew_incl
    b_t = b_vec * e_inv
    k_t = k2 * e_inv

    ri = lax.broadcasted_iota(jnp.int32, (R, GROUP_W), 0)
    li = lax.broadcasted_iota(jnp.int32, (R, GROUP_W), 1)
    one_zero = lambda m: jnp.where(m, 1.0, 0.0).astype(BF16)
    head_mask = one_zero((ri // C) == (li // HEAD))
    wi = lax.broadcasted_iota(jnp.int32, (R, R), 0)
    wj = lax.broadcasted_iota(jnp.int32, (R, R), 1)
    wide_mask = one_zero((wi // C) == (wj // C))
    tt = lax.broadcasted_iota(jnp.int32, (C, R), 0)
    ss = lax.broadcasted_iota(jnp.int32, (C, R), 1) % C
    strict = tt > ss
    incl = tt >= ss
    eye_w = jnp.where(tt == ss, 1.0, 0.0)
    di = lax.broadcasted_iota(jnp.int32, (GROUP_W, GROUP_W), 0)
    dj = lax.broadcasted_iota(jnp.int32, (GROUP_W, GROUP_W), 1)
    state_mask = (di // HEAD) == (dj // HEAD)

    def bd(x, mask):
        if C % 16 == 0:
            return jnp.concatenate([x.astype(BF16)] * GROUP_HEADS, axis=0) * mask
        return (jnp.concatenate([x] * GROUP_HEADS, axis=0) * mask.astype(F32)).astype(BF16)

    def dot_cat(lhs, rhs):
        if all(a.shape[1] % LANES == 0 for a in lhs):
            return _dot(jnp.concatenate(lhs, axis=1), jnp.concatenate(rhs, axis=0))
        return sum(_dot(a, b) for a, b in zip(lhs, rhs))

    n_levels = int(math.log2(C))

    ops = {}
    for b in range(NB):
        for ch in range(CH):
            for g in range(N_GROUPS):
                r0 = (b * CH + ch) * C
                rs = slice(r0, r0 + C)
                ls = slice(g * GROUP_W, (g + 1) * GROUP_W)
                g_c = ew_incl[r0 + C - 1:r0 + C, ls]
                ops[b, ch, g] = dict(
                    a=a_t[rs, ls].astype(BF16), r=r_t[rs, ls].astype(BF16),
                    bk_hat=jnp.concatenate([b_t[rs, ls] * g_c, k_t[rs, ls] * g_c], axis=0),
                    b_bd=bd(b_t[rs, ls], head_mask), k_bd=bd(k_t[rs, ls], head_mask),
                    v=v[rs, ls], v_bd=bd(v[rs, ls], head_mask), g_c=g_c)
    for o in ops.values():
        ar = jnp.concatenate([o["a"], o["r"]], axis=0)
        nb = _dot_nt(ar, o["b_bd"])
        nk = _dot_nt(ar, o["k_bd"])
        o["n_ab"] = jnp.where(strict, nb[0:C], 0.0)
        o["m_rb"] = jnp.where(incl, nb[C:2 * C], 0.0)
        o["n_ak"] = jnp.where(strict, nk[0:C], 0.0)
        o["m_rk"] = jnp.where(incl, nk[C:2 * C], 0.0)
    for o in ops.values():
        o["t_inv"] = eye_w + o["n_ab"]
        o["p"] = _dot(o["n_ab"], bd(o["n_ab"], wide_mask))
    for j in range(1, n_levels - 1):
        for o in ops.values():
            both = _dot(jnp.concatenate([o["p"], o["t_inv"]], axis=0), bd(o["p"], wide_mask))
            o["p"] = both[0:C]
            o["t_inv"] = o["t_inv"] + both[C:2 * C]
    for o in ops.values():
        o["t_inv"] = o["t_inv"] + _dot(o["t_inv"], bd(o["p"], wide_mask))

    for ch in range(CH):
        now = [(b, g, ops[b, ch, g]) for b in range(NB) for g in range(N_GROUPS)]
        for b, g, o in now:
            o["s_prev"] = s_scr[b, g]
            o["s_bf"] = o["s_prev"].astype(BF16)
            o["rhs"] = dot_cat([o["a"], o["n_ak"]], [o["s_bf"], o["v_bd"]])
        for b, g, o in now:
            o["u"] = _dot(o["t_inv"], bd(o["rhs"], head_mask))
        for b, g, o in now:
            o["y"] = dot_cat([o["r"], o["m_rb"], o["m_rk"]],
                             [o["s_bf"], bd(o["u"], head_mask), o["v_bd"]])
        for b, g, o in now:
            g_col = jnp.transpose(jnp.broadcast_to(o["g_c"], (GROUP_W, GROUP_W)))
            upd = _dot_tn(o["bk_hat"], jnp.concatenate([o["u"], o["v"]], axis=0))
            s_scr[b, g] = o["s_prev"] * g_col + jnp.where(state_mask, upd, 0.0)

    y = jnp.concatenate(
        [jnp.concatenate([ops[b, ch, g]["y"] for g in range(N_GROUPS)], axis=-1)
         for b in range(NB) for ch in range(CH)], axis=0)
    inv_n = 1.0 / HEAD
    mean = head_sum(y) * inv_n
    d = y - mean
    var = head_sum(d * d) * inv_n
    yn = d * lax.rsqrt(var + GN_EPS) * lnw_ref[...] + lnb_ref[...]
    bonus = head_sum(r * k2 * rk_ref[...]) * v
    gate = gr_ref[...].reshape(NS * C, RWKV_WIDTH)
    out = (yn + bonus) * (gate * _sigmoid(gate))
    y_ref[...] = out.reshape(NB, CH * C, RWKV_WIDTH).astype(y_ref.dtype)

    @pl.when(c_idx == last)
    def _():
        for b in range(NB):
            shift_ref[b] = last_rows[b]
            for g in range(N_GROUPS):
                s_bd = s_scr[b, g]
                s_t = s_bd[0:HEAD]
                for h in range(1, GROUP_HEADS):
                    s_t = s_t + s_bd[h * HEAD:(h + 1) * HEAD]
                heads = exact_nt(eye_g, s_t)
                sout_ref[b, g * GROUP_HEADS:(g + 1) * GROUP_HEADS] = heads.reshape(
                    GROUP_HEADS, HEAD, HEAD)


def _rwkv(zsh, gr, shift_prev, s0, p, chunk, t_valid, seqs_per_step, chunks_per_step):
    b, t, _ = zsh.shape
    step_tokens = chunk * chunks_per_step
    nc = t // step_tokens
    nb = seqs_per_step
    tok = lambda width: pl.BlockSpec((nb, step_tokens, width), lambda i, c: (i, c, 0))
    per_b = lambda shape: pl.BlockSpec((nb,) + shape, lambda i, c: (i,) + (0,) * len(shape))
    const = lambda a: pl.BlockSpec(a.shape, lambda i, c: (0,) * a.ndim)
    params = (p["mu"], p["w0"], p["w1"], p["a0"], p["w2"], p["k_k"], p["k_a"], p["r_k"],
              p["lnx_w"], p["lnx_b"], p["e_ones"])
    return pl.pallas_call(
        functools.partial(_rwkv_kernel, chunk=chunk, t_valid=t_valid),
        grid=(b // nb, nc),
        in_specs=[tok(SHIFT_WIDTH), tok(512), per_b((1, SHIFT_WIDTH)),
                  per_b((RWKV_HEADS, HEAD, HEAD))] + [const(a) for a in params],
        out_specs=(tok(512), per_b((RWKV_HEADS, HEAD, HEAD)), per_b((1, SHIFT_WIDTH))),
        out_shape=(jax.ShapeDtypeStruct((b, t, 512), BF16),
                   jax.ShapeDtypeStruct((b, RWKV_HEADS, HEAD, HEAD), F32),
                   jax.ShapeDtypeStruct((b, 1, SHIFT_WIDTH), F32)),
        scratch_shapes=[pltpu.VMEM((nb, N_GROUPS, GROUP_W, GROUP_W), F32),
                        pltpu.VMEM((nb, 1, SHIFT_WIDTH), F32)],
        compiler_params=pltpu.CompilerParams(dimension_semantics=("arbitrary", "arbitrary"),
                                             vmem_limit_bytes=VMEM_LIMIT_V7X),
        name="rwkv7",
    )(zsh, gr, shift_prev, s0, *params)


def _lambda(lq1_ref, lk1_ref, lq2_ref, lk2_ref, lam_init):
    s1 = jnp.sum(lq1_ref[...] * lk1_ref[...], axis=-1, keepdims=True)
    s2 = jnp.sum(lq2_ref[...] * lk2_ref[...], axis=-1, keepdims=True)
    return jnp.exp(s1) - jnp.exp(s2) + lam_init


def _diff_epilogue(acc0, l0, acc1, l1, lam, lam_init, subln, gate):
    o = acc0 / l0 - lam * (acc1 / l1)
    ms = jnp.mean(o * o, axis=-1, keepdims=True)
    on = o * lax.rsqrt(ms + SUBLN_EPS) * subln * (1.0 - lam_init)
    return on * (gate * _sigmoid(gate))


def _prompt_attn_step(h, qi, rb_ref, li_ref, qt_ref, k_ref, vt_ref, gd_ref, lq1_ref, lk1_ref,
                      lq2_ref, lk2_ref, sub_ref, o_ref, bias_scr, s_scr, acc_scr, *, tq, tk):
    nsub = tq // tk
    first_near = nsub * qi - 1

    @pl.when(qi == 0)
    def _():
        key = lax.broadcasted_iota(jnp.int32, (tk, tq), 0)
        qry = lax.broadcasted_iota(jnp.int32, (tk, tq), 1)
        for i in range(nsub + 1):
            dist = qry - key - (i - 1) * tk
            bias = _rel_bias_minus_far(dist, rb_ref, h) * LOG2E
            bias = jnp.where(dist >= 0, bias, NEG_INF)
            bias_scr[i] = jnp.concatenate([bias, bias], axis=1)

    qt = qt_ref[...]
    feat = lax.broadcasted_iota(jnp.int32, (2 * HEAD, tq), 0)
    qt2 = jnp.concatenate([jnp.where(feat < HEAD, qt, jnp.zeros_like(qt)),
                           jnp.where(feat >= HEAD, qt, jnp.zeros_like(qt))], axis=1)

    acc_scr[...] = jnp.zeros(acc_scr.shape, F32)
    ones_rows = jnp.ones((ONES_ROWS, tk), BF16)

    def logits(kj):
        off = pl.multiple_of(kj * tk, tk)
        return jnp.dot(k_ref[pl.ds(off, tk), :], qt2, preferred_element_type=F32)

    def absorb(s, kj, m_old):
        off = pl.multiple_of(kj * tk, tk)
        vt_ext = jnp.concatenate([vt_ref[:, pl.ds(off, tk)], ones_rows], axis=0)
        m_new = jnp.maximum(m_old, jnp.max(s, axis=0, keepdims=True))
        pm = jnp.exp2(s - m_new).astype(BF16)
        acc_scr[...] = (jnp.exp2(m_old - m_new) * acc_scr[...]
                        + jnp.dot(vt_ext, pm, preferred_element_type=F32))
        return m_new

    assert nsub == 2
    m_run = jnp.full((1, 2 * tq), NEG_INF, F32)

    @pl.when(qi >= 1)
    def _():
        s_scr[0] = logits(0)

    def far_pair(i, m):
        s_scr[1] = logits(2 * i + 1)
        m = absorb(s_scr[0], 2 * i, m)
        s_scr[0] = logits(2 * i + 2)
        return absorb(s_scr[1], 2 * i + 1, m)

    m_run = lax.fori_loop(0, qi - 1, far_pair, m_run)

    def finish_below_diagonal(m):
        s_scr[1] = logits(first_near)
        m = absorb(s_scr[0], first_near - 1, m)
        s_scr[0] = logits(first_near + 1)
        m = absorb(s_scr[1] + bias_scr[0], first_near, m)
        s_scr[1] = logits(first_near + 2)
        m = absorb(s_scr[0] + bias_scr[1], first_near + 1, m)
        return absorb(s_scr[1] + bias_scr[2], first_near + 2, m)

    def first_query_tile(m):
        s_scr[0] = logits(0)
        s_scr[1] = logits(1)
        m = absorb(s_scr[0] + bias_scr[1], 0, m)
        return absorb(s_scr[1] + bias_scr[2], 1, m)

    lax.cond(qi >= 1, finish_below_diagonal, first_query_tile, m_run)

    lam_init = li_ref[0]
    lam = _lambda(lq1_ref, lk1_ref, lq2_ref, lk2_ref, lam_init)
    l_fin = acc_scr[DIFF_V:DIFF_V + 1, :]
    o_t = (acc_scr[0:DIFF_V, 0:tq] / l_fin[:, 0:tq]
           - lam * (acc_scr[0:DIFF_V, tq:2 * tq] / l_fin[:, tq:2 * tq]))
    ms = jnp.mean(o_t * o_t, axis=0, keepdims=True)
    o = jnp.transpose(o_t * lax.rsqrt(ms + SUBLN_EPS))
    gate = gd_ref[...]
    out = o * sub_ref[...] * (1.0 - lam_init) * (gate * _sigmoid(gate))
    o_ref[...] = out.astype(o_ref.dtype)


TOKENS_PADDED = 8
SCORE_ROWS = DIFF_HEADS * 2 * TOKENS_PADDED
PAGE_ROWS = PAGE * DIFF_HEADS
PAGES_PER_SOFTMAX_STEP = 2


def _decode_step(j, last, rb_ref, li_ref, q_ref, kn_ref, vn_ref, gd_ref, lq1_ref, lk1_ref,
                 lq2_ref, lk2_ref, sub_ref, k_refs, v_refs, o_ref, m_scr, acc_scr):
    P = len(k_refs)
    TP = TOKENS_PADDED
    NR = SCORE_ROWS

    @pl.when(j == 0)
    def _():
        m_scr[...] = jnp.full(m_scr.shape, NEG_INF, F32)
        acc_scr[...] = jnp.zeros(acc_scr.shape, F32)

    q = q_ref[...]
    lane = lax.broadcasted_iota(jnp.int32, (TP, DIFF_V), 1)
    q_rows = jnp.concatenate(
        [jnp.where((lane // HEAD) == c, q[:, h * DIFF_V:(h + 1) * DIFF_V], 0.0)
         for h in range(DIFF_HEADS) for c in range(2)], axis=0)
    q_rows = q_rows.astype(BF16).astype(F32)
    row = lax.broadcasted_iota(jnp.int32, (NR, 1), 0)
    row_head = row // (2 * TP)
    row_tok = row % TP

    def scores(k_rows, bias=None, causal=False):
        n = k_rows.shape[0]
        s = lax.dot_general(q_rows, k_rows, (((1,), (1,)), ((), ())),
                            preferred_element_type=F32)
        col = lax.broadcasted_iota(jnp.int32, (1, n), 1)
        keep = (col % DIFF_HEADS) == row_head
        if causal:
            keep = keep & (row_tok >= col // DIFF_HEADS)
        if bias is not None:
            s = s + bias
        return jnp.where(keep, s, NEG_INF)

    def row_bias(n, dist_of):
        col_tok = lax.broadcasted_iota(jnp.int32, (1, n), 1) // DIFF_HEADS
        t_row = lax.broadcasted_iota(jnp.int32, (2 * TP, 1), 0) % TP
        dist = dist_of(t_row, col_tok)
        return jnp.concatenate([_rel_bias_minus_far(dist, rb_ref, h) for h in range(DIFF_HEADS)],
                               axis=0) * LOG2E

    def update(s_list, v_list):
        m_old = m_scr[...]
        m_new = m_old
        for s in s_list:
            m_new = jnp.maximum(m_new, jnp.max(s, axis=-1, keepdims=True))
        acc = jnp.exp2(m_old - m_new) * acc_scr[...]
        for s, v in zip(s_list, v_list):
            v_ext = jnp.concatenate([v, jnp.ones(v.shape, F32)], axis=-1)
            acc = acc + jnp.dot(jnp.exp2(s - m_new), v_ext, preferred_element_type=F32)
        acc_scr[...] = acc
        m_scr[...] = m_new

    page_k = lambda i: k_refs[i][...]
    page_v = lambda i: v_refs[i][...]
    G = min(PAGES_PER_SOFTMAX_STEP, P)

    @pl.when(j != last)
    def _():
        for i0 in range(0, P, G):
            update([scores(page_k(i)) for i in range(i0, i0 + G)],
                   [page_v(i) for i in range(i0, i0 + G)])

    @pl.when(j == last)
    def _():
        bias_p = row_bias(PAGE_ROWS, lambda t_row, col_tok: (PAGE + t_row) - col_tok)
        n_new = TP * DIFF_HEADS
        bias_n = row_bias(n_new, lambda t_row, col_tok: t_row - col_tok)
        for i0 in range(0, P - G, G):
            update([scores(page_k(i)) for i in range(i0, i0 + G)],
                   [page_v(i) for i in range(i0, i0 + G)])
        s_list = [scores(page_k(i)) for i in range(P - G, P - 1)]
        s_list.append(scores(page_k(P - 1), bias=bias_p))
        s_list.append(scores(kn_ref[...], bias=bias_n, causal=True))
        update(s_list, [page_v(i) for i in range(P - G, P)] + [vn_ref[...]])

        lam_init = li_ref[0]
        lam = _lambda(lq1_ref, lk1_ref, lq2_ref, lk2_ref, lam_init)
        gate = gd_ref[...]
        outs = []
        for h in range(DIFF_HEADS):
            r0 = h * 2 * TP
            r1 = r0 + TP
            outs.append(_diff_epilogue(
                acc_scr[r0:r0 + TP, 0:DIFF_V], acc_scr[r0:r0 + TP, DIFF_V:DIFF_V + 1],
                acc_scr[r1:r1 + TP, 0:DIFF_V], acc_scr[r1:r1 + TP, DIFF_V:DIFF_V + 1],
                lam, lam_init, sub_ref[...], gate[:, h * DIFF_V:(h + 1) * DIFF_V]))
        o_ref[...] = jnp.concatenate(outs, axis=-1).astype(o_ref.dtype)


def _attn_fused_kernel(pt_ref, rb_ref, li_ref, qt_ref, k_ref, vt_ref, gdp_ref, lq1_ref, lk1_ref,
                       lq2_ref, lk2_ref, sub_ref, qs_ref, kn_ref, vn_ref, gds_ref, *rest,
                       tq, tk, pages_per_step, groups_per_seq):
    del pt_ref
    P = pages_per_step
    k_pages = rest[0:P]
    v_pages = rest[P:2 * P]
    op_ref, os_ref, bias_scr, s_scr, acc_scr, m_dec, acc_dec = rest[2 * P:]
    h = pl.program_id(1)
    qi = pl.program_id(2)
    step = (pl.program_id(0) * pl.num_programs(1) + h) * pl.num_programs(2) + qi
    lam_refs = (lq1_ref, lk1_ref, lq2_ref, lk2_ref)
    _decode_step(step % groups_per_seq, groups_per_seq - 1, rb_ref, li_ref, qs_ref, kn_ref, vn_ref,
                 gds_ref, *lam_refs, sub_ref, k_pages, v_pages, os_ref, m_dec, acc_dec)
    _prompt_attn_step(h, qi, rb_ref, li_ref, qt_ref, k_ref, vt_ref, gdp_ref, *lam_refs, sub_ref,
                      op_ref, bias_scr, s_scr, acc_scr, tq=tq, tk=tk)


def _attn_fused(qt, kb, vt, gdp, qs, kn, vn, gds, cache_k, cache_v, page_ids, rel_bias, lam_init,
                lam_vecs, subln, batch, tq, tk):
    m = kb.shape[0]
    t = m // batch
    nq = t // tq
    assert tk >= MAX_DISTANCE and tq % tk == 0
    bs = qs.shape[0]
    n_pages = page_ids.shape[0] // bs
    steps = batch * DIFF_HEADS * nq
    P = bs * n_pages // steps
    groups = n_pages // P
    assert P * steps == bs * n_pages and groups * P == n_pages

    step_of = lambda b, h, i: (b * DIFF_HEADS + h) * nq + i
    tile = pl.BlockSpec((tq, DIFF_V), lambda b, h, i, pt: (b * nq + i, h))
    tile_t = pl.BlockSpec((DIFF_V, tq), lambda b, h, i, pt: (h, b * nq + i))
    seq = pl.BlockSpec((t, DIFF_V), lambda b, h, i, pt: (b, h))
    seq_t = pl.BlockSpec((DIFF_V, t), lambda b, h, i, pt: (h, b))
    vec = lambda a: pl.BlockSpec(a.shape, lambda b, h, i, pt: (0,) * a.ndim)
    smem = pl.BlockSpec(memory_space=pltpu.SMEM)
    per_seq = lambda shape: pl.BlockSpec(
        (None,) + shape, lambda b, h, i, pt: (step_of(b, h, i) // groups, 0, 0))
    tok = per_seq((TOKENS_PADDED, DIFF_WIDTH))
    new = per_seq((TOKENS_PADDED * DIFF_HEADS, DIFF_V))

    def page_spec(slot):
        return pl.BlockSpec((None, PAGE_ROWS, DIFF_V),
                            lambda b, h, i, pt: (pt[step_of(b, h, i) * P + slot], 0, 0))

    grid_spec = pltpu.PrefetchScalarGridSpec(
        num_scalar_prefetch=1,
        grid=(batch, DIFF_HEADS, nq),
        in_specs=([smem, smem, tile_t, seq, seq_t, tile] + [vec(a) for a in lam_vecs]
                  + [vec(subln), tok, new, new, tok]
                  + [page_spec(s) for s in range(P)] + [page_spec(s) for s in range(P)]),
        out_specs=(tile, tok),
        scratch_shapes=[pltpu.VMEM((tq // tk + 1, tk, 2 * tq), F32),
                        pltpu.VMEM((2, tk, 2 * tq), F32),
                        pltpu.VMEM((DIFF_V + ONES_ROWS, 2 * tq), F32),
                        pltpu.VMEM((SCORE_ROWS, 1), F32),
                        pltpu.VMEM((SCORE_ROWS, 2 * DIFF_V), F32)],
    )
    return pl.pallas_call(
        functools.partial(_attn_fused_kernel, tq=tq, tk=tk, pages_per_step=P,
                          groups_per_seq=groups),
        grid_spec=grid_spec,
        out_shape=(jax.ShapeDtypeStruct((m, DIFF_WIDTH), BF16),
                   jax.ShapeDtypeStruct((bs, TOKENS_PADDED, DIFF_WIDTH), BF16)),
        compiler_params=pltpu.CompilerParams(
            dimension_semantics=("arbitrary", "arbitrary", "arbitrary"),
            vmem_limit_bytes=VMEM_LIMIT_V7X),
        name="diff_attn",
    )(page_ids, rel_bias, lam_init, qt, kb, vt, gdp, *lam_vecs, subln, qs, kn, vn, gds,
      *([cache_k] * P), *([cache_v] * P))


def _layer_params(l, pre_norm, post_norm, w_in, mu_shift, w0, w_up, a0, a_up, k_k, k_a, r_k,
                  lnx_w, lnx_b, lam_q1, lam_k1, lam_q2, lam_k2, subln, w_out):
    row = lambda a: a.reshape(1, -1)
    zeros = jnp.zeros((LORA, RWKV_WIDTH), F32)
    head_of = jnp.arange(GROUP_W) // HEAD
    return dict(
        pre_norm=row(pre_norm[l]), post_norm=row(post_norm[l]),
        w_in=w_in[l].astype(BF16), w_out=w_out[l].astype(BF16),
        mu=row(mu_shift[l]), w0=row(w0[l]), a0=row(a0[l]),
        w1=jnp.concatenate([w_up[l], zeros], axis=0).astype(BF16),
        w2=jnp.concatenate([zeros, a_up[l]], axis=0).astype(BF16),
        k_k=row(k_k[l]), k_a=row(k_a[l]), r_k=row(r_k[l]),
        lnx_w=row(lnx_w[l]), lnx_b=row(lnx_b[l]),
        e_ones=(head_of[:, None] == head_of[None, :]).astype(BF16),
        lam_vecs=(row(lam_q1[l]), row(lam_k1[l]), row(lam_q2[l]), row(lam_k2[l])),
        subln=row(subln[l]),
        lam_init=jnp.full((1,), 0.8 - 0.6 * math.exp(-0.3 * l), F32),
    )


def _rwkv_stream(zsh, gr, batch, p, shift_prev, s0, chunk, t_valid, seqs_per_step,
                 chunks_per_step=1):
    t = zsh.shape[0] // batch
    return _rwkv(zsh.reshape(batch, t, SHIFT_WIDTH), gr.reshape(batch, t, 512), shift_prev, s0, p,
                 chunk, t_valid, seqs_per_step, chunks_per_step)


def kernel(x_prompt, x_sample, cache_k, cache_v, state_wkv, state_shift, page_table, rel_bias,
           pre_norm, post_norm, w_in, mu_shift, w0, w_up, a0, a_up, k_k, k_a, r_k, lnx_w, lnx_b,
           lam_q1, lam_k1, lam_q2, lam_k2, subln, w_out):
    return _forward(x_prompt, x_sample, cache_k, cache_v, state_wkv, state_shift, page_table,
                    rel_bias, pre_norm, post_norm, w_in, mu_shift, w0, w_up, a0, a_up, k_k, k_a,
                    r_k, lnx_w, lnx_b, lam_q1, lam_k1, lam_q2, lam_k2, subln, w_out,
                    PROMPT_TM=512, PROMPT_TM_OUT=1024, PROMPT_CHUNK=64, PROMPT_TQ=512,
                    PROMPT_TK=256, SAMPLE_SEQS_PER_STEP=8, PROMPT_CHUNKS_PER_STEP=2)


def _forward(x_prompt, x_sample, cache_k, cache_v, state_wkv, state_shift, page_table, rel_bias,
             pre_norm, post_norm, w_in, mu_shift, w0, w_up, a0, a_up, k_k, k_a, r_k, lnx_w, lnx_b,
             lam_q1, lam_k1, lam_q2, lam_k2, subln, w_out, *, PROMPT_TM, PROMPT_TM_OUT,
             PROMPT_CHUNK, PROMPT_TQ, PROMPT_TK, SAMPLE_SEQS_PER_STEP, PROMPT_CHUNKS_PER_STEP):
    bp, tp, d = x_prompt.shape
    bs, ts, _ = x_sample.shape
    depth = w_in.shape[0]
    n_pool = cache_k.shape[1]
    n_pages = page_table.shape[1]
    TS_PAD = TOKENS_PADDED

    xp = x_prompt.reshape(bp * tp, d)
    xs = jnp.pad(x_sample, ((0, 0), (0, TS_PAD - ts), (0, 0))).reshape(bs * TS_PAD, d)
    ck = cache_k.reshape(depth * n_pool, PAGE_ROWS, DIFF_V)
    cv = cache_v.reshape(depth * n_pool, PAGE_ROWS, DIFF_V)
    zero_shift = jnp.zeros((bp, 1, SHIFT_WIDTH), F32)
    zero_state = jnp.zeros((bp, RWKV_HEADS, HEAD, HEAD), F32)

    outs = {name: [] for name in ("wp", "sp", "ks", "vs", "ws", "ss")}
    kv_rows = None
    for l in range(depth):
        p = _layer_params(l, pre_norm, post_norm, w_in, mu_shift, w0, w_up, a0, a_up, k_k, k_a,
                          r_k, lnx_w, lnx_b, lam_q1, lam_k1, lam_q2, lam_k2, subln, w_out)

        zsh_p, gr_p, qt, k_rows, v_rows, gd_p, kb, vt = _inproj_prompt(
            xp, p["pre_norm"], p["w_in"], PROMPT_TM, l, depth, kv_rows)
        kv_rows = (k_rows, v_rows)
        zsh_s, gr_s, q_s, k_s, v_s, gd_s = _inproj_sample(xs, p["pre_norm"], p["w_in"],
                                                          bs * TS_PAD)
        yr_p, wp, sp = _rwkv_stream(zsh_p, gr_p, bp, p, zero_shift, zero_state, PROMPT_CHUNK,
                                    PROMPT_CHUNK, bp, PROMPT_CHUNKS_PER_STEP)
        yr_s, ws, ss = _rwkv_stream(zsh_s, gr_s, bs, p, state_shift[l][:, None, :], state_wkv[l],
                                    TS_PAD, ts, SAMPLE_SEQS_PER_STEP)
        page_ids = (page_table + l * n_pool).reshape(-1).astype(jnp.int32)
        r3 = lambda a: a.reshape(bs, TS_PAD, DIFF_WIDTH)
        rows = lambda a: a.reshape(bs, TS_PAD * DIFF_HEADS, DIFF_V)
        yd_p, yd_s = _attn_fused(qt, kb, vt, gd_p, r3(q_s), rows(k_s), rows(v_s), r3(gd_s), ck, cv,
                                 page_ids, rel_bias, p["lam_init"], p["lam_vecs"], p["subln"], bp,
                                 PROMPT_TQ, PROMPT_TK)
        xp = _outproj(yr_p.reshape(bp * tp, RWKV_WIDTH), yd_p, p["w_out"], p["post_norm"], xp,
                      PROMPT_TM_OUT)
        xs = _outproj(yr_s.reshape(bs * TS_PAD, RWKV_WIDTH), yd_s.reshape(bs * TS_PAD, DIFF_WIDTH),
                      p["w_out"], p["post_norm"], xs, bs * TS_PAD)
        outs["wp"].append(wp)
        outs["sp"].append(sp.reshape(bp, SHIFT_WIDTH))
        outs["ks"].append(k_s.reshape(bs, TS_PAD, DIFF_HEADS, DIFF_V)[:, :ts])
        outs["vs"].append(v_s.reshape(bs, TS_PAD, DIFF_HEADS, DIFF_V)[:, :ts])
        outs["ws"].append(ws)
        outs["ss"].append(ss.reshape(bs, SHIFT_WIDTH))

    st = lambda name: jnp.stack(outs[name])
    kv_shape = (depth, bp, tp, DIFF_HEADS, DIFF_V)
    return (xp.reshape(bp, tp, d), xs.reshape(bs, TS_PAD, d)[:, :ts], kv_rows[0].reshape(kv_shape),
            kv_rows[1].reshape(kv_shape), st("wp"), st("sp"), st("ks"), st("vs"), st("ws"),
            st("ss"))
```

```python
import functools
import math

import jax
import jax.numpy as jnp
from jax import lax
from jax.experimental import pallas as pl
from jax.experimental.pallas import tpu as pltpu

F32 = jnp.float32
BF16 = jnp.bfloat16

RWKV_WIDTH = 512
DIFF_WIDTH = 512
HEAD = 64
RWKV_HEADS = RWKV_WIDTH // HEAD
LORA = 64
DIFF_HEADS = 4
DIFF_V = 128
NUM_BUCKETS = 32
MAX_EXACT = NUM_BUCKETS // 2
MAX_DISTANCE = 128
PAGE = 128
RMS_EPS = 1e-6
SUBLN_EPS = 1e-5
GN_EPS = 64e-5
NEG_INF = -1e30
LOG2E = 1.4426950408889634
SHIFT_WIDTH = 3 * RWKV_WIDTH + 2 * LORA
SEG_SHIFT = (0, SHIFT_WIDTH)
SEG_GR = (SHIFT_WIDTH, SHIFT_WIDTH + 512)
SEG_Q = (SEG_GR[1], SEG_GR[1] + 512)
SEG_K = (SEG_Q[1], SEG_Q[1] + 512)
SEG_V = (SEG_K[1], SEG_K[1] + 512)
SEG_GD = (SEG_V[1], SEG_V[1] + 512)

VMEM_LIMIT_V7X = 48 * 1024 * 1024
LANES = 128
ONES_ROWS = 16
GROUP_HEADS = 4
GROUP_W = GROUP_HEADS * HEAD
N_GROUPS = RWKV_HEADS // GROUP_HEADS


def _dot(a, b):
    return jnp.dot(a.astype(BF16), b.astype(BF16), preferred_element_type=F32)


def _dot_nt(a, b):
    return lax.dot_general(a.astype(BF16), b.astype(BF16), (((1,), (1,)), ((), ())),
                           preferred_element_type=F32)


def _dot_tn(a, b):
    return lax.dot_general(a.astype(BF16), b.astype(BF16), (((0,), (0,)), ((), ())),
                           preferred_element_type=F32)


def _split(x):
    hi = x.astype(BF16)
    lo = (x - hi.astype(F32)).astype(BF16)
    return hi, lo


def _dot_exact_lhs(a_exact, b):
    hi, lo = _split(b)
    return (jnp.dot(a_exact, hi, preferred_element_type=F32)
            + jnp.dot(a_exact, lo, preferred_element_type=F32))


def _sigmoid(x):
    return 1.0 / (1.0 + jnp.exp(-x))


def _rel_bias_minus_far(dist, rb_ref, head):
    n = jnp.maximum(dist, 0)
    nf = jnp.maximum(n, 1).astype(F32)
    large = MAX_EXACT + (jnp.log(nf / MAX_EXACT) / math.log(MAX_DISTANCE / MAX_EXACT)
                         * (NUM_BUCKETS - MAX_EXACT)).astype(jnp.int32)
    large = jnp.minimum(large, NUM_BUCKETS - 1)
    bucket = jnp.where(n < MAX_EXACT, n, large)
    far = rb_ref[NUM_BUCKETS - 1, head]
    out = jnp.zeros(dist.shape, F32)
    for b in range(NUM_BUCKETS - 1):
        out = jnp.where(bucket == b, rb_ref[b, head] - far, out)
    return out


def _projected_segments(x_ref, g_ref, w_ref):
    x = x_ref[...]
    ms = jnp.mean(x * x, axis=-1, keepdims=True)
    h = (x * lax.rsqrt(ms + RMS_EPS) * g_ref[...]).astype(BF16)

    def seg(lo_hi):
        lo, hi = lo_hi
        return jnp.dot(h, w_ref[:, lo:hi], preferred_element_type=F32)

    return seg


Q_SCALE = LOG2E * HEAD ** -0.5


def _inproj_sample_kernel(x_ref, g_ref, w_ref, zsh_ref, gr_ref, q_ref, k_ref, v_ref, gd_ref):
    seg = _projected_segments(x_ref, g_ref, w_ref)
    zsh_ref[...] = seg(SEG_SHIFT)
    gr_ref[...] = seg(SEG_GR)
    q_ref[...] = seg(SEG_Q) * Q_SCALE
    k_ref[...] = seg(SEG_K)
    v_ref[...] = seg(SEG_V)
    gd_ref[...] = seg(SEG_GD)


def _inproj_prompt_kernel(x_ref, g_ref, w_ref, *refs):
    zsh_ref, gr_ref, qt_ref, krows_ref, vrows_ref, gd_ref, kb_ref, vt_ref = refs[-8:]
    seg = _projected_segments(x_ref, g_ref, w_ref)
    tm = x_ref.shape[0]
    zsh_ref[...] = seg(SEG_SHIFT)
    gr_ref[...] = seg(SEG_GR)
    qt_ref[...] = jnp.transpose(seg(SEG_Q) * Q_SCALE).astype(BF16)
    gd_ref[...] = seg(SEG_GD)
    k = seg(SEG_K)
    v = seg(SEG_V)
    kb_ref[...] = k.astype(BF16)
    vt_ref[...] = jnp.transpose(v).astype(BF16)
    for hd in range(DIFF_HEADS):
        cols = slice(hd * DIFF_V, (hd + 1) * DIFF_V)
        krows_ref[pl.ds(hd, tm, stride=DIFF_HEADS), :] = k[:, cols]
        vrows_ref[pl.ds(hd, tm, stride=DIFF_HEADS), :] = v[:, cols]


def _inproj_sample(x, g, w_bf16, tm):
    m, d = x.shape
    row = lambda width: pl.BlockSpec((tm, width), lambda i: (i, 0))
    f32 = jax.ShapeDtypeStruct((m, 512), F32)
    return pl.pallas_call(
        _inproj_sample_kernel,
        grid=(m // tm,),
        in_specs=[row(d), pl.BlockSpec((1, d), lambda i: (0, 0)),
                  pl.BlockSpec(w_bf16.shape, lambda i: (0, 0))],
        out_specs=(row(SHIFT_WIDTH), row(512), row(512), row(512), row(512), row(512)),
        out_shape=(jax.ShapeDtypeStruct((m, SHIFT_WIDTH), F32), f32, f32, f32, f32, f32),
        compiler_params=pltpu.CompilerParams(dimension_semantics=("arbitrary",),
                                             vmem_limit_bytes=VMEM_LIMIT_V7X),
        name="inproj_sample",
    )(x, g, w_bf16)


def _inproj_prompt(x, g, w_bf16, tm, layer, depth, kv_rows):
    m, d = x.shape
    steps = m // tm
    row = lambda width: pl.BlockSpec((tm, width), lambda i: (i, 0))
    col = pl.BlockSpec((512, tm), lambda i: (0, i))
    rows = pl.BlockSpec((tm * DIFF_HEADS, DIFF_V), lambda i: (layer * steps + i, 0))
    f32 = jax.ShapeDtypeStruct((m, 512), F32)
    bf_t = jax.ShapeDtypeStruct((512, m), BF16)
    rows_shape = jax.ShapeDtypeStruct((depth * m * DIFF_HEADS, DIFF_V), F32)
    in_specs = [row(d), pl.BlockSpec((1, d), lambda i: (0, 0)),
                pl.BlockSpec(w_bf16.shape, lambda i: (0, 0), pipeline_mode=pl.Buffered(1))]
    operands = [x, g, w_bf16]
    aliases = {}
    if kv_rows is not None:
        in_specs += [pl.BlockSpec(memory_space=pl.ANY)] * 2
        operands += list(kv_rows)
        aliases = {3: 3, 4: 4}
    return pl.pallas_call(
        _inproj_prompt_kernel,
        grid=(steps,),
        in_specs=in_specs,
        out_specs=(row(SHIFT_WIDTH), row(512), col, rows, rows, row(512), row(512), col),
        out_shape=(jax.ShapeDtypeStruct((m, SHIFT_WIDTH), F32), f32, bf_t, rows_shape, rows_shape,
                   f32, jax.ShapeDtypeStruct((m, 512), BF16), bf_t),
        input_output_aliases=aliases,
        compiler_params=pltpu.CompilerParams(dimension_semantics=("arbitrary",),
                                             vmem_limit_bytes=VMEM_LIMIT_V7X),
        name="inproj_prompt",
    )(*operands)


def _outproj_kernel(yr_ref, yd_ref, w_ref, g_ref, x_ref, o_ref):
    out = (jnp.dot(yr_ref[...], w_ref[0:RWKV_WIDTH, :], preferred_element_type=F32)
           + jnp.dot(yd_ref[...], w_ref[RWKV_WIDTH:, :], preferred_element_type=F32))
    ms = jnp.mean(out * out, axis=-1, keepdims=True)
    o_ref[...] = x_ref[...] + out * lax.rsqrt(ms + RMS_EPS) * g_ref[...]


def _outproj(yr, yd, w_bf16, g, x, tm):
    m, d = x.shape
    row = lambda width: pl.BlockSpec((tm, width), lambda i: (i, 0))
    return pl.pallas_call(
        _outproj_kernel,
        grid=(m // tm,),
        in_specs=[row(512), row(512), pl.BlockSpec(w_bf16.shape, lambda i: (0, 0)),
                  pl.BlockSpec((1, d), lambda i: (0, 0)), row(d)],
        out_specs=row(d),
        out_shape=jax.ShapeDtypeStruct((m, d), F32),
        compiler_params=pltpu.CompilerParams(dimension_semantics=("arbitrary",),
                                             vmem_limit_bytes=VMEM_LIMIT_V7X),
        name="outproj",
    )(yr, yd, w_bf16, g, x)


def _rwkv_kernel(zsh_ref, gr_ref, sprev_ref, s0_ref, mu_ref, w0_ref, w1_ref, a0_ref, w2_ref,
                 kk_ref, ka_ref, rk_ref, lnw_ref, lnb_ref, e_ref,
                 y_ref, sout_ref, shift_ref, s_scr, carry_scr, *, chunk, t_valid):
    C = chunk
    NB = zsh_ref.shape[0]
    R = GROUP_HEADS * C
    c_idx = pl.program_id(1)
    last = pl.num_programs(1) - 1

    ii = lax.broadcasted_iota(jnp.int32, (GROUP_W, GROUP_W), 0)
    jj = lax.broadcasted_iota(jnp.int32, (GROUP_W, GROUP_W), 1)
    eye_g = jnp.where(ii == jj, 1.0, 0.0).astype(BF16)
    block_g = (ii // HEAD) == (jj // HEAD)

    def exact_nt(lhs_exact, x):
        hi = x.astype(BF16)
        r1 = x - hi.astype(F32)
        mid = r1.astype(BF16)
        lo = (r1 - mid.astype(F32)).astype(BF16)
        return _dot_nt(lhs_exact, hi) + _dot_nt(lhs_exact, mid) + _dot_nt(lhs_exact, lo)

    @pl.when(c_idx == 0)
    def _():
        carry_scr[...] = sprev_ref[...]
        for b in range(NB):
            for g in range(N_GROUPS):
                heads = s0_ref[b, g * GROUP_HEADS:(g + 1) * GROUP_HEADS]
                s_t = exact_nt(eye_g[0:HEAD, 0:HEAD], heads.reshape(GROUP_W, HEAD))
                s_scr[b, g] = jnp.where(block_g, jnp.concatenate([s_t] * GROUP_HEADS, axis=0),
                                        0.0)

    CH = zsh_ref.shape[1] // C
    NS = NB * CH
    assert CH == 1 or t_valid == C
    z = zsh_ref[...].reshape(NS * C, SHIFT_WIDTH)
    row = lax.broadcasted_iota(jnp.int32, (NS * C, 1), 0)
    tok = row % C
    prev = pltpu.roll(z, 1, axis=0)
    last_rows = []
    for b in range(NB):
        prev = jnp.where(row == b * CH * C, carry_scr[b], prev)
        last_tok = b * CH * C + (CH - 1) * C + t_valid - 1
        last_rows.append(z[last_tok:last_tok + 1, :])
        carry_scr[b] = last_rows[b]
    zm = z + mu_ref[...] * (prev - z)

    r = zm[:, 0:512]
    k = zm[:, 512:1024]
    v = zm[:, 1024:1536]
    zwa = zm[:, 1536:1664]
    lora_w = _dot(jnp.tanh(zwa), w1_ref[...])
    lora_a = _dot(zwa, w2_ref[...])
    logw = -math.exp(-0.5) * _sigmoid(w0_ref[...] + lora_w)
    alr = _sigmoid(a0_ref[...] + lora_a)

    e_ones = e_ref[...]

    def head_sum(x):
        return jnp.concatenate(
            [jnp.dot(x[:, g * GROUP_W:(g + 1) * GROUP_W].astype(BF16), e_ones,
                     preferred_element_type=F32) for g in range(N_GROUPS)], axis=1)

    kk = k * kk_ref[...]
    kk = kk * lax.rsqrt(jnp.maximum(head_sum(kk * kk), 1e-24))
    k2 = k * (1.0 + (alr - 1.0) * ka_ref[...])
    a_vec = -kk
    b_vec = kk * alr
    if t_valid < C:
        valid = tok < t_valid
        logw = jnp.where(valid, logw, 0.0)
        b_vec = jnp.where(valid, b_vec, 0.0)
        k2 = jnp.where(valid, k2, 0.0)

    ti = lax.broadcasted_iota(jnp.int32, (NS * C, NS * C), 0)
    si = lax.broadcasted_iota(jnp.int32, (NS * C, NS * C), 1)
    ltri = jnp.where(((ti // C) == (si // C)) & (ti >= si), 1.0, 0.0).astype(BF16)
    cum = _dot_exact_lhs(ltri, logw)
    ew_incl = jnp.exp(cum)
    ew_excl = jnp.exp(cum - logw)
    e_inv = jnp.exp(-cum)
    a_t = a_vec * ew_excl
    r_t = r * ew_incl
    b_t = b_vec * e_inv
    k_t = k2 * e_inv

    ri = lax.broadcasted_iota(jnp.int32, (R, GROUP_W), 0)
    li = lax.broadcasted_iota(jnp.int32, (R, GROUP_W), 1)
    one_zero = lambda m: jnp.where(m, 1.0, 0.0).astype(BF16)
    head_mask = one_zero((ri // C) == (li // HEAD))
    wi = lax.broadcasted_iota(jnp.int32, (R, R), 0)
    wj = lax.broadcasted_iota(jnp.int32, (R, R), 1)
    wide_mask = one_zero((wi // C) == (wj // C))
    tt = lax.broadcasted_iota(jnp.int32, (C, R), 0)
    ss = lax.broadcasted_iota(jnp.int32, (C, R), 1) % C
    strict = tt > ss
    incl = tt >= ss
    eye_w = jnp.where(tt == ss, 1.0, 0.0)
    di = lax.broadcasted_iota(jnp.int32, (GROUP_W, GROUP_W), 0)
    dj = lax.broadcasted_iota(jnp.int32, (GROUP_W, GROUP_W), 1)
    state_mask = (di // HEAD) == (dj // HEAD)

    def bd(x, mask):
        if C % 16 == 0:
            return jnp.concatenate([x.astype(BF16)] * GROUP_HEADS, axis=0) * mask
        return (jnp.concatenate([x] * GROUP_HEADS, axis=0) * mask.astype(F32)).astype(BF16)

    def dot_cat(lhs, rhs):
        if all(a.shape[1] % LANES == 0 for a in lhs):
            return _dot(jnp.concatenate(lhs, axis=1), jnp.concatenate(rhs, axis=0))
        return sum(_dot(a, b) for a, b in zip(lhs, rhs))

    n_levels = int(math.log2(C))

    ops = {}
    for b in range(NB):
        for ch in range(CH):
            for g in range(N_GROUPS):
                r0 = (b * CH + ch) * C
                rs = slice(r0, r0 + C)
                ls = slice(g * GROUP_W, (g + 1) * GROUP_W)
                g_c = ew_incl[r0 + C - 1:r0 + C, ls]
                ops[b, ch, g] = dict(
                    a=a_t[rs, ls].astype(BF16), r=r_t[rs, ls].astype(BF16),
                    bk_hat=jnp.concatenate([b_t[rs, ls] * g_c, k_t[rs, ls] * g_c], axis=0),
                    b_bd=bd(b_t[rs, ls], head_mask), k_bd=bd(k_t[rs, ls], head_mask),
                    v=v[rs, ls], v_bd=bd(v[rs, ls], head_mask), g_c=g_c)
    for o in ops.values():
        ar = jnp.concatenate([o["a"], o["r"]], axis=0)
        nb = _dot_nt(ar, o["b_bd"])
        nk = _dot_nt(ar, o["k_bd"])
        o["n_ab"] = jnp.where(strict, nb[0:C], 0.0)
        o["m_rb"] = jnp.where(incl, nb[C:2 * C], 0.0)
        o["n_ak"] = jnp.where(strict, nk[0:C], 0.0)
        o["m_rk"] = jnp.where(incl, nk[C:2 * C], 0.0)
    for o in ops.values():
        o["t_inv"] = eye_w + o["n_ab"]
        o["p"] = _dot(o["n_ab"], bd(o["n_ab"], wide_mask))
    for j in range(1, n_levels - 1):
        for o in ops.values():
            both = _dot(jnp.concatenate([o["p"], o["t_inv"]], axis=0), bd(o["p"], wide_mask))
            o["p"] = both[0:C]
            o["t_inv"] = o["t_inv"] + both[C:2 * C]
    for o in ops.values():
        o["t_inv"] = o["t_inv"] + _dot(o["t_inv"], bd(o["p"], wide_mask))

    for ch in range(CH):
        now = [(b, g, ops[b, ch, g]) for b in range(NB) for g in range(N_GROUPS)]
        for b, g, o in now:
            o["s_prev"] = s_scr[b, g]
            o["s_bf"] = o["s_prev"].astype(BF16)
            o["rhs"] = dot_cat([o["a"], o["n_ak"]], [o["s_bf"], o["v_bd"]])
        for b, g, o in now:
            o["u"] = _dot(o["t_inv"], bd(o["rhs"], head_mask))
        for b, g, o in now:
            o["y"] = dot_cat([o["r"], o["m_rb"], o["m_rk"]],
                             [o["s_bf"], bd(o["u"], head_mask), o["v_bd"]])
        for b, g, o in now:
            g_col = jnp.transpose(jnp.broadcast_to(o["g_c"], (GROUP_W, GROUP_W)))
            upd = _dot_tn(o["bk_hat"], jnp.concatenate([o["u"], o["v"]], axis=0))
            s_scr[b, g] = o["s_prev"] * g_col + jnp.where(state_mask, upd, 0.0)

    y = jnp.concatenate(
        [jnp.concatenate([ops[b, ch, g]["y"] for g in range(N_GROUPS)], axis=-1)
         for b in range(NB) for ch in range(CH)], axis=0)
    inv_n = 1.0 / HEAD
    mean = head_sum(y) * inv_n
    d = y - mean
    var = head_sum(d * d) * inv_n
    yn = d * lax.rsqrt(var + GN_EPS) * lnw_ref[...] + lnb_ref[...]
    bonus = head_sum(r * k2 * rk_ref[...]) * v
    gate = gr_ref[...].reshape(NS * C, RWKV_WIDTH)
    out = (yn + bonus) * (gate * _sigmoid(gate))
    y_ref[...] = out.reshape(NB, CH * C, RWKV_WIDTH).astype(y_ref.dtype)

    @pl.when(c_idx == last)
    def _():
        for b in range(NB):
            shift_ref[b] = last_rows[b]
            for g in range(N_GROUPS):
                s_bd = s_scr[b, g]
                s_t = s_bd[0:HEAD]
                for h in range(1, GROUP_HEADS):
                    s_t = s_t + s_bd[h * HEAD:(h + 1) * HEAD]
                heads = exact_nt(eye_g, s_t)
                sout_ref[b, g * GROUP_HEADS:(g + 1) * GROUP_HEADS] = heads.reshape(
                    GROUP_HEADS, HEAD, HEAD)


def _rwkv(zsh, gr, shift_prev, s0, p, chunk, t_valid, seqs_per_step, chunks_per_step):
    b, t, _ = zsh.shape
    step_tokens = chunk * chunks_per_step
    nc = t // step_tokens
    nb = seqs_per_step
    tok = lambda width: pl.BlockSpec((nb, step_tokens, width), lambda i, c: (i, c, 0))
    per_b = lambda shape: pl.BlockSpec((nb,) + shape, lambda i, c: (i,) + (0,) * len(shape))
    const = lambda a: pl.BlockSpec(a.shape, lambda i, c: (0,) * a.ndim)
    params = (p["mu"], p["w0"], p["w1"], p["a0"], p["w2"], p["k_k"], p["k_a"], p["r_k"],
              p["lnx_w"], p["lnx_b"], p["e_ones"])
    return pl.pallas_call(
        functools.partial(_rwkv_kernel, chunk=chunk, t_valid=t_valid),
        grid=(b // nb, nc),
        in_specs=[tok(SHIFT_WIDTH), tok(512), per_b((1, SHIFT_WIDTH)),
                  per_b((RWKV_HEADS, HEAD, HEAD))] + [const(a) for a in params],
        out_specs=(tok(512), per_b((RWKV_HEADS, HEAD, HEAD)), per_b((1, SHIFT_WIDTH))),
        out_shape=(jax.ShapeDtypeStruct((b, t, 512), BF16),
                   jax.ShapeDtypeStruct((b, RWKV_HEADS, HEAD, HEAD), F32),
                   jax.ShapeDtypeStruct((b, 1, SHIFT_WIDTH), F32)),
        scratch_shapes=[pltpu.VMEM((nb, N_GROUPS, GROUP_W, GROUP_W), F32),
                        pltpu.VMEM((nb, 1, SHIFT_WIDTH), F32)],
        compiler_params=pltpu.CompilerParams(dimension_semantics=("arbitrary", "arbitrary"),
                                             vmem_limit_bytes=VMEM_LIMIT_V7X),
        name="rwkv7",
    )(zsh, gr, shift_prev, s0, *params)


def _lambda(lq1_ref, lk1_ref, lq2_ref, lk2_ref, lam_init):
    s1 = jnp.sum(lq1_ref[...] * lk1_ref[...], axis=-1, keepdims=True)
    s2 = jnp.sum(lq2_ref[...] * lk2_ref[...], axis=-1, keepdims=True)
    return jnp.exp(s1) - jnp.exp(s2) + lam_init


def _diff_epilogue(acc0, l0, acc1, l1, lam, lam_init, subln, gate):
    o = acc0 / l0 - lam * (acc1 / l1)
    ms = jnp.mean(o * o, axis=-1, keepdims=True)
    on = o * lax.rsqrt(ms + SUBLN_EPS) * subln * (1.0 - lam_init)
    return on * (gate * _sigmoid(gate))


def _prompt_attn_step(h, qi, rb_ref, li_ref, qt_ref, k_ref, vt_ref, gd_ref, lq1_ref, lk1_ref,
                      lq2_ref, lk2_ref, sub_ref, o_ref, bias_scr, s_scr, acc_scr, *, tq, tk):
    nsub = tq // tk
    first_near = nsub * qi - 1

    @pl.when(qi == 0)
    def _():
        key = lax.broadcasted_iota(jnp.int32, (tk, tq), 0)
        qry = lax.broadcasted_iota(jnp.int32, (tk, tq), 1)
        for i in range(nsub + 1):
            dist = qry - key - (i - 1) * tk
            bias = _rel_bias_minus_far(dist, rb_ref, h) * LOG2E
            bias = jnp.where(dist >= 0, bias, NEG_INF)
            bias_scr[i] = jnp.concatenate([bias, bias], axis=1)

    qt = qt_ref[...]
    feat = lax.broadcasted_iota(jnp.int32, (2 * HEAD, tq), 0)
    qt2 = jnp.concatenate([jnp.where(feat < HEAD, qt, jnp.zeros_like(qt)),
                           jnp.where(feat >= HEAD, qt, jnp.zeros_like(qt))], axis=1)

    acc_scr[...] = jnp.zeros(acc_scr.shape, F32)
    ones_rows = jnp.ones((ONES_ROWS, tk), BF16)

    def logits(kj):
        off = pl.multiple_of(kj * tk, tk)
        return jnp.dot(k_ref[pl.ds(off, tk), :], qt2, preferred_element_type=F32)

    def absorb(s, kj, m_old):
        off = pl.multiple_of(kj * tk, tk)
        vt_ext = jnp.concatenate([vt_ref[:, pl.ds(off, tk)], ones_rows], axis=0)
        m_new = jnp.maximum(m_old, jnp.max(s, axis=0, keepdims=True))
        pm = jnp.exp2(s - m_new).astype(BF16)
        acc_scr[...] = (jnp.exp2(m_old - m_new) * acc_scr[...]
                        + jnp.dot(vt_ext, pm, preferred_element_type=F32))
        return m_new

    assert nsub == 2
    m_run = jnp.full((1, 2 * tq), NEG_INF, F32)

    @pl.when(qi >= 1)
    def _():
        s_scr[0] = logits(0)

    def far_pair(i, m):
        s_scr[1] = logits(2 * i + 1)
        m = absorb(s_scr[0], 2 * i, m)
        s_scr[0] = logits(2 * i + 2)
        return absorb(s_scr[1], 2 * i + 1, m)

    m_run = lax.fori_loop(0, qi - 1, far_pair, m_run)

    def finish_below_diagonal(m):
        s_scr[1] = logits(first_near)
        m = absorb(s_scr[0], first_near - 1, m)
        s_scr[0] = logits(first_near + 1)
        m = absorb(s_scr[1] + bias_scr[0], first_near, m)
        s_scr[1] = logits(first_near + 2)
        m = absorb(s_scr[0] + bias_scr[1], first_near + 1, m)
        return absorb(s_scr[1] + bias_scr[2], first_near + 2, m)

    def first_query_tile(m):
        s_scr[0] = logits(0)
        s_scr[1] = logits(1)
        m = absorb(s_scr[0] + bias_scr[1], 0, m)
        return absorb(s_scr[1] + bias_scr[2], 1, m)

    lax.cond(qi >= 1, finish_below_diagonal, first_query_tile, m_run)

    lam_init = li_ref[0]
    lam = _lambda(lq1_ref, lk1_ref, lq2_ref, lk2_ref, lam_init)
    l_fin = acc_scr[DIFF_V:DIFF_V + 1, :]
    o_t = (acc_scr[0:DIFF_V, 0:tq] / l_fin[:, 0:tq]
           - lam * (acc_scr[0:DIFF_V, tq:2 * tq] / l_fin[:, tq:2 * tq]))
    ms = jnp.mean(o_t * o_t, axis=0, keepdims=True)
    o = jnp.transpose(o_t * lax.rsqrt(ms + SUBLN_EPS))
    gate = gd_ref[...]
    out = o * sub_ref[...] * (1.0 - lam_init) * (gate * _sigmoid(gate))
    o_ref[...] = out.astype(o_ref.dtype)


TOKENS_PADDED = 8
SCORE_ROWS = DIFF_HEADS * 2 * TOKENS_PADDED
PAGE_ROWS = PAGE * DIFF_HEADS
PAGES_PER_SOFTMAX_STEP = 2


def _decode_step(j, last, rb_ref, li_ref, q_ref, kn_ref, vn_ref, gd_ref, lq1_ref, lk1_ref,
                 lq2_ref, lk2_ref, sub_ref, k_refs, v_refs, o_ref, m_scr, acc_scr):
    P = len(k_refs)
    TP = TOKENS_PADDED
    NR = SCORE_ROWS

    @pl.when(j == 0)
    def _():
        m_scr[...] = jnp.full(m_scr.shape, NEG_INF, F32)
        acc_scr[...] = jnp.zeros(acc_scr.shape, F32)

    q = q_ref[...]
    lane = lax.broadcasted_iota(jnp.int32, (TP, DIFF_V), 1)
    q_rows = jnp.concatenate(
        [jnp.where((lane // HEAD) == c, q[:, h * DIFF_V:(h + 1) * DIFF_V], 0.0)
         for h in range(DIFF_HEADS) for c in range(2)], axis=0)
    q_rows = q_rows.astype(BF16).astype(F32)
    row = lax.broadcasted_iota(jnp.int32, (NR, 1), 0)
    row_head = row // (2 * TP)
    row_tok = row % TP

    def head_mask_bias(n, causal=False):
        col = lax.broadcasted_iota(jnp.int32, (1, n), 1)
        keep = (col % DIFF_HEADS) == row_head
        if causal:
            keep = keep & (row_tok >= col // DIFF_HEADS)
        return jnp.where(keep, 0.0, NEG_INF)

    page_mask = head_mask_bias(PAGE_ROWS)

    def scores(k_rows, bias=None, causal=False):
        n = k_rows.shape[0]
        s = lax.dot_general(q_rows, k_rows, (((1,), (1,)), ((), ())),
                            preferred_element_type=F32)
        mask = page_mask if (n == PAGE_ROWS and not causal) else head_mask_bias(n, causal)
        if bias is not None:
            mask = mask + bias
        return s + mask

    def row_bias(n, dist_of):
        col_tok = lax.broadcasted_iota(jnp.int32, (1, n), 1) // DIFF_HEADS
        t_row = lax.broadcasted_iota(jnp.int32, (2 * TP, 1), 0) % TP
        dist = dist_of(t_row, col_tok)
        return jnp.concatenate([_rel_bias_minus_far(dist, rb_ref, h) for h in range(DIFF_HEADS)],
                               axis=0) * LOG2E

    def update(s_list, v_list):
        m_old = m_scr[...]
        m_new = m_old
        for s in s_list:
            m_new = jnp.maximum(m_new, jnp.max(s, axis=-1, keepdims=True))
        acc = jnp.exp2(m_old - m_new) * acc_scr[...]
        for s, v in zip(s_list, v_list):
            v_ext = jnp.concatenate([v, jnp.ones(v.shape, F32)], axis=-1)
            acc = acc + jnp.dot(jnp.exp2(s - m_new), v_ext, preferred_element_type=F32)
        acc_scr[...] = acc
        m_scr[...] = m_new

    page_k = lambda i: k_refs[i][...]
    page_v = lambda i: v_refs[i][...]
    G = min(PAGES_PER_SOFTMAX_STEP, P)

    @pl.when(j != last)
    def _():
        for i0 in range(0, P, G):
            update([scores(page_k(i)) for i in range(i0, i0 + G)],
                   [page_v(i) for i in range(i0, i0 + G)])

    @pl.when(j == last)
    def _():
        bias_p = row_bias(PAGE_ROWS, lambda t_row, col_tok: (PAGE + t_row) - col_tok)
        n_new = TP * DIFF_HEADS
        bias_n = row_bias(n_new, lambda t_row, col_tok: t_row - col_tok)
        for i0 in range(0, P - G, G):
            update([scores(page_k(i)) for i in range(i0, i0 + G)],
                   [page_v(i) for i in range(i0, i0 + G)])
        s_list = [scores(page_k(i)) for i in range(P - G, P - 1)]
        s_list.append(scores(page_k(P - 1), bias=bias_p))
        s_list.append(scores(kn_ref[...], bias=bias_n, causal=True))
        update(s_list, [page_v(i) for i in range(P - G, P)] + [vn_ref[...]])

        lam_init = li_ref[0]
        lam = _lambda(lq1_ref, lk1_ref, lq2_ref, lk2_ref, lam_init)
        gate = gd_ref[...]
        outs = []
        for h in range(DIFF_HEADS):
            r0 = h * 2 * TP
            r1 = r0 + TP
            outs.append(_diff_epilogue(
                acc_scr[r0:r0 + TP, 0:DIFF_V], acc_scr[r0:r0 + TP, DIFF_V:DIFF_V + 1],
                acc_scr[r1:r1 + TP, 0:DIFF_V], acc_scr[r1:r1 + TP, DIFF_V:DIFF_V + 1],
                lam, lam_init, sub_ref[...], gate[:, h * DIFF_V:(h + 1) * DIFF_V]))
        o_ref[...] = jnp.concatenate(outs, axis=-1).astype(o_ref.dtype)


def _attn_fused_kernel(pt_ref, rb_ref, li_ref, qt_ref, k_ref, vt_ref, gdp_ref, lq1_ref, lk1_ref,
                       lq2_ref, lk2_ref, sub_ref, qs_ref, kn_ref, vn_ref, gds_ref, *rest,
                       tq, tk, pages_per_step, groups_per_seq):
    del pt_ref
    P = pages_per_step
    k_pages = rest[0:P]
    v_pages = rest[P:2 * P]
    op_ref, os_ref, bias_scr, s_scr, acc_scr, m_dec, acc_dec = rest[2 * P:]
    h = pl.program_id(1)
    qi = pl.program_id(2)
    step = (pl.program_id(0) * pl.num_programs(1) + h) * pl.num_programs(2) + qi
    lam_refs = (lq1_ref, lk1_ref, lq2_ref, lk2_ref)
    _decode_step(step % groups_per_seq, groups_per_seq - 1, rb_ref, li_ref, qs_ref, kn_ref, vn_ref,
                 gds_ref, *lam_refs, sub_ref, k_pages, v_pages, os_ref, m_dec, acc_dec)
    _prompt_attn_step(h, qi, rb_ref, li_ref, qt_ref, k_ref, vt_ref, gdp_ref, *lam_refs, sub_ref,
                      op_ref, bias_scr, s_scr, acc_scr, tq=tq, tk=tk)


def _attn_fused(qt, kb, vt, gdp, qs, kn, vn, gds, cache_k, cache_v, page_ids, rel_bias, lam_init,
                lam_vecs, subln, batch, tq, tk):
    m = kb.shape[0]
    t = m // batch
    nq = t // tq
    assert tk >= MAX_DISTANCE and tq % tk == 0
    bs = qs.shape[0]
    n_pages = page_ids.shape[0] // bs
    steps = batch * DIFF_HEADS * nq
    P = bs * n_pages // steps
    groups = n_pages // P
    assert P * steps == bs * n_pages and groups * P == n_pages

    step_of = lambda b, h, i: (b * DIFF_HEADS + h) * nq + i
    tile = pl.BlockSpec((tq, DIFF_V), lambda b, h, i, pt: (b * nq + i, h))
    tile_t = pl.BlockSpec((DIFF_V, tq), lambda b, h, i, pt: (h, b * nq + i))
    seq = pl.BlockSpec((t, DIFF_V), lambda b, h, i, pt: (b, h))
    seq_t = pl.BlockSpec((DIFF_V, t), lambda b, h, i, pt: (h, b))
    vec = lambda a: pl.BlockSpec(a.shape, lambda b, h, i, pt: (0,) * a.ndim)
    smem = pl.BlockSpec(memory_space=pltpu.SMEM)
    per_seq = lambda shape: pl.BlockSpec(
        (None,) + shape, lambda b, h, i, pt: (step_of(b, h, i) // groups, 0, 0))
    tok = per_seq((TOKENS_PADDED, DIFF_WIDTH))
    new = per_seq((TOKENS_PADDED * DIFF_HEADS, DIFF_V))

    def page_spec(slot):
        return pl.BlockSpec((None, PAGE_ROWS, DIFF_V),
                            lambda b, h, i, pt: (pt[step_of(b, h, i) * P + slot], 0, 0))

    grid_spec = pltpu.PrefetchScalarGridSpec(
        num_scalar_prefetch=1,
        grid=(batch, DIFF_HEADS, nq),
        in_specs=([smem, smem, tile_t, seq, seq_t, tile] + [vec(a) for a in lam_vecs]
                  + [vec(subln), tok, new, new, tok]
                  + [page_spec(s) for s in range(P)] + [page_spec(s) for s in range(P)]),
        out_specs=(tile, tok),
        scratch_shapes=[pltpu.VMEM((tq // tk + 1, tk, 2 * tq), F32),
                        pltpu.VMEM((2, tk, 2 * tq), F32),
                        pltpu.VMEM((DIFF_V + ONES_ROWS, 2 * tq), F32),
                        pltpu.VMEM((SCORE_ROWS, 1), F32),
                        pltpu.VMEM((SCORE_ROWS, 2 * DIFF_V), F32)],
    )
    return pl.pallas_call(
        functools.partial(_attn_fused_kernel, tq=tq, tk=tk, pages_per_step=P,
                          groups_per_seq=groups),
        grid_spec=grid_spec,
        out_shape=(jax.ShapeDtypeStruct((m, DIFF_WIDTH), BF16),
                   jax.ShapeDtypeStruct((bs, TOKENS_PADDED, DIFF_WIDTH), BF16)),
        compiler_params=pltpu.CompilerParams(
            dimension_semantics=("arbitrary", "arbitrary", "arbitrary"),
            vmem_limit_bytes=VMEM_LIMIT_V7X),
        name="diff_attn",
    )(page_ids, rel_bias, lam_init, qt, kb, vt, gdp, *lam_vecs, subln, qs, kn, vn, gds,
      *([cache_k] * P), *([cache_v] * P))


def _layer_params(l, pre_norm, post_norm, w_in, mu_shift, w0, w_up, a0, a_up, k_k, k_a, r_k,
                  lnx_w, lnx_b, lam_q1, lam_k1, lam_q2, lam_k2, subln, w_out):
    row = lambda a: a.reshape(1, -1)
    zeros = jnp.zeros((LORA, RWKV_WIDTH), F32)
    head_of = jnp.arange(GROUP_W) // HEAD
    return dict(
        pre_norm=row(pre_norm[l]), post_norm=row(post_norm[l]),
        w_in=w_in[l].astype(BF16), w_out=w_out[l].astype(BF16),
        mu=row(mu_shift[l]), w0=row(w0[l]), a0=row(a0[l]),
        w1=jnp.concatenate([w_up[l], zeros], axis=0).astype(BF16),
        w2=jnp.concatenate([zeros, a_up[l]], axis=0).astype(BF16),
        k_k=row(k_k[l]), k_a=row(k_a[l]), r_k=row(r_k[l]),
        lnx_w=row(lnx_w[l]), lnx_b=row(lnx_b[l]),
        e_ones=(head_of[:, None] == head_of[None, :]).astype(BF16),
        lam_vecs=(row(lam_q1[l]), row(lam_k1[l]), row(lam_q2[l]), row(lam_k2[l])),
        subln=row(subln[l]),
        lam_init=jnp.full((1,), 0.8 - 0.6 * math.exp(-0.3 * l), F32),
    )


def _rwkv_stream(zsh, gr, batch, p, shift_prev, s0, chunk, t_valid, seqs_per_step,
                 chunks_per_step=1):
    t = zsh.shape[0] // batch
    return _rwkv(zsh.reshape(batch, t, SHIFT_WIDTH), gr.reshape(batch, t, 512), shift_prev, s0, p,
                 chunk, t_valid, seqs_per_step, chunks_per_step)


def kernel(x_prompt, x_sample, cache_k, cache_v, state_wkv, state_shift, page_table, rel_bias,
           pre_norm, post_norm, w_in, mu_shift, w0, w_up, a0, a_up, k_k, k_a, r_k, lnx_w, lnx_b,
           lam_q1, lam_k1, lam_q2, lam_k2, subln, w_out):
    return _forward(x_prompt, x_sample, cache_k, cache_v, state_wkv, state_shift, page_table,
                    rel_bias, pre_norm, post_norm, w_in, mu_shift, w0, w_up, a0, a_up, k_k, k_a,
                    r_k, lnx_w, lnx_b, lam_q1, lam_k1, lam_q2, lam_k2, subln, w_out,
                    PROMPT_TM=512, PROMPT_TM_OUT=1024, PROMPT_CHUNK=64, PROMPT_TQ=512,
                    PROMPT_TK=256, SAMPLE_SEQS_PER_STEP=8, PROMPT_CHUNKS_PER_STEP=4)


def _forward(x_prompt, x_sample, cache_k, cache_v, state_wkv, state_shift, page_table, rel_bias,
             pre_norm, post_norm, w_in, mu_shift, w0, w_up, a0, a_up, k_k, k_a, r_k, lnx_w, lnx_b,
             lam_q1, lam_k1, lam_q2, lam_k2, subln, w_out, *, PROMPT_TM, PROMPT_TM_OUT,
             PROMPT_CHUNK, PROMPT_TQ, PROMPT_TK, SAMPLE_SEQS_PER_STEP, PROMPT_CHUNKS_PER_STEP):
    bp, tp, d = x_prompt.shape
    bs, ts, _ = x_sample.shape
    depth = w_in.shape[0]
    n_pool = cache_k.shape[1]
    n_pages = page_table.shape[1]
    TS_PAD = TOKENS_PADDED

    xp = x_prompt.reshape(bp * tp, d)
    xs = jnp.pad(x_sample, ((0, 0), (0, TS_PAD - ts), (0, 0))).reshape(bs * TS_PAD, d)
    ck = cache_k.reshape(depth * n_pool, PAGE_ROWS, DIFF_V)
    cv = cache_v.reshape(depth * n_pool, PAGE_ROWS, DIFF_V)
    zero_shift = jnp.zeros((bp, 1, SHIFT_WIDTH), F32)
    zero_state = jnp.zeros((bp, RWKV_HEADS, HEAD, HEAD), F32)

    outs = {name: [] for name in ("wp", "sp", "ks", "vs", "ws", "ss")}
    kv_rows = None
    for l in range(depth):
        p = _layer_params(l, pre_norm, post_norm, w_in, mu_shift, w0, w_up, a0, a_up, k_k, k_a,
                          r_k, lnx_w, lnx_b, lam_q1, lam_k1, lam_q2, lam_k2, subln, w_out)

        zsh_p, gr_p, qt, k_rows, v_rows, gd_p, kb, vt = _inproj_prompt(
            xp, p["pre_norm"], p["w_in"], PROMPT_TM, l, depth, kv_rows)
        kv_rows = (k_rows, v_rows)
        zsh_s, gr_s, q_s, k_s, v_s, gd_s = _inproj_sample(xs, p["pre_norm"], p["w_in"],
                                                          bs * TS_PAD)
        yr_p, wp, sp = _rwkv_stream(zsh_p, gr_p, bp, p, zero_shift, zero_state, PROMPT_CHUNK,
                                    PROMPT_CHUNK, bp, PROMPT_CHUNKS_PER_STEP)
        yr_s, ws, ss = _rwkv_stream(zsh_s, gr_s, bs, p, state_shift[l][:, None, :], state_wkv[l],
                                    TS_PAD, ts, SAMPLE_SEQS_PER_STEP)
        page_ids = (page_table + l * n_pool).reshape(-1).astype(jnp.int32)
        r3 = lambda a: a.reshape(bs, TS_PAD, DIFF_WIDTH)
        rows = lambda a: a.reshape(bs, TS_PAD * DIFF_HEADS, DIFF_V)
        yd_p, yd_s = _attn_fused(qt, kb, vt, gd_p, r3(q_s), rows(k_s), rows(v_s), r3(gd_s), ck, cv,
                                 page_ids, rel_bias, p["lam_init"], p["lam_vecs"], p["subln"], bp,
                                 PROMPT_TQ, PROMPT_TK)
        xp = _outproj(yr_p.reshape(bp * tp, RWKV_WIDTH), yd_p, p["w_out"], p["post_norm"], xp,
                      PROMPT_TM_OUT)
        xs = _outproj(yr_s.reshape(bs * TS_PAD, RWKV_WIDTH), yd_s.reshape(bs * TS_PAD, DIFF_WIDTH),
                      p["w_out"], p["post_norm"], xs, bs * TS_PAD)
        outs["wp"].append(wp)
        outs["sp"].append(sp.reshape(bp, SHIFT_WIDTH))
        outs["ks"].append(k_s.reshape(bs, TS_PAD, DIFF_HEADS, DIFF_V)[:, :ts])
        outs["vs"].append(v_s.reshape(bs, TS_PAD, DIFF_HEADS, DIFF_V)[:, :ts])
        outs["ws"].append(ws)
        outs["ss"].append(ss.reshape(bs, SHIFT_WIDTH))

    st = lambda name: jnp.stack(outs[name])
    kv_shape = (depth, bp, tp, DIFF_HEADS, DIFF_V)
    return (xp.reshape(bp, tp, d), xs.reshape(bs, TS_PAD, d)[:, :ts], kv_rows[0].reshape(kv_shape),
            kv_rows[1].reshape(kv_shape), st("wp"), st("sp"), st("ks"), st("vs"), st("ws"),
            st("ss"))
```

```python
import functools
import math

import jax
import jax.numpy as jnp
from jax import lax
from jax.experimental import pallas as pl
from jax.experimental.pallas import tpu as pltpu

F32 = jnp.float32
BF16 = jnp.bfloat16

RWKV_WIDTH = 512
DIFF_WIDTH = 512
HEAD = 64
RWKV_HEADS = RWKV_WIDTH // HEAD
LORA = 64
DIFF_HEADS = 4
DIFF_V = 128
NUM_BUCKETS = 32
MAX_EXACT = NUM_BUCKETS // 2
MAX_DISTANCE = 128
PAGE = 128
RMS_EPS = 1e-6
SUBLN_EPS = 1e-5
GN_EPS = 64e-5
NEG_INF = -1e30
LOG2E = 1.4426950408889634
SHIFT_WIDTH = 3 * RWKV_WIDTH + 2 * LORA
SEG_SHIFT = (0, SHIFT_WIDTH)
SEG_GR = (SHIFT_WIDTH, SHIFT_WIDTH + 512)
SEG_Q = (SEG_GR[1], SEG_GR[1] + 512)
SEG_K = (SEG_Q[1], SEG_Q[1] + 512)
SEG_V = (SEG_K[1], SEG_K[1] + 512)
SEG_GD = (SEG_V[1], SEG_V[1] + 512)

VMEM_LIMIT_V7X = 48 * 1024 * 1024
LANES = 128
ONES_ROWS = 16
GROUP_HEADS = 4
GROUP_W = GROUP_HEADS * HEAD
N_GROUPS = RWKV_HEADS // GROUP_HEADS


def _dot(a, b):
    return jnp.dot(a.astype(BF16), b.astype(BF16), preferred_element_type=F32)


def _dot_nt(a, b):
    return lax.dot_general(a.astype(BF16), b.astype(BF16), (((1,), (1,)), ((), ())),
                           preferred_element_type=F32)


def _dot_tn(a, b):
    return lax.dot_general(a.astype(BF16), b.astype(BF16), (((0,), (0,)), ((), ())),
                           preferred_element_type=F32)


def _split(x):
    hi = x.astype(BF16)
    lo = (x - hi.astype(F32)).astype(BF16)
    return hi, lo


def _dot_exact_lhs(a_exact, b):
    hi, lo = _split(b)
    return (jnp.dot(a_exact, hi, preferred_element_type=F32)
            + jnp.dot(a_exact, lo, preferred_element_type=F32))


def _sigmoid(x):
    return 1.0 / (1.0 + jnp.exp(-x))


def _rel_bias_minus_far(dist, rb_ref, head):
    n = jnp.maximum(dist, 0)
    nf = jnp.maximum(n, 1).astype(F32)
    large = MAX_EXACT + (jnp.log(nf / MAX_EXACT) / math.log(MAX_DISTANCE / MAX_EXACT)
                         * (NUM_BUCKETS - MAX_EXACT)).astype(jnp.int32)
    large = jnp.minimum(large, NUM_BUCKETS - 1)
    bucket = jnp.where(n < MAX_EXACT, n, large)
    far = rb_ref[NUM_BUCKETS - 1, head]
    out = jnp.zeros(dist.shape, F32)
    for b in range(NUM_BUCKETS - 1):
        out = jnp.where(bucket == b, rb_ref[b, head] - far, out)
    return out


def _projected_segments(x_ref, g_ref, w_ref):
    x = x_ref[...]
    ms = jnp.mean(x * x, axis=-1, keepdims=True)
    h = (x * lax.rsqrt(ms + RMS_EPS) * g_ref[...]).astype(BF16)

    def seg(lo_hi):
        lo, hi = lo_hi
        return jnp.dot(h, w_ref[:, lo:hi], preferred_element_type=F32)

    return seg


Q_SCALE = LOG2E * HEAD ** -0.5


def _inproj_sample_kernel(x_ref, g_ref, w_ref, zsh_ref, gr_ref, q_ref, k_ref, v_ref, gd_ref):
    seg = _projected_segments(x_ref, g_ref, w_ref)
    zsh_ref[...] = seg(SEG_SHIFT)
    gr_ref[...] = seg(SEG_GR)
    q_ref[...] = seg(SEG_Q) * Q_SCALE
    k_ref[...] = seg(SEG_K)
    v_ref[...] = seg(SEG_V)
    gd_ref[...] = seg(SEG_GD)


def _inproj_prompt_kernel(x_ref, g_ref, w_ref, *refs):
    zsh_ref, gr_ref, qt_ref, krows_ref, vrows_ref, gd_ref, kb_ref, vt_ref = refs[-8:]
    seg = _projected_segments(x_ref, g_ref, w_ref)
    tm = x_ref.shape[0]
    zsh_ref[...] = seg(SEG_SHIFT)
    gr_ref[...] = seg(SEG_GR)
    qt_ref[...] = jnp.transpose(seg(SEG_Q) * Q_SCALE).astype(BF16)
    gd_ref[...] = seg(SEG_GD)
    k = seg(SEG_K)
    v = seg(SEG_V)
    kb_ref[...] = k.astype(BF16)
    vt_ref[...] = jnp.transpose(v).astype(BF16)
    for hd in range(DIFF_HEADS):
        cols = slice(hd * DIFF_V, (hd + 1) * DIFF_V)
        krows_ref[pl.ds(hd, tm, stride=DIFF_HEADS), :] = k[:, cols]
        vrows_ref[pl.ds(hd, tm, stride=DIFF_HEADS), :] = v[:, cols]


def _inproj_sample(x, g, w_bf16, tm):
    m, d = x.shape
    row = lambda width: pl.BlockSpec((tm, width), lambda i: (i, 0))
    f32 = jax.ShapeDtypeStruct((m, 512), F32)
    return pl.pallas_call(
        _inproj_sample_kernel,
        grid=(m // tm,),
        in_specs=[row(d), pl.BlockSpec((1, d), lambda i: (0, 0)),
                  pl.BlockSpec(w_bf16.shape, lambda i: (0, 0))],
        out_specs=(row(SHIFT_WIDTH), row(512), row(512), row(512), row(512), row(512)),
        out_shape=(jax.ShapeDtypeStruct((m, SHIFT_WIDTH), F32), f32, f32, f32, f32, f32),
        compiler_params=pltpu.CompilerParams(dimension_semantics=("arbitrary",),
                                             vmem_limit_bytes=VMEM_LIMIT_V7X),
        name="inproj_sample",
    )(x, g, w_bf16)


def _inproj_prompt(x, g, w_bf16, tm, layer, depth, kv_rows):
    m, d = x.shape
    steps = m // tm
    row = lambda width: pl.BlockSpec((tm, width), lambda i: (i, 0))
    col = pl.BlockSpec((512, tm), lambda i: (0, i))
    rows = pl.BlockSpec((tm * DIFF_HEADS, DIFF_V), lambda i: (layer * steps + i, 0))
    f32 = jax.ShapeDtypeStruct((m, 512), F32)
    bf_t = jax.ShapeDtypeStruct((512, m), BF16)
    rows_shape = jax.ShapeDtypeStruct((depth * m * DIFF_HEADS, DIFF_V), F32)
    in_specs = [row(d), pl.BlockSpec((1, d), lambda i: (0, 0)),
                pl.BlockSpec(w_bf16.shape, lambda i: (0, 0), pipeline_mode=pl.Buffered(1))]
    operands = [x, g, w_bf16]
    aliases = {}
    if kv_rows is not None:
        in_specs += [pl.BlockSpec(memory_space=pl.ANY)] * 2
        operands += list(kv_rows)
        aliases = {3: 3, 4: 4}
    return pl.pallas_call(
        _inproj_prompt_kernel,
        grid=(steps,),
        in_specs=in_specs,
        out_specs=(row(SHIFT_WIDTH), row(512), col, rows, rows, row(512), row(512), col),
        out_shape=(jax.ShapeDtypeStruct((m, SHIFT_WIDTH), F32), f32, bf_t, rows_shape, rows_shape,
                   f32, jax.ShapeDtypeStruct((m, 512), BF16), bf_t),
        input_output_aliases=aliases,
        compiler_params=pltpu.CompilerParams(dimension_semantics=("arbitrary",),
                                             vmem_limit_bytes=VMEM_LIMIT_V7X),
        name="inproj_prompt",
    )(*operands)


def _outproj_kernel(yr_ref, yd_ref, w_ref, g_ref, x_ref, o_ref):
    out = (jnp.dot(yr_ref[...], w_ref[0:RWKV_WIDTH, :], preferred_element_type=F32)
           + jnp.dot(yd_ref[...], w_ref[RWKV_WIDTH:, :], preferred_element_type=F32))
    ms = jnp.mean(out * out, axis=-1, keepdims=True)
    o_ref[...] = x_ref[...] + out * lax.rsqrt(ms + RMS_EPS) * g_ref[...]


def _outproj(yr, yd, w_bf16, g, x, tm):
    m, d = x.shape
    row = lambda width: pl.BlockSpec((tm, width), lambda i: (i, 0))
    return pl.pallas_call(
        _outproj_kernel,
        grid=(m // tm,),
        in_specs=[row(512), row(512), pl.BlockSpec(w_bf16.shape, lambda i: (0, 0)),
                  pl.BlockSpec((1, d), lambda i: (0, 0)), row(d)],
        out_specs=row(d),
        out_shape=jax.ShapeDtypeStruct((m, d), F32),
        compiler_params=pltpu.CompilerParams(dimension_semantics=("arbitrary",),
                                             vmem_limit_bytes=VMEM_LIMIT_V7X),
        name="outproj",
    )(yr, yd, w_bf16, g, x)


def _rwkv_kernel(zsh_ref, gr_ref, sprev_ref, s0_ref, mu_ref, w0_ref, w1_ref, a0_ref, w2_ref,
                 kk_ref, ka_ref, rk_ref, lnw_ref, lnb_ref, e_ref,
                 y_ref, sout_ref, shift_ref, s_scr, carry_scr, *, chunk, t_valid):
    C = chunk
    NB = zsh_ref.shape[0]
    R = GROUP_HEADS * C
    c_idx = pl.program_id(1)
    last = pl.num_programs(1) - 1

    ii = lax.broadcasted_iota(jnp.int32, (GROUP_W, GROUP_W), 0)
    jj = lax.broadcasted_iota(jnp.int32, (GROUP_W, GROUP_W), 1)
    eye_g = jnp.where(ii == jj, 1.0, 0.0).astype(BF16)
    block_g = (ii // HEAD) == (jj // HEAD)

    def exact_nt(lhs_exact, x):
        hi = x.astype(BF16)
        r1 = x - hi.astype(F32)
        mid = r1.astype(BF16)
        lo = (r1 - mid.astype(F32)).astype(BF16)
        return _dot_nt(lhs_exact, hi) + _dot_nt(lhs_exact, mid) + _dot_nt(lhs_exact, lo)

    @pl.when(c_idx == 0)
    def _():
        carry_scr[...] = sprev_ref[...]
        for b in range(NB):
            for g in range(N_GROUPS):
                heads = s0_ref[b, g * GROUP_HEADS:(g + 1) * GROUP_HEADS]
                s_t = exact_nt(eye_g[0:HEAD, 0:HEAD], heads.reshape(GROUP_W, HEAD))
                s_scr[b, g] = jnp.where(block_g, jnp.concatenate([s_t] * GROUP_HEADS, axis=0),
                                        0.0)

    CH = zsh_ref.shape[1] // C
    NS = NB * CH
    assert CH == 1 or t_valid == C
    z = zsh_ref[...].reshape(NS * C, SHIFT_WIDTH)
    row = lax.broadcasted_iota(jnp.int32, (NS * C, 1), 0)
    tok = row % C
    prev = pltpu.roll(z, 1, axis=0)
    last_rows = []
    for b in range(NB):
        prev = jnp.where(row == b * CH * C, carry_scr[b], prev)
        last_tok = b * CH * C + (CH - 1) * C + t_valid - 1
        last_rows.append(z[last_tok:last_tok + 1, :])
        carry_scr[b] = last_rows[b]
    zm = z + mu_ref[...] * (prev - z)

    r = zm[:, 0:512]
    k = zm[:, 512:1024]
    v = zm[:, 1024:1536]
    zwa = zm[:, 1536:1664]
    lora_w = _dot(jnp.tanh(zwa), w1_ref[...])
    lora_a = _dot(zwa, w2_ref[...])
    logw = -math.exp(-0.5) * _sigmoid(w0_ref[...] + lora_w)
    alr = _sigmoid(a0_ref[...] + lora_a)

    e_ones = e_ref[...]

    def head_sum(x):
        return jnp.concatenate(
            [jnp.dot(x[:, g * GROUP_W:(g + 1) * GROUP_W].astype(BF16), e_ones,
                     preferred_element_type=F32) for g in range(N_GROUPS)], axis=1)

    kk = k * kk_ref[...]
    kk = kk * lax.rsqrt(jnp.maximum(head_sum(kk * kk), 1e-24))
    k2 = k * (1.0 + (alr - 1.0) * ka_ref[...])
    a_vec = -kk
    b_vec = kk * alr
    if t_valid < C:
        valid = tok < t_valid
        logw = jnp.where(valid, logw, 0.0)
        b_vec = jnp.where(valid, b_vec, 0.0)
        k2 = jnp.where(valid, k2, 0.0)

    ti = lax.broadcasted_iota(jnp.int32, (NS * C, NS * C), 0)
    si = lax.broadcasted_iota(jnp.int32, (NS * C, NS * C), 1)
    ltri = jnp.where(((ti // C) == (si // C)) & (ti >= si), 1.0, 0.0).astype(BF16)
    cum = _dot_exact_lhs(ltri, logw)
    ew_incl = jnp.exp(cum)
    ew_excl = jnp.exp(cum - logw)
    e_inv = jnp.exp(-cum)
    a_t = a_vec * ew_excl
    r_t = r * ew_incl
    b_t = b_vec * e_inv
    k_t = k2 * e_inv

    ri = lax.broadcasted_iota(jnp.int32, (R, GROUP_W), 0)
    li = lax.broadcasted_iota(jnp.int32, (R, GROUP_W), 1)
    one_zero = lambda m: jnp.where(m, 1.0, 0.0).astype(BF16)
    head_mask = one_zero((ri // C) == (li // HEAD))
    wi = lax.broadcasted_iota(jnp.int32, (R, R), 0)
    wj = lax.broadcasted_iota(jnp.int32, (R, R), 1)
    wide_mask = one_zero((wi // C) == (wj // C))
    tt = lax.broadcasted_iota(jnp.int32, (C, R), 0)
    ss = lax.broadcasted_iota(jnp.int32, (C, R), 1) % C
    strict = tt > ss
    incl = tt >= ss
    eye_w = jnp.where(tt == ss, 1.0, 0.0)
    di = lax.broadcasted_iota(jnp.int32, (GROUP_W, GROUP_W), 0)
    dj = lax.broadcasted_iota(jnp.int32, (GROUP_W, GROUP_W), 1)
    state_mask = (di // HEAD) == (dj // HEAD)

    def bd(x, mask):
        if C % 16 == 0:
            return jnp.concatenate([x.astype(BF16)] * GROUP_HEADS, axis=0) * mask
        return (jnp.concatenate([x] * GROUP_HEADS, axis=0) * mask.astype(F32)).astype(BF16)

    def dot_cat(lhs, rhs):
        if all(a.shape[1] % LANES == 0 for a in lhs):
            return _dot(jnp.concatenate(lhs, axis=1), jnp.concatenate(rhs, axis=0))
        return sum(_dot(a, b) for a, b in zip(lhs, rhs))

    n_levels = int(math.log2(C))

    ops = {}
    for b in range(NB):
        for ch in range(CH):
            for g in range(N_GROUPS):
                r0 = (b * CH + ch) * C
                rs = slice(r0, r0 + C)
                ls = slice(g * GROUP_W, (g + 1) * GROUP_W)
                g_c = ew_incl[r0 + C - 1:r0 + C, ls]
                ops[b, ch, g] = dict(
                    a=a_t[rs, ls].astype(BF16), r=r_t[rs, ls].astype(BF16),
                    bk_hat=jnp.concatenate([b_t[rs, ls] * g_c, k_t[rs, ls] * g_c], axis=0),
                    b_bd=bd(b_t[rs, ls], head_mask), k_bd=bd(k_t[rs, ls], head_mask),
                    v=v[rs, ls], v_bd=bd(v[rs, ls], head_mask), g_c=g_c)
    for o in ops.values():
        ar = jnp.concatenate([o["a"], o["r"]], axis=0)
        nb = _dot_nt(ar, o["b_bd"])
        nk = _dot_nt(ar, o["k_bd"])
        o["n_ab"] = jnp.where(strict, nb[0:C], 0.0)
        o["m_rb"] = jnp.where(incl, nb[C:2 * C], 0.0)
        o["n_ak"] = jnp.where(strict, nk[0:C], 0.0)
        o["m_rk"] = jnp.where(incl, nk[C:2 * C], 0.0)
    for o in ops.values():
        o["t_inv"] = eye_w + o["n_ab"]
        o["p"] = _dot(o["n_ab"], bd(o["n_ab"], wide_mask))
    for j in range(1, n_levels - 1):
        for o in ops.values():
            both = _dot(jnp.concatenate([o["p"], o["t_inv"]], axis=0), bd(o["p"], wide_mask))
            o["p"] = both[0:C]
            o["t_inv"] = o["t_inv"] + both[C:2 * C]
    for o in ops.values():
        o["t_inv"] = o["t_inv"] + _dot(o["t_inv"], bd(o["p"], wide_mask))

    for ch in range(CH):
        now = [(b, g, ops[b, ch, g]) for b in range(NB) for g in range(N_GROUPS)]
        for b, g, o in now:
            o["s_prev"] = s_scr[b, g]
            o["s_bf"] = o["s_prev"].astype(BF16)
            o["rhs"] = dot_cat([o["a"], o["n_ak"]], [o["s_bf"], o["v_bd"]])
        for b, g, o in now:
            o["u"] = _dot(o["t_inv"], bd(o["rhs"], head_mask))
        for b, g, o in now:
            o["y"] = dot_cat([o["r"], o["m_rb"], o["m_rk"]],
                             [o["s_bf"], bd(o["u"], head_mask), o["v_bd"]])
        for b, g, o in now:
            g_col = jnp.transpose(jnp.broadcast_to(o["g_c"], (GROUP_W, GROUP_W)))
            upd = _dot_tn(o["bk_hat"], jnp.concatenate([o["u"], o["v"]], axis=0))
            s_scr[b, g] = o["s_prev"] * g_col + jnp.where(state_mask, upd, 0.0)

    y = jnp.concatenate(
        [jnp.concatenate([ops[b, ch, g]["y"] for g in range(N_GROUPS)], axis=-1)
         for b in range(NB) for ch in range(CH)], axis=0)
    inv_n = 1.0 / HEAD
    mean = head_sum(y) * inv_n
    d = y - mean
    var = head_sum(d * d) * inv_n
    yn = d * lax.rsqrt(var + GN_EPS) * lnw_ref[...] + lnb_ref[...]
    bonus = head_sum(r * k2 * rk_ref[...]) * v
    gate = gr_ref[...].reshape(NS * C, RWKV_WIDTH)
    out = (yn + bonus) * (gate * _sigmoid(gate))
    y_ref[...] = out.reshape(NB, CH * C, RWKV_WIDTH).astype(y_ref.dtype)

    @pl.when(c_idx == last)
    def _():
        for b in range(NB):
            shift_ref[b] = last_rows[b]
            for g in range(N_GROUPS):
                s_bd = s_scr[b, g]
                s_t = s_bd[0:HEAD]
                for h in range(1, GROUP_HEADS):
                    s_t = s_t + s_bd[h * HEAD:(h + 1) * HEAD]
                heads = exact_nt(eye_g, s_t)
                sout_ref[b, g * GROUP_HEADS:(g + 1) * GROUP_HEADS] = heads.reshape(
                    GROUP_HEADS, HEAD, HEAD)


def _rwkv(zsh, gr, shift_prev, s0, p, chunk, t_valid, seqs_per_step, chunks_per_step):
    b, t, _ = zsh.shape
    step_tokens = chunk * chunks_per_step
    nc = t // step_tokens
    nb = seqs_per_step
    tok = lambda width: pl.BlockSpec((nb, step_tokens, width), lambda i, c: (i, c, 0))
    per_b = lambda shape: pl.BlockSpec((nb,) + shape, lambda i, c: (i,) + (0,) * len(shape))
    const = lambda a: pl.BlockSpec(a.shape, lambda i, c: (0,) * a.ndim)
    params = (p["mu"], p["w0"], p["w1"], p["a0"], p["w2"], p["k_k"], p["k_a"], p["r_k"],
              p["lnx_w"], p["lnx_b"], p["e_ones"])
    return pl.pallas_call(
        functools.partial(_rwkv_kernel, chunk=chunk, t_valid=t_valid),
        grid=(b // nb, nc),
        in_specs=[tok(SHIFT_WIDTH), tok(512), per_b((1, SHIFT_WIDTH)),
                  per_b((RWKV_HEADS, HEAD, HEAD))] + [const(a) for a in params],
        out_specs=(tok(512), per_b((RWKV_HEADS, HEAD, HEAD)), per_b((1, SHIFT_WIDTH))),
        out_shape=(jax.ShapeDtypeStruct((b, t, 512), BF16),
                   jax.ShapeDtypeStruct((b, RWKV_HEADS, HEAD, HEAD), F32),
                   jax.ShapeDtypeStruct((b, 1, SHIFT_WIDTH), F32)),
        scratch_shapes=[pltpu.VMEM((nb, N_GROUPS, GROUP_W, GROUP_W), F32),
                        pltpu.VMEM((nb, 1, SHIFT_WIDTH), F32)],
        compiler_params=pltpu.CompilerParams(dimension_semantics=("arbitrary", "arbitrary"),
                                             vmem_limit_bytes=VMEM_LIMIT_V7X),
        name="rwkv7",
    )(zsh, gr, shift_prev, s0, *params)


def _lambda(lq1_ref, lk1_ref, lq2_ref, lk2_ref, lam_init):
    s1 = jnp.sum(lq1_ref[...] * lk1_ref[...], axis=-1, keepdims=True)
    s2 = jnp.sum(lq2_ref[...] * lk2_ref[...], axis=-1, keepdims=True)
    return jnp.exp(s1) - jnp.exp(s2) + lam_init


def _diff_epilogue(acc0, l0, acc1, l1, lam, lam_init, subln, gate):
    o = acc0 / l0 - lam * (acc1 / l1)
    ms = jnp.mean(o * o, axis=-1, keepdims=True)
    on = o * lax.rsqrt(ms + SUBLN_EPS) * subln * (1.0 - lam_init)
    return on * (gate * _sigmoid(gate))


def _prompt_attn_step(h, qi, rb_ref, li_ref, qt_ref, k_ref, vt_ref, gd_ref, lq1_ref, lk1_ref,
                      lq2_ref, lk2_ref, sub_ref, o_ref, bias_scr, s_scr, acc_scr, *, tq, tk):
    nsub = tq // tk
    first_near = nsub * qi - 1

    @pl.when(qi == 0)
    def _():
        key = lax.broadcasted_iota(jnp.int32, (tk, tq), 0)
        qry = lax.broadcasted_iota(jnp.int32, (tk, tq), 1)
        for i in range(nsub + 1):
            dist = qry - key - (i - 1) * tk
            bias = _rel_bias_minus_far(dist, rb_ref, h) * LOG2E
            bias = jnp.where(dist >= 0, bias, NEG_INF)
            bias_scr[i] = jnp.concatenate([bias, bias], axis=1)

    qt = qt_ref[...]
    feat = lax.broadcasted_iota(jnp.int32, (2 * HEAD, tq), 0)
    qt2 = jnp.concatenate([jnp.where(feat < HEAD, qt, jnp.zeros_like(qt)),
                           jnp.where(feat >= HEAD, qt, jnp.zeros_like(qt))], axis=1)

    acc_scr[...] = jnp.zeros(acc_scr.shape, F32)
    ones_rows = jnp.ones((ONES_ROWS, tk), BF16)

    def logits(kj):
        off = pl.multiple_of(kj * tk, tk)
        return jnp.dot(k_ref[pl.ds(off, tk), :], qt2, preferred_element_type=F32)

    def absorb(s, kj, m_old):
        off = pl.multiple_of(kj * tk, tk)
        vt_ext = jnp.concatenate([vt_ref[:, pl.ds(off, tk)], ones_rows], axis=0)
        m_new = jnp.maximum(m_old, jnp.max(s, axis=0, keepdims=True))
        pm = jnp.exp2(s - m_new).astype(BF16)
        acc_scr[...] = (jnp.exp2(m_old - m_new) * acc_scr[...]
                        + jnp.dot(vt_ext, pm, preferred_element_type=F32))
        return m_new

    assert nsub == 2
    m_run = jnp.full((1, 2 * tq), NEG_INF, F32)

    @pl.when(qi >= 1)
    def _():
        s_scr[0] = logits(0)

    def far_pair(i, m):
        s_scr[1] = logits(2 * i + 1)
        m = absorb(s_scr[0], 2 * i, m)
        s_scr[0] = logits(2 * i + 2)
        return absorb(s_scr[1], 2 * i + 1, m)

    m_run = lax.fori_loop(0, qi - 1, far_pair, m_run)

    def finish_below_diagonal(m):
        s_scr[1] = logits(first_near)
        m = absorb(s_scr[0], first_near - 1, m)
        s_scr[0] = logits(first_near + 1)
        m = absorb(s_scr[1] + bias_scr[0], first_near, m)
        s_scr[1] = logits(first_near + 2)
        m = absorb(s_scr[0] + bias_scr[1], first_near + 1, m)
        return absorb(s_scr[1] + bias_scr[2], first_near + 2, m)

    def first_query_tile(m):
        s_scr[0] = logits(0)
        s_scr[1] = logits(1)
        m = absorb(s_scr[0] + bias_scr[1], 0, m)
        return absorb(s_scr[1] + bias_scr[2], 1, m)

    lax.cond(qi >= 1, finish_below_diagonal, first_query_tile, m_run)

    lam_init = li_ref[0]
    lam = _lambda(lq1_ref, lk1_ref, lq2_ref, lk2_ref, lam_init)
    l_fin = acc_scr[DIFF_V:DIFF_V + 1, :]
    o_t = (acc_scr[0:DIFF_V, 0:tq] / l_fin[:, 0:tq]
           - lam * (acc_scr[0:DIFF_V, tq:2 * tq] / l_fin[:, tq:2 * tq]))
    ms = jnp.mean(o_t * o_t, axis=0, keepdims=True)
    o = jnp.transpose(o_t * lax.rsqrt(ms + SUBLN_EPS))
    gate = gd_ref[...]
    out = o * sub_ref[...] * (1.0 - lam_init) * (gate * _sigmoid(gate))
    o_ref[...] = out.astype(o_ref.dtype)


TOKENS_PADDED = 8
SCORE_ROWS = DIFF_HEADS * 2 * TOKENS_PADDED
PAGE_ROWS = PAGE * DIFF_HEADS
PAGES_PER_SOFTMAX_STEP = 2


def _decode_step(j, last, rb_ref, li_ref, q_ref, kn_ref, vn_ref, gd_ref, lq1_ref, lk1_ref,
                 lq2_ref, lk2_ref, sub_ref, k_refs, v_refs, o_ref, m_scr, acc_scr):
    P = len(k_refs)
    TP = TOKENS_PADDED
    NR = SCORE_ROWS

    @pl.when(j == 0)
    def _():
        m_scr[...] = jnp.full(m_scr.shape, NEG_INF, F32)
        acc_scr[...] = jnp.zeros(acc_scr.shape, F32)

    q = q_ref[...]
    lane = lax.broadcasted_iota(jnp.int32, (TP, DIFF_V), 1)
    q_rows = jnp.concatenate(
        [jnp.where((lane // HEAD) == c, q[:, h * DIFF_V:(h + 1) * DIFF_V], 0.0)
         for h in range(DIFF_HEADS) for c in range(2)], axis=0)
    q_rows = q_rows.astype(BF16).astype(F32)
    row = lax.broadcasted_iota(jnp.int32, (NR, 1), 0)
    row_head = row // (2 * TP)
    row_tok = row % TP

    def scores(k_rows, bias=None, causal=False):
        n = k_rows.shape[0]
        s = lax.dot_general(q_rows, k_rows, (((1,), (1,)), ((), ())),
                            preferred_element_type=F32)
        col = lax.broadcasted_iota(jnp.int32, (1, n), 1)
        keep = (col % DIFF_HEADS) == row_head
        if causal:
            keep = keep & (row_tok >= col // DIFF_HEADS)
        if bias is not None:
            s = s + bias
        return jnp.where(keep, s, NEG_INF)

    def row_bias(n, dist_of):
        col_tok = lax.broadcasted_iota(jnp.int32, (1, n), 1) // DIFF_HEADS
        t_row = lax.broadcasted_iota(jnp.int32, (2 * TP, 1), 0) % TP
        dist = dist_of(t_row, col_tok)
        return jnp.concatenate([_rel_bias_minus_far(dist, rb_ref, h) for h in range(DIFF_HEADS)],
                               axis=0) * LOG2E

    def update(s_list, v_list):
        m_old = m_scr[...]
        m_new = m_old
        for s in s_list:
            m_new = jnp.maximum(m_new, jnp.max(s, axis=-1, keepdims=True))
        acc = jnp.exp2(m_old - m_new) * acc_scr[...]
        for s, v in zip(s_list, v_list):
            v_ext = jnp.concatenate([v, jnp.ones(v.shape, F32)], axis=-1)
            acc = acc + jnp.dot(jnp.exp2(s - m_new), v_ext, preferred_element_type=F32)
        acc_scr[...] = acc
        m_scr[...] = m_new

    page_k = lambda i: k_refs[i][...]
    page_v = lambda i: v_refs[i][...]
    G = min(PAGES_PER_SOFTMAX_STEP, P)

    @pl.when(j != last)
    def _():
        for i0 in range(0, P, G):
            update([scores(page_k(i)) for i in range(i0, i0 + G)],
                   [page_v(i) for i in range(i0, i0 + G)])

    @pl.when(j == last)
    def _():
        bias_p = row_bias(PAGE_ROWS, lambda t_row, col_tok: (PAGE + t_row) - col_tok)
        n_new = TP * DIFF_HEADS
        bias_n = row_bias(n_new, lambda t_row, col_tok: t_row - col_tok)
        for i0 in range(0, P - G, G):
            update([scores(page_k(i)) for i in range(i0, i0 + G)],
                   [page_v(i) for i in range(i0, i0 + G)])
        s_list = [scores(page_k(i)) for i in range(P - G, P - 1)]
        s_list.append(scores(page_k(P - 1), bias=bias_p))
        s_list.append(scores(kn_ref[...], bias=bias_n, causal=True))
        update(s_list, [page_v(i) for i in range(P - G, P)] + [vn_ref[...]])

        lam_init = li_ref[0]
        lam = _lambda(lq1_ref, lk1_ref, lq2_ref, lk2_ref, lam_init)
        gate = gd_ref[...]
        outs = []
        for h in range(DIFF_HEADS):
            r0 = h * 2 * TP
            r1 = r0 + TP
            outs.append(_diff_epilogue(
                acc_scr[r0:r0 + TP, 0:DIFF_V], acc_scr[r0:r0 + TP, DIFF_V:DIFF_V + 1],
                acc_scr[r1:r1 + TP, 0:DIFF_V], acc_scr[r1:r1 + TP, DIFF_V:DIFF_V + 1],
                lam, lam_init, sub_ref[...], gate[:, h * DIFF_V:(h + 1) * DIFF_V]))
        o_ref[...] = jnp.concatenate(outs, axis=-1).astype(o_ref.dtype)


def _attn_fused_kernel(pt_ref, rb_ref, li_ref, qt_ref, k_ref, vt_ref, gdp_ref, lq1_ref, lk1_ref,
                       lq2_ref, lk2_ref, sub_ref, qs_ref, kn_ref, vn_ref, gds_ref, *rest,
                       tq, tk, pages_per_step, groups_per_seq):
    del pt_ref
    P = pages_per_step
    k_pages = rest[0:P]
    v_pages = rest[P:2 * P]
    op_ref, os_ref, bias_scr, s_scr, acc_scr, m_dec, acc_dec = rest[2 * P:]
    h = pl.program_id(1)
    qi = pl.program_id(2)
    step = (pl.program_id(0) * pl.num_programs(1) + h) * pl.num_programs(2) + qi
    lam_refs = (lq1_ref, lk1_ref, lq2_ref, lk2_ref)
    _decode_step(step % groups_per_seq, groups_per_seq - 1, rb_ref, li_ref, qs_ref, kn_ref, vn_ref,
                 gds_ref, *lam_refs, sub_ref, k_pages, v_pages, os_ref, m_dec, acc_dec)
    _prompt_attn_step(h, qi, rb_ref, li_ref, qt_ref, k_ref, vt_ref, gdp_ref, *lam_refs, sub_ref,
                      op_ref, bias_scr, s_scr, acc_scr, tq=tq, tk=tk)


def _attn_fused(qt, kb, vt, gdp, qs, kn, vn, gds, cache_k, cache_v, page_ids, rel_bias, lam_init,
                lam_vecs, subln, batch, tq, tk):
    m = kb.shape[0]
    t = m // batch
    nq = t // tq
    assert tk >= MAX_DISTANCE and tq % tk == 0
    bs = qs.shape[0]
    n_pages = page_ids.shape[0] // bs
    steps = batch * DIFF_HEADS * nq
    P = bs * n_pages // steps
    groups = n_pages // P
    assert P * steps == bs * n_pages and groups * P == n_pages

    step_of = lambda b, h, i: (b * DIFF_HEADS + h) * nq + i
    tile = pl.BlockSpec((tq, DIFF_V), lambda b, h, i, pt: (b * nq + i, h))
    tile_t = pl.BlockSpec((DIFF_V, tq), lambda b, h, i, pt: (h, b * nq + i))
    seq = pl.BlockSpec((t, DIFF_V), lambda b, h, i, pt: (b, h))
    seq_t = pl.BlockSpec((DIFF_V, t), lambda b, h, i, pt: (h, b))
    vec = lambda a: pl.BlockSpec(a.shape, lambda b, h, i, pt: (0,) * a.ndim)
    smem = pl.BlockSpec(memory_space=pltpu.SMEM)
    per_seq = lambda shape: pl.BlockSpec(
        (None,) + shape, lambda b, h, i, pt: (step_of(b, h, i) // groups, 0, 0))
    tok = per_seq((TOKENS_PADDED, DIFF_WIDTH))
    new = per_seq((TOKENS_PADDED * DIFF_HEADS, DIFF_V))

    def page_spec(slot):
        return pl.BlockSpec((None, PAGE_ROWS, DIFF_V),
                            lambda b, h, i, pt: (pt[step_of(b, h, i) * P + slot], 0, 0))

    grid_spec = pltpu.PrefetchScalarGridSpec(
        num_scalar_prefetch=1,
        grid=(batch, DIFF_HEADS, nq),
        in_specs=([smem, smem, tile_t, seq, seq_t, tile] + [vec(a) for a in lam_vecs]
                  + [vec(subln), tok, new, new, tok]
                  + [page_spec(s) for s in range(P)] + [page_spec(s) for s in range(P)]),
        out_specs=(tile, tok),
        scratch_shapes=[pltpu.VMEM((tq // tk + 1, tk, 2 * tq), F32),
                        pltpu.VMEM((2, tk, 2 * tq), F32),
                        pltpu.VMEM((DIFF_V + ONES_ROWS, 2 * tq), F32),
                        pltpu.VMEM((SCORE_ROWS, 1), F32),
                        pltpu.VMEM((SCORE_ROWS, 2 * DIFF_V), F32)],
    )
    return pl.pallas_call(
        functools.partial(_attn_fused_kernel, tq=tq, tk=tk, pages_per_step=P,
                          groups_per_seq=groups),
        grid_spec=grid_spec,
        out_shape=(jax.ShapeDtypeStruct((m, DIFF_WIDTH), BF16),
                   jax.ShapeDtypeStruct((bs, TOKENS_PADDED, DIFF_WIDTH), BF16)),
        compiler_params=pltpu.CompilerParams(
            dimension_semantics=("arbitrary", "arbitrary", "arbitrary"),
            vmem_limit_bytes=VMEM_LIMIT_V7X),
        name="diff_attn",
    )(page_ids, rel_bias, lam_init, qt, kb, vt, gdp, *lam_vecs, subln, qs, kn, vn, gds,
      *([cache_k] * P), *([cache_v] * P))


def _layer_params(l, pre_norm, post_norm, w_in, mu_shift, w0, w_up, a0, a_up, k_k, k_a, r_k,
                  lnx_w, lnx_b, lam_q1, lam_k1, lam_q2, lam_k2, subln, w_out):
    row = lambda a: a.reshape(1, -1)
    zeros = jnp.zeros((LORA, RWKV_WIDTH), F32)
    head_of = jnp.arange(GROUP_W) // HEAD
    return dict(
        pre_norm=row(pre_norm[l]), post_norm=row(post_norm[l]),
        w_in=w_in[l].astype(BF16), w_out=w_out[l].astype(BF16),
        mu=row(mu_shift[l]), w0=row(w0[l]), a0=row(a0[l]),
        w1=jnp.concatenate([w_up[l], zeros], axis=0).astype(BF16),
        w2=jnp.concatenate([zeros, a_up[l]], axis=0).astype(BF16),
        k_k=row(k_k[l]), k_a=row(k_a[l]), r_k=row(r_k[l]),
        lnx_w=row(lnx_w[l]), lnx_b=row(lnx_b[l]),
        e_ones=(head_of[:, None] == head_of[None, :]).astype(BF16),
        lam_vecs=(row(lam_q1[l]), row(lam_k1[l]), row(lam_q2[l]), row(lam_k2[l])),
        subln=row(subln[l]),
        lam_init=jnp.full((1,), 0.8 - 0.6 * math.exp(-0.3 * l), F32),
    )


def _rwkv_stream(zsh, gr, batch, p, shift_prev, s0, chunk, t_valid, seqs_per_step,
                 chunks_per_step=1):
    t = zsh.shape[0] // batch
    return _rwkv(zsh.reshape(batch, t, SHIFT_WIDTH), gr.reshape(batch, t, 512), shift_prev, s0, p,
                 chunk, t_valid, seqs_per_step, chunks_per_step)


def kernel(x_prompt, x_sample, cache_k, cache_v, state_wkv, state_shift, page_table, rel_bias,
           pre_norm, post_norm, w_in, mu_shift, w0, w_up, a0, a_up, k_k, k_a, r_k, lnx_w, lnx_b,
           lam_q1, lam_k1, lam_q2, lam_k2, subln, w_out):
    return _forward(x_prompt, x_sample, cache_k, cache_v, state_wkv, state_shift, page_table,
                    rel_bias, pre_norm, post_norm, w_in, mu_shift, w0, w_up, a0, a_up, k_k, k_a,
                    r_k, lnx_w, lnx_b, lam_q1, lam_k1, lam_q2, lam_k2, subln, w_out,
                    PROMPT_TM=512, PROMPT_TM_OUT=1024, PROMPT_CHUNK=64, PROMPT_TQ=512,
                    PROMPT_TK=256, SAMPLE_SEQS_PER_STEP=8, PROMPT_CHUNKS_PER_STEP=2)


def _forward(x_prompt, x_sample, cache_k, cache_v, state_wkv, state_shift, page_table, rel_bias,
             pre_norm, post_norm, w_in, mu_shift, w0, w_up, a0, a_up, k_k, k_a, r_k, lnx_w, lnx_b,
             lam_q1, lam_k1, lam_q2, lam_k2, subln, w_out, *, PROMPT_TM, PROMPT_TM_OUT,
             PROMPT_CHUNK, PROMPT_TQ, PROMPT_TK, SAMPLE_SEQS_PER_STEP, PROMPT_CHUNKS_PER_STEP):
    bp, tp, d = x_prompt.shape
    bs, ts, _ = x_sample.shape
    depth = w_in.shape[0]
    n_pool = cache_k.shape[1]
    n_pages = page_table.shape[1]
    TS_PAD = TOKENS_PADDED

    xp = x_prompt.reshape(bp * tp, d)
    xs = jnp.pad(x_sample, ((0, 0), (0, TS_PAD - ts), (0, 0))).reshape(bs * TS_PAD, d)
    ck = cache_k.reshape(depth * n_pool, PAGE_ROWS, DIFF_V)
    cv = cache_v.reshape(depth * n_pool, PAGE_ROWS, DIFF_V)
    zero_shift = jnp.zeros((bp, 1, SHIFT_WIDTH), F32)
    zero_state = jnp.zeros((bp, RWKV_HEADS, HEAD, HEAD), F32)

    outs = {name: [] for name in ("wp", "sp", "ks", "vs", "ws", "ss")}
    kv_rows = None
    for l in range(depth):
        p = _layer_params(l, pre_norm, post_norm, w_in, mu_shift, w0, w_up, a0, a_up, k_k, k_a,
                          r_k, lnx_w, lnx_b, lam_q1, lam_k1, lam_q2, lam_k2, subln, w_out)

        zsh_p, gr_p, qt, k_rows, v_rows, gd_p, kb, vt = _inproj_prompt(
            xp, p["pre_norm"], p["w_in"], PROMPT_TM, l, depth, kv_rows)
        kv_rows = (k_rows, v_rows)
        zsh_s, gr_s, q_s, k_s, v_s, gd_s = _inproj_sample(xs, p["pre_norm"], p["w_in"],
                                                          bs * TS_PAD)
        yr_p, wp, sp = _rwkv_stream(zsh_p, gr_p, bp, p, zero_shift, zero_state, PROMPT_CHUNK,
                                    PROMPT_CHUNK, bp, PROMPT_CHUNKS_PER_STEP)
        yr_s, ws, ss = _rwkv_stream(zsh_s, gr_s, bs, p, state_shift[l][:, None, :], state_wkv[l],
                                    TS_PAD, ts, SAMPLE_SEQS_PER_STEP)
        page_ids = (page_table + l * n_pool).reshape(-1).astype(jnp.int32)
        r3 = lambda a: a.reshape(bs, TS_PAD, DIFF_WIDTH)
        rows = lambda a: a.reshape(bs, TS_PAD * DIFF_HEADS, DIFF_V)
        yd_p, yd_s = _attn_fused(qt, kb, vt, gd_p, r3(q_s), rows(k_s), rows(v_s), r3(gd_s), ck, cv,
                                 page_ids, rel_bias, p["lam_init"], p["lam_vecs"], p["subln"], bp,
                                 PROMPT_TQ, PROMPT_TK)
        xp = _outproj(yr_p.reshape(bp * tp, RWKV_WIDTH), yd_p, p["w_out"], p["post_norm"], xp,
                      PROMPT_TM_OUT)
        xs = _outproj(yr_s.reshape(bs * TS_PAD, RWKV_WIDTH), yd_s.reshape(bs * TS_PAD, DIFF_WIDTH),
                      p["w_out"], p["post_norm"], xs, bs * TS_PAD)
        outs["wp"].append(wp)
        outs["sp"].append(sp.reshape(bp, SHIFT_WIDTH))
        outs["ks"].append(k_s.reshape(bs, TS_PAD, DIFF_HEADS, DIFF_V)[:, :ts])
        outs["vs"].append(v_s.reshape(bs, TS_PAD, DIFF_HEADS, DIFF_V)[:, :ts])
        outs["ws"].append(ws)
        outs["ss"].append(ss.reshape(bs, SHIFT_WIDTH))

    st = lambda name: jnp.stack(outs[name])
    kv_shape = (depth, bp, tp, DIFF_HEADS, DIFF_V)
    return (xp.reshape(bp, tp, d), xs.reshape(bs, TS_PAD, d)[:, :ts], kv_rows[0].reshape(kv_shape),
            kv_rows[1].reshape(kv_shape), st("wp"), st("sp"), st("ks"), st("vs"), st("ws"),
            st("ss"))
```
